```python
import jax, jax.numpy as jnp
from jax import lax
import numpy as np

D_MODEL = 2048
BATCH = 1
SEQ = 8192
DEPTH = 1

N_HEADS = 16
N_KV_HEADS = 4
HEAD_DIM = D_MODEL // N_HEADS
GROUP = N_HEADS // N_KV_HEADS
Q_BLOCK = 128
ROPE_THETA = 10000.0
AXIS_DIM = HEAD_DIM // 2
GRID_W = 64
CONV_WIDTH = D_MODEL // 2
CONV_KERNEL = 31
D_FF = 4 * D_MODEL
PLE_DIM = 256
N_BRANCHES = 2
EPS = 1e-6

Q_W = N_HEADS * HEAD_DIM
KV_W = N_KV_HEADS * HEAD_DIM
IN_W = 2 * CONV_WIDTH + Q_W + 2 * KV_W + N_BRANCHES * D_MODEL

kernel_name = "hybrid_conformer_gqa_axial_gated_encoder_block"


def rms_norm(x, g):
    xf = x.astype(jnp.float32)
    y = xf * lax.rsqrt(jnp.mean(xf * xf, axis=-1, keepdims=True) + EPS)
    return (y * g.astype(jnp.float32)).astype(x.dtype)


def layer_norm(x, g, b):
    xf = x.astype(jnp.float32)
    mu = jnp.mean(xf, axis=-1, keepdims=True)
    xc = xf - mu
    y = xc * lax.rsqrt(jnp.mean(xc * xc, axis=-1, keepdims=True) + EPS)
    return (y * g.astype(jnp.float32) + b.astype(jnp.float32)).astype(x.dtype)


def rope_half(x, ang):
    n = ang.shape[-1]
    cos = jnp.cos(ang)[None, :, None, :].astype(x.dtype)
    sin = jnp.sin(ang)[None, :, None, :].astype(x.dtype)
    x1, x2 = x[..., :n], x[..., n:]
    return jnp.concatenate([x1 * cos - x2 * sin, x2 * cos + x1 * sin], axis=-1)


def axial_rope(x, ang_row, ang_col):
    return jnp.concatenate([rope_half(x[..., :AXIS_DIM], ang_row),
                            rope_half(x[..., AXIS_DIM:], ang_col)], axis=-1)


def conformer_branch(u_a, u_b, w_dw, ln_g, ln_b, w_proj):
    u = u_a * jax.nn.sigmoid(u_b)
    u = lax.conv_general_dilated(
        u, w_dw[:, None, :].astype(u.dtype), window_strides=(1,),
        padding=[(CONV_KERNEL // 2, CONV_KERNEL // 2)],
        dimension_numbers=("NWC", "WIO", "NWC"),
        feature_group_count=CONV_WIDTH)
    u = jax.nn.silu(layer_norm(u, ln_g, ln_b))
    return u @ w_proj


def attention_branch(q, k, v, q_g, k_g, w_proj):
    B, S, _ = q.shape
    rows = S // GRID_W
    row = jnp.repeat(jnp.arange(rows, dtype=jnp.int32), GRID_W)
    col = jnp.tile(jnp.arange(GRID_W, dtype=jnp.int32), rows)
    inv_freq = ROPE_THETA ** (-jnp.arange(0, AXIS_DIM, 2, dtype=jnp.float32) / AXIS_DIM)
    ang_row = row.astype(jnp.float32)[:, None] * inv_freq[None, :]
    ang_col = col.astype(jnp.float32)[:, None] * inv_freq[None, :]

    q = q.reshape(B, S, N_HEADS, HEAD_DIM)
    k = k.reshape(B, S, N_KV_HEADS, HEAD_DIM)
    v = v.reshape(B, S, N_KV_HEADS, HEAD_DIM)
    q = axial_rope(rms_norm(q, q_g), ang_row, ang_col)
    k = axial_rope(rms_norm(k, k_g), ang_row, ang_col)

    n_blk = S // Q_BLOCK
    qb = q.reshape(B, n_blk, Q_BLOCK, N_KV_HEADS, GROUP, HEAD_DIM)
    qb = qb.transpose(1, 0, 3, 4, 2, 5)
    kt = k.transpose(0, 2, 1, 3)
    vt = v.transpose(0, 2, 1, 3)
    scale = HEAD_DIM ** -0.5

    def one_block(qblk):
        s = jnp.einsum("bkgqd,bksd->bkgqs", qblk, kt).astype(jnp.float32) * scale
        pr = jax.nn.softmax(s, axis=-1).astype(vt.dtype)
        return jnp.einsum("bkgqs,bksd->bkgqd", pr, vt)

    o = lax.map(one_block, qb)
    o = o.transpose(1, 0, 4, 2, 3, 5).reshape(B, S, Q_W)
    return o @ w_proj


def setup_inputs(seed: int = 0) -> dict:
    key = jax.random.key(seed)
    ks = jax.random.split(key, 24)
    f32 = jnp.float32

    def nrm(k, shape, scale):
        return jax.random.normal(k, shape, f32) * scale

    def gain(k, shape):
        return 1.0 + 0.02 * jax.random.normal(k, shape, f32)

    return {
        "x": nrm(ks[0], (BATCH, SEQ, D_MODEL), 1.0),
        "p": nrm(ks[1], (DEPTH, BATCH, SEQ, PLE_DIM), 1.0),
        "norm_mix": gain(ks[2], (DEPTH, D_MODEL)),
        "w_in": nrm(ks[3], (DEPTH, D_MODEL, IN_W), D_MODEL ** -0.5),
        "w_dw": nrm(ks[4], (DEPTH, CONV_KERNEL, CONV_WIDTH), CONV_KERNEL ** -0.5),
        "conv_ln_g": gain(ks[5], (DEPTH, CONV_WIDTH)),
        "conv_ln_b": nrm(ks[6], (DEPTH, CONV_WIDTH), 0.02),
        "w_conv_proj": nrm(ks[7], (DEPTH, CONV_WIDTH, D_MODEL), CONV_WIDTH ** -0.5),
        "q_norm": gain(ks[8], (DEPTH, HEAD_DIM)),
        "k_norm": gain(ks[9], (DEPTH, HEAD_DIM)),
        "w_attn_proj": nrm(ks[10], (DEPTH, Q_W, D_MODEL), Q_W ** -0.5),
        "w_out": nrm(ks[11], (DEPTH, D_MODEL, D_MODEL), D_MODEL ** -0.5),
        "norm_ffn": gain(ks[12], (DEPTH, D_MODEL)),
        "w_ff1": nrm(ks[13], (DEPTH, D_MODEL, D_FF), D_MODEL ** -0.5),
        "w_ff2": nrm(ks[14], (DEPTH, D_FF, D_MODEL), D_FF ** -0.5),
        "norm_ple": gain(ks[15], (DEPTH, D_MODEL)),
        "w_ple_gate": nrm(ks[16], (DEPTH, D_MODEL, D_MODEL), D_MODEL ** -0.5),
        "w_ple_proj": nrm(ks[17], (DEPTH, PLE_DIM, D_MODEL), PLE_DIM ** -0.5),
        "norm_final": gain(ks[18], (D_MODEL,)),
    }


def reference(x, p, norm_mix, w_in, w_dw, conv_ln_g, conv_ln_b, w_conv_proj,
              q_norm, k_norm, w_attn_proj, w_out, norm_ffn, w_ff1, w_ff2,
              norm_ple, w_ple_gate, w_ple_proj, norm_final):
    split_at = list(np.cumsum([CONV_WIDTH, CONV_WIDTH, Q_W, KV_W, KV_W, D_MODEL]))
    for i in range(DEPTH):
        h = rms_norm(x, norm_mix[i])
        z = h @ w_in[i]
        c_a, c_b, q, k, v, g_c, g_a = jnp.split(z, split_at, axis=-1)
        y_c = conformer_branch(c_a, c_b, w_dw[i], conv_ln_g[i], conv_ln_b[i], w_conv_proj[i])
        y_a = attention_branch(q, k, v, q_norm[i], k_norm[i], w_attn_proj[i])
        merged = jax.nn.sigmoid(g_c) * y_c + jax.nn.sigmoid(g_a) * y_a
        x = x + merged @ w_out[i]
        h = rms_norm(x, norm_ffn[i])
        x = x + jnp.square(jax.nn.relu(h @ w_ff1[i])) @ w_ff2[i]
        gate = jax.nn.sigmoid(rms_norm(x, norm_ple[i]) @ w_ple_gate[i])
        x = x + gate * (p[i] @ w_ple_proj[i])
    return rms_norm(x, norm_final)
```

```python
import functools
import math

import jax
import jax.numpy as jnp
from jax import lax
from jax.experimental import pallas as pl
from jax.experimental.pallas import tpu as pltpu

D_MODEL = 2048
SEQ = 8192
N_HEADS = 16
N_KV_HEADS = 4
HEAD_DIM = 128
GROUP = N_HEADS // N_KV_HEADS
ROPE_THETA = 10000.0
AXIS_DIM = HEAD_DIM // 2
GRID_W = 64
CONV_WIDTH = D_MODEL // 2
CONV_KERNEL = 31
CONV_HALO = 16
D_FF = 4 * D_MODEL
PLE_DIM = 256
EPS = 1e-6
Q_W = N_HEADS * HEAD_DIM
KV_W = N_KV_HEADS * HEAD_DIM

OFF_CA = 0
OFF_CB = CONV_WIDTH
OFF_Q = 2 * CONV_WIDTH
OFF_K = OFF_Q + Q_W
OFF_V = OFF_K + KV_W
OFF_G = OFF_V + KV_W

BF16 = jnp.bfloat16
F32 = jnp.float32
MIB = 1024 * 1024


def _params(sem, vmem_mib):
    return pltpu.CompilerParams(dimension_semantics=sem, vmem_limit_bytes=vmem_mib * MIB)


def _sigmoid(v):
    return 1.0 / (1.0 + jnp.exp(-v))


def _rms_rows(v, g):
    ms = jnp.mean(v * v, axis=-1, keepdims=True)
    return v * lax.rsqrt(ms + EPS) * g


def _rmsnorm_kernel(x_ref, g_ref, o_ref):
    o_ref[...] = _rms_rows(x_ref[...], g_ref[...]).astype(o_ref.dtype)


def _rmsnorm_cast(x, g, tm=512):
    s, d = x.shape
    return pl.pallas_call(
        _rmsnorm_kernel,
        grid=(s // tm,),
        in_specs=[pl.BlockSpec((tm, d), lambda i: (i, 0)),
                  pl.BlockSpec((1, d), lambda i: (0, 0))],
        out_specs=pl.BlockSpec((tm, d), lambda i: (i, 0)),
        out_shape=jax.ShapeDtypeStruct((s, d), BF16),
        compiler_params=_params(("parallel",), 32),
        name="rmsnorm_cast",
    )(x, g.reshape(1, d))


def _glu_kernel(h_ref, wa_ref, wb_ref, o_ref):
    h = h_ref[...]
    a = jnp.dot(h, wa_ref[...], preferred_element_type=F32)
    b = jnp.dot(h, wb_ref[...], preferred_element_type=F32)
    o_ref[...] = a * _sigmoid(b)


def _glu_proj(h, w, tm=1024, tn=512):
    s, d = h.shape
    nb = CONV_WIDTH // tn
    return pl.pallas_call(
        _glu_kernel,
        grid=(nb, s // tm),
        in_specs=[pl.BlockSpec((tm, d), lambda j, i: (i, 0)),
                  pl.BlockSpec((d, tn), lambda j, i: (0, OFF_CA // tn + j)),
                  pl.BlockSpec((d, tn), lambda j, i: (0, OFF_CB // tn + j))],
        out_specs=pl.BlockSpec((tm, tn), lambda j, i: (i, j)),
        out_shape=jax.ShapeDtypeStruct((s, CONV_WIDTH), F32),
        compiler_params=_params(("parallel", "parallel"), 40),
        name="glu_proj",
    )(h, w, w)


def _qk_kernel(h_ref, w_ref, g_ref, cos_ref, sa_ref, sb_ref, o_ref, *, scale):
    z = jnp.dot(h_ref[...], w_ref[...], preferred_element_type=F32)
    g = g_ref[...]
    cos = cos_ref[...]
    sa = sa_ref[...]
    sb = sb_ref[...]
    for hh in range(z.shape[1] // HEAD_DIM):
        zh = z[:, hh * HEAD_DIM:(hh + 1) * HEAD_DIM]
        ms = jnp.mean(zh * zh, axis=-1, keepdims=True)
        y = zh * lax.rsqrt(ms + EPS) * g
        up = pltpu.roll(y, HEAD_DIM - AXIS_DIM // 2, 1)
        dn = pltpu.roll(y, AXIS_DIM // 2, 1)
        r = y * cos + up * sa + dn * sb
        if scale != 1.0:
            r = r * scale
        o_ref[:, hh * HEAD_DIM:(hh + 1) * HEAD_DIM] = r.astype(o_ref.dtype)


def _qk_proj(h, w, gain, tabs, col_off, width, scale, tm=1024):
    s, d = h.shape
    tn = min(width, 1024)
    cos_t, sa_t, sb_t = tabs
    tab_spec = pl.BlockSpec((tm, HEAD_DIM), lambda j, i: (i, 0))
    return pl.pallas_call(
        functools.partial(_qk_kernel, scale=scale),
        grid=(width // tn, s // tm),
        in_specs=[pl.BlockSpec((tm, d), lambda j, i: (i, 0)),
                  pl.BlockSpec((d, tn), lambda j, i: (0, col_off // tn + j)),
                  pl.BlockSpec((1, HEAD_DIM), lambda j, i: (0, 0)),
                  tab_spec, tab_spec, tab_spec],
        out_specs=pl.BlockSpec((tm, tn), lambda j, i: (i, j)),
        out_shape=jax.ShapeDtypeStruct((s, width), BF16),
        compiler_params=_params(("parallel", "parallel"), 40),
        name="qk_proj",
    )(h, w, gain.reshape(1, HEAD_DIM), cos_t, sa_t, sb_t)


def _act_kernel(h_ref, w_ref, o_ref, *, act):
    z = jnp.dot(h_ref[...], w_ref[...], preferred_element_type=F32)
    if act == "sigmoid":
        z = _sigmoid(z)
    elif act == "relu2":
        z = jnp.square(jnp.maximum(z, 0.0))
    o_ref[...] = z.astype(o_ref.dtype)


def _act_proj(h, w, col_off, width, act, name, tm=1024):
    s, d = h.shape
    tn = min(width, 1024)
    return pl.pallas_call(
        functools.partial(_act_kernel, act=act),
        grid=(width // tn, s // tm),
        in_specs=[pl.BlockSpec((tm, d), lambda j, i: (i, 0)),
                  pl.BlockSpec((d, tn), lambda j, i: (0, col_off // tn + j))],
        out_specs=pl.BlockSpec((tm, tn), lambda j, i: (i, j)),
        out_shape=jax.ShapeDtypeStruct((s, width), BF16),
        compiler_params=_params(("parallel", "parallel"), 40),
        name=name,
    )(h, w)


def _conv_kernel(u_ref, up_ref, un_ref, wdw_ref, lng_ref, lnb_ref, wp_ref, gate_ref,
                 o_ref, buf_ref, cv_ref, *, tm, rows, lanes):
    i = pl.program_id(0)
    last = pl.num_programs(0) - 1
    prev_ok = (i > 0).astype(F32)
    next_ok = (i < last).astype(F32)
    buf_ref[0:CONV_HALO, :] = up_ref[...] * prev_ok
    buf_ref[CONV_HALO:CONV_HALO + tm, :] = u_ref[...]
    buf_ref[CONV_HALO + tm:, :] = un_ref[...] * next_ok

    base = CONV_HALO - CONV_KERNEL // 2
    for c in range(0, CONV_WIDTH, lanes):
        def chunk(r, carry, c=c):
            r0 = pl.multiple_of(r * rows, rows)
            win = buf_ref[pl.ds(r0, rows + 2 * CONV_HALO), c:c + lanes]
            acc = jnp.zeros((rows, lanes), F32)
            for res in range(8):
                taps = [k for k in range(CONV_KERNEL) if (base + k) % 8 == res]
                span = max(taps) + base - res + rows
                shifted = win[res:res + span, :]
                for k in taps:
                    off = base + k - res
                    acc = acc + shifted[off:off + rows, :] * wdw_ref[k:k + 1, c:c + lanes]
            cv_ref[pl.ds(r0, rows), c:c + lanes] = acc
            return carry
        lax.fori_loop(0, tm // rows, chunk, 0)

    cv = cv_ref[...]
    mu = jnp.mean(cv, axis=-1, keepdims=True)
    xc = cv - mu
    var = jnp.mean(xc * xc, axis=-1, keepdims=True)
    y = xc * lax.rsqrt(var + EPS) * lng_ref[...] + lnb_ref[...]
    y = y * _sigmoid(y)
    yc = jnp.dot(y.astype(BF16), wp_ref[...], preferred_element_type=F32)
    o_ref[...] = gate_ref[...].astype(F32) * yc


def _conv_branch(u, w_dw, ln_g, ln_b, w_proj, gates, tm=512, rows=64, lanes=256):
    s = u.shape[0]
    hb = tm // CONV_HALO
    n_hblk = s // CONV_HALO
    return pl.pallas_call(
        functools.partial(_conv_kernel, tm=tm, rows=rows, lanes=lanes),
        grid=(s // tm,),
        in_specs=[pl.BlockSpec((tm, CONV_WIDTH), lambda i: (i, 0)),
                  pl.BlockSpec((CONV_HALO, CONV_WIDTH), lambda i: (jnp.maximum(i * hb - 1, 0), 0)),
                  pl.BlockSpec((CONV_HALO, CONV_WIDTH), lambda i: (jnp.minimum((i + 1) * hb, n_hblk - 1), 0)),
                  pl.BlockSpec((CONV_KERNEL, CONV_WIDTH), lambda i: (0, 0)),
                  pl.BlockSpec((1, CONV_WIDTH), lambda i: (0, 0)),
                  pl.BlockSpec((1, CONV_WIDTH), lambda i: (0, 0)),
                  pl.BlockSpec((CONV_WIDTH, D_MODEL), lambda i: (0, 0)),
                  pl.BlockSpec((tm, D_MODEL), lambda i: (i, 0))],
        out_specs=pl.BlockSpec((tm, D_MODEL), lambda i: (i, 0)),
        out_shape=jax.ShapeDtypeStruct((s, D_MODEL), F32),
        scratch_shapes=[pltpu.VMEM((tm + 2 * CONV_HALO, CONV_WIDTH), F32),
                        pltpu.VMEM((tm, CONV_WIDTH), F32)],
        compiler_params=_params(("parallel",), 48),
        name="conv_branch",
    )(u, u, u, w_dw, ln_g.reshape(1, -1), ln_b.reshape(1, -1), w_proj, gates)


def _attn_kernel(q_ref, k_ref, v_ref, o_ref, qs_ref, m_ref, l_ref, acc_ref, *, tq, tk):
    for hh in range(GROUP):
        qs_ref[hh * tq:(hh + 1) * tq, :] = q_ref[:, hh * HEAD_DIM:(hh + 1) * HEAD_DIM]
    m_ref[...] = jnp.full(m_ref.shape, -1e30, F32)
    l_ref[...] = jnp.zeros(l_ref.shape, F32)
    acc_ref[...] = jnp.zeros(acc_ref.shape, F32)
    q = qs_ref[...]

    def body(c, carry):
        c0 = pl.multiple_of(c * tk, tk)
        kc = k_ref[pl.ds(c0, tk), :]
        vc = v_ref[pl.ds(c0, tk), :]
        s = lax.dot_general(q, kc, (((1,), (1,)), ((), ())), preferred_element_type=F32)
        m_prev = m_ref[...]
        m_new = jnp.maximum(m_prev, jnp.max(s, axis=1, keepdims=True))
        alpha = jnp.exp2(m_prev - m_new)
        p = jnp.exp2(s - m_new)
        l_ref[...] = alpha * l_ref[...] + jnp.sum(p, axis=1, keepdims=True)
        acc_ref[...] = alpha * acc_ref[...] + jnp.dot(p.astype(BF16), vc, preferred_element_type=F32)
        m_ref[...] = m_new
        return carry

    lax.fori_loop(0, k_ref.shape[0] // tk, body, 0)
    o = acc_ref[...] / l_ref[...]
    for hh in range(GROUP):
        o_ref[:, hh * HEAD_DIM:(hh + 1) * HEAD_DIM] = o[hh * tq:(hh + 1) * tq, :].astype(o_ref.dtype)


def _attention(q, k, v, tq=256, tk=512):
    s = q.shape[0]
    gw = GROUP * HEAD_DIM
    m = GROUP * tq
    return pl.pallas_call(
        functools.partial(_attn_kernel, tq=tq, tk=tk),
        grid=(N_KV_HEADS, s // tq),
        in_specs=[pl.BlockSpec((tq, gw), lambda g, i: (i, g)),
                  pl.BlockSpec((s, HEAD_DIM), lambda g, i: (0, g)),
                  pl.BlockSpec((s, HEAD_DIM), lambda g, i: (0, g))],
        out_specs=pl.BlockSpec((tq, gw), lambda g, i: (i, g)),
        out_shape=jax.ShapeDtypeStruct((s, Q_W), BF16),
        scratch_shapes=[pltpu.VMEM((m, HEAD_DIM), BF16),
                        pltpu.VMEM((m, 1), F32),
                        pltpu.VMEM((m, 1), F32),
                        pltpu.VMEM((m, HEAD_DIM), F32)],
        compiler_params=_params(("parallel", "parallel"), 40),
        name="gqa_flash",
    )(q, k, v)


def _merge_kernel(o_ref, w_ref, mc_ref, ga_ref, out_ref):
    ya = jnp.dot(o_ref[...], w_ref[...], preferred_element_type=F32)
    out_ref[...] = (mc_ref[...] + ga_ref[...].astype(F32) * ya).astype(out_ref.dtype)


def _merge(o, w_ap, m_c, gates, tm=1024, tn=1024):
    s, d = o.shape
    ga_off = D_MODEL // tn
    return pl.pallas_call(
        _merge_kernel,
        grid=(D_MODEL // tn, s // tm),
        in_specs=[pl.BlockSpec((tm, d), lambda j, i: (i, 0)),
                  pl.BlockSpec((d, tn), lambda j, i: (0, j)),
                  pl.BlockSpec((tm, tn), lambda j, i: (i, j)),
                  pl.BlockSpec((tm, tn), lambda j, i: (i, ga_off + j))],
        out_specs=pl.BlockSpec((tm, tn), lambda j, i: (i, j)),
        out_shape=jax.ShapeDtypeStruct((s, D_MODEL), BF16),
        compiler_params=_params(("parallel", "parallel"), 48),
        name="merge_attn_proj",
    )(o, w_ap, m_c, gates)


def _out_kernel(a_ref, w_ref, x_ref, g_ref, x1_ref, h_ref):
    x1 = x_ref[...] + jnp.dot(a_ref[...], w_ref[...], preferred_element_type=F32)
    x1_ref[...] = x1
    h_ref[...] = _rms_rows(x1, g_ref[...]).astype(h_ref.dtype)


def _out_proj(a, w, x, g, tm=256):
    s, d = x.shape
    return pl.pallas_call(
        _out_kernel,
        grid=(s // tm,),
        in_specs=[pl.BlockSpec((tm, d), lambda i: (i, 0)),
                  pl.BlockSpec((d, d), lambda i: (0, 0)),
                  pl.BlockSpec((tm, d), lambda i: (i, 0)),
                  pl.BlockSpec((1, d), lambda i: (0, 0))],
        out_specs=[pl.BlockSpec((tm, d), lambda i: (i, 0)),
                   pl.BlockSpec((tm, d), lambda i: (i, 0))],
        out_shape=[jax.ShapeDtypeStruct((s, d), F32),
                   jax.ShapeDtypeStruct((s, d), BF16)],
        compiler_params=_params(("parallel",), 48),
        name="out_proj_residual",
    )(a, w, x, g.reshape(1, d))


def _ffn2_kernel(a_ref, w_ref, x_ref, o_ref, acc_ref):
    kk = pl.program_id(2)

    @pl.when(kk == 0)
    def _():
        acc_ref[...] = x_ref[...]

    acc_ref[...] += jnp.dot(a_ref[...], w_ref[...], preferred_element_type=F32)

    @pl.when(kk == pl.num_programs(2) - 1)
    def _():
        o_ref[...] = acc_ref[...]


def _ffn2(a, w, x, tm=1024, tn=1024, tk=2048):
    s, kdim = a.shape
    d = x.shape[1]
    return pl.pallas_call(
        _ffn2_kernel,
        grid=(s // tm, d // tn, kdim // tk),
        in_specs=[pl.BlockSpec((tm, tk), lambda i, j, k: (i, k)),
                  pl.BlockSpec((tk, tn), lambda i, j, k: (k, j)),
                  pl.BlockSpec((tm, tn), lambda i, j, k: (i, j))],
        out_specs=pl.BlockSpec((tm, tn), lambda i, j, k: (i, j)),
        out_shape=jax.ShapeDtypeStruct((s, d), F32),
        scratch_shapes=[pltpu.VMEM((tm, tn), F32)],
        compiler_params=_params(("parallel", "parallel", "arbitrary"), 48),
        name="ffn_down_residual",
    )(a, w, x)


def _ple_kernel(x_ref, p_ref, gp_ref, wg_ref, wp_ref, gf_ref, o_ref):
    x = x_ref[...]
    h = _rms_rows(x, gp_ref[...]).astype(BF16)
    gate = _sigmoid(jnp.dot(h, wg_ref[...], preferred_element_type=F32))
    pp = jnp.dot(p_ref[...].astype(BF16), wp_ref[...], preferred_element_type=F32)
    x3 = x + gate * pp
    o_ref[...] = _rms_rows(x3, gf_ref[...])


def _ple_final(x, p, g_ple, w_gate, w_proj, g_final, tm=256):
    s, d = x.shape
    return pl.pallas_call(
        _ple_kernel,
        grid=(s // tm,),
        in_specs=[pl.BlockSpec((tm, d), lambda i: (i, 0)),
                  pl.BlockSpec((tm, PLE_DIM), lambda i: (i, 0)),
                  pl.BlockSpec((1, d), lambda i: (0, 0)),
                  pl.BlockSpec((d, d), lambda i: (0, 0)),
                  pl.BlockSpec((PLE_DIM, d), lambda i: (0, 0)),
                  pl.BlockSpec((1, d), lambda i: (0, 0))],
        out_specs=pl.BlockSpec((tm, d), lambda i: (i, 0)),
        out_shape=jax.ShapeDtypeStruct((s, d), F32),
        compiler_params=_params(("parallel",), 48),
        name="ple_final_norm",
    )(x, p, g_ple.reshape(1, d), w_gate, w_proj, g_final.reshape(1, d))


def _rope_tables():
    t = jnp.arange(SEQ, dtype=jnp.int32)
    row = (t // GRID_W).astype(F32)
    col = (t % GRID_W).astype(F32)
    inv_freq = ROPE_THETA ** (-jnp.arange(0, AXIS_DIM, 2, dtype=F32) / AXIS_DIM)
    ang_row = row[:, None] * inv_freq[None, :]
    ang_col = col[:, None] * inv_freq[None, :]
    cr, sr = jnp.cos(ang_row), jnp.sin(ang_row)
    cc, sc = jnp.cos(ang_col), jnp.sin(ang_col)
    zero = jnp.zeros_like(sr)
    cos_t = jnp.concatenate([cr, cr, cc, cc], axis=-1)
    sin_up = jnp.concatenate([-sr, zero, -sc, zero], axis=-1)
    sin_dn = jnp.concatenate([zero, sr, zero, sc], axis=-1)
    return cos_t, sin_up, sin_dn


def kernel(x, p, norm_mix, w_in, w_dw, conv_ln_g, conv_ln_b, w_conv_proj, q_norm, k_norm,
           w_attn_proj, w_out, norm_ffn, w_ff1, w_ff2, norm_ple, w_ple_gate, w_ple_proj, norm_final):
    depth = w_in.shape[0]
    assert depth == 1, "the final norm is fused into the last layer's kernel"
    tabs = _rope_tables()
    q_scale = (HEAD_DIM ** -0.5) * math.log2(math.e)
    xs = x[0]
    for li in range(depth):
        w_in_b = w_in[li].astype(BF16)
        h = _rmsnorm_cast(xs, norm_mix[li])
        u = _glu_proj(h, w_in_b)
        q = _qk_proj(h, w_in_b, q_norm[li], tabs, OFF_Q, Q_W, q_scale)
        k = _qk_proj(h, w_in_b, k_norm[li], tabs, OFF_K, KV_W, 1.0)
        v = _act_proj(h, w_in_b, OFF_V, KV_W, "none", "v_proj")
        gates = _act_proj(h, w_in_b, OFF_G, 2 * D_MODEL, "sigmoid", "gate_proj")
        m_c = _conv_branch(u, w_dw[li], conv_ln_g[li], conv_ln_b[li],
                           w_conv_proj[li].astype(BF16), gates)
        o = _attention(q, k, v)
        merged = _merge(o, w_attn_proj[li].astype(BF16), m_c, gates)
        x1, h2 = _out_proj(merged, w_out[li].astype(BF16), xs, norm_ffn[li])
        a = _act_proj(h2, w_ff1[li].astype(BF16), 0, D_FF, "relu2", "ffn_up")
        x2 = _ffn2(a, w_ff2[li].astype(BF16), x1)
        xs = _ple_final(x2, p[li, 0], norm_ple[li], w_ple_gate[li].astype(BF16),
                        w_ple_proj[li].astype(BF16), norm_final)
    return xs[None]
```

```python
import functools
import math

import jax
import jax.numpy as jnp
from jax import lax
from jax.experimental import pallas as pl
from jax.experimental.pallas import tpu as pltpu

D_MODEL = 2048
SEQ = 8192
N_HEADS = 16
N_KV_HEADS = 4
HEAD_DIM = 128
GROUP = N_HEADS // N_KV_HEADS
ROPE_THETA = 10000.0
AXIS_DIM = HEAD_DIM // 2
GRID_W = 64
CONV_WIDTH = D_MODEL // 2
CONV_KERNEL = 31
CONV_HALO = 16
D_FF = 4 * D_MODEL
PLE_DIM = 256
EPS = 1e-6
Q_W = N_HEADS * HEAD_DIM
KV_W = N_KV_HEADS * HEAD_DIM

OFF_CA = 0
OFF_CB = CONV_WIDTH
OFF_Q = 2 * CONV_WIDTH
OFF_K = OFF_Q + Q_W
OFF_V = OFF_K + KV_W
OFF_G = OFF_V + KV_W

BF16 = jnp.bfloat16
F32 = jnp.float32
MIB = 1024 * 1024


def _params(sem, vmem_mib):
    return pltpu.CompilerParams(dimension_semantics=sem, vmem_limit_bytes=vmem_mib * MIB)


def _sigmoid(v):
    return 1.0 / (1.0 + jnp.exp(-v))


def _rms_rows(v, g):
    ms = jnp.mean(v * v, axis=-1, keepdims=True)
    return v * lax.rsqrt(ms + EPS) * g


def _rmsnorm_kernel(x_ref, g_ref, o_ref):
    o_ref[...] = _rms_rows(x_ref[...], g_ref[...]).astype(o_ref.dtype)


def _rmsnorm_cast(x, g, tm=512):
    s, d = x.shape
    return pl.pallas_call(
        _rmsnorm_kernel,
        grid=(s // tm,),
        in_specs=[pl.BlockSpec((tm, d), lambda i: (i, 0)),
                  pl.BlockSpec((1, d), lambda i: (0, 0))],
        out_specs=pl.BlockSpec((tm, d), lambda i: (i, 0)),
        out_shape=jax.ShapeDtypeStruct((s, d), BF16),
        compiler_params=_params(("parallel",), 32),
        name="rmsnorm_cast",
    )(x, g.reshape(1, d))


def _glu_kernel(h_ref, wa_ref, wb_ref, o_ref):
    h = h_ref[...]
    a = jnp.dot(h, wa_ref[...], preferred_element_type=F32)
    b = jnp.dot(h, wb_ref[...], preferred_element_type=F32)
    o_ref[...] = a * _sigmoid(b)


def _glu_proj(h, w, tm=1024, tn=512):
    s, d = h.shape
    nb = CONV_WIDTH // tn
    return pl.pallas_call(
        _glu_kernel,
        grid=(nb, s // tm),
        in_specs=[pl.BlockSpec((tm, d), lambda j, i: (i, 0)),
                  pl.BlockSpec((d, tn), lambda j, i: (0, OFF_CA // tn + j)),
                  pl.BlockSpec((d, tn), lambda j, i: (0, OFF_CB // tn + j))],
        out_specs=pl.BlockSpec((tm, tn), lambda j, i: (i, j)),
        out_shape=jax.ShapeDtypeStruct((s, CONV_WIDTH), F32),
        compiler_params=_params(("parallel", "parallel"), 40),
        name="glu_proj",
    )(h, w, w)


def _qk_kernel(h_ref, w_ref, g_ref, cos_ref, sa_ref, sb_ref, o_ref, *, scale, transpose_out):
    z = jnp.dot(h_ref[...], w_ref[...], preferred_element_type=F32)
    g = g_ref[...]
    cos = cos_ref[...]
    sa = sa_ref[...]
    sb = sb_ref[...]
    for hh in range(z.shape[1] // HEAD_DIM):
        zh = z[:, hh * HEAD_DIM:(hh + 1) * HEAD_DIM]
        ms = jnp.mean(zh * zh, axis=-1, keepdims=True)
        y = zh * lax.rsqrt(ms + EPS) * g
        up = pltpu.roll(y, HEAD_DIM - AXIS_DIM // 2, 1)
        dn = pltpu.roll(y, AXIS_DIM // 2, 1)
        r = y * cos + up * sa + dn * sb
        if scale != 1.0:
            r = r * scale
        if transpose_out:
            o_ref[hh * HEAD_DIM:(hh + 1) * HEAD_DIM, :] = r.T.astype(o_ref.dtype)
        else:
            o_ref[:, hh * HEAD_DIM:(hh + 1) * HEAD_DIM] = r.astype(o_ref.dtype)


def _qk_proj(h, w, gain, tabs, col_off, width, scale, transpose_out, tm=1024):
    s, d = h.shape
    tn = min(width, 1024)
    cos_t, sa_t, sb_t = tabs
    tab_spec = pl.BlockSpec((tm, HEAD_DIM), lambda j, i: (i, 0))
    if transpose_out:
        out_spec = pl.BlockSpec((tn, tm), lambda j, i: (j, i))
        out_shape = jax.ShapeDtypeStruct((width, s), BF16)
    else:
        out_spec = pl.BlockSpec((tm, tn), lambda j, i: (i, j))
        out_shape = jax.ShapeDtypeStruct((s, width), BF16)
    return pl.pallas_call(
        functools.partial(_qk_kernel, scale=scale, transpose_out=transpose_out),
        grid=(width // tn, s // tm),
        in_specs=[pl.BlockSpec((tm, d), lambda j, i: (i, 0)),
                  pl.BlockSpec((d, tn), lambda j, i: (0, col_off // tn + j)),
                  pl.BlockSpec((1, HEAD_DIM), lambda j, i: (0, 0)),
                  tab_spec, tab_spec, tab_spec],
        out_specs=out_spec,
        out_shape=out_shape,
        compiler_params=_params(("parallel", "parallel"), 40),
        name="qk_proj",
    )(h, w, gain.reshape(1, HEAD_DIM), cos_t, sa_t, sb_t)


def _vt_kernel(h_ref, w_ref, o_ref):
    z = jnp.dot(h_ref[...], w_ref[...], preferred_element_type=F32)
    for hh in range(N_KV_HEADS):
        o_ref[hh * HEAD_DIM:(hh + 1) * HEAD_DIM, :] = (
            z[:, hh * HEAD_DIM:(hh + 1) * HEAD_DIM].T.astype(o_ref.dtype))


def _vt_proj(h, w, tm=1024):
    s, d = h.shape
    return pl.pallas_call(
        _vt_kernel,
        grid=(s // tm,),
        in_specs=[pl.BlockSpec((tm, d), lambda i: (i, 0)),
                  pl.BlockSpec((d, KV_W), lambda i: (0, OFF_V // KV_W))],
        out_specs=pl.BlockSpec((KV_W, tm), lambda i: (0, i)),
        out_shape=jax.ShapeDtypeStruct((KV_W, s), BF16),
        compiler_params=_params(("parallel",), 40),
        name="vt_proj",
    )(h, w)


def _act_kernel(h_ref, w_ref, o_ref, *, act):
    z = jnp.dot(h_ref[...], w_ref[...], preferred_element_type=F32)
    if act == "sigmoid":
        z = _sigmoid(z)
    elif act == "relu2":
        z = jnp.square(jnp.maximum(z, 0.0))
    o_ref[...] = z.astype(o_ref.dtype)


def _act_proj(h, w, col_off, width, act, name, tm=1024):
    s, d = h.shape
    tn = min(width, 1024)
    return pl.pallas_call(
        functools.partial(_act_kernel, act=act),
        grid=(width // tn, s // tm),
        in_specs=[pl.BlockSpec((tm, d), lambda j, i: (i, 0)),
                  pl.BlockSpec((d, tn), lambda j, i: (0, col_off // tn + j))],
        out_specs=pl.BlockSpec((tm, tn), lambda j, i: (i, j)),
        out_shape=jax.ShapeDtypeStruct((s, width), BF16),
        compiler_params=_params(("parallel", "parallel"), 40),
        name=name,
    )(h, w)


def _conv_kernel(u_ref, up_ref, un_ref, wdw_ref, lng_ref, lnb_ref, wp_ref, gate_ref,
                 o_ref, buf_ref, cv_ref, *, tm, rows, lanes):
    i = pl.program_id(0)
    last = pl.num_programs(0) - 1
    prev_ok = (i > 0).astype(F32)
    next_ok = (i < last).astype(F32)
    buf_ref[0:CONV_HALO, :] = up_ref[...] * prev_ok
    buf_ref[CONV_HALO:CONV_HALO + tm, :] = u_ref[...]
    buf_ref[CONV_HALO + tm:, :] = un_ref[...] * next_ok

    base = CONV_HALO - CONV_KERNEL // 2
    for c in range(0, CONV_WIDTH, lanes):
        def chunk(r, carry, c=c):
            r0 = pl.multiple_of(r * rows, rows)
            win = buf_ref[pl.ds(r0, rows + 2 * CONV_HALO), c:c + lanes]
            acc = jnp.zeros((rows, lanes), F32)
            for res in range(8):
                taps = [k for k in range(CONV_KERNEL) if (base + k) % 8 == res]
                span = max(taps) + base - res + rows
                shifted = win[res:res + span, :]
                for k in taps:
                    off = base + k - res
                    acc = acc + shifted[off:off + rows, :] * wdw_ref[k:k + 1, c:c + lanes]
            cv_ref[pl.ds(r0, rows), c:c + lanes] = acc
            return carry
        lax.fori_loop(0, tm // rows, chunk, 0)

    cv = cv_ref[...]
    mu = jnp.mean(cv, axis=-1, keepdims=True)
    xc = cv - mu
    var = jnp.mean(xc * xc, axis=-1, keepdims=True)
    y = xc * lax.rsqrt(var + EPS) * lng_ref[...] + lnb_ref[...]
    y = y * _sigmoid(y)
    yc = jnp.dot(y.astype(BF16), wp_ref[...], preferred_element_type=F32)
    o_ref[...] = gate_ref[...].astype(F32) * yc


def _conv_branch(u, w_dw, ln_g, ln_b, w_proj, gates, tm=512, rows=64, lanes=256):
    s = u.shape[0]
    hb = tm // CONV_HALO
    n_hblk = s // CONV_HALO
    return pl.pallas_call(
        functools.partial(_conv_kernel, tm=tm, rows=rows, lanes=lanes),
        grid=(s // tm,),
        in_specs=[pl.BlockSpec((tm, CONV_WIDTH), lambda i: (i, 0)),
                  pl.BlockSpec((CONV_HALO, CONV_WIDTH), lambda i: (jnp.maximum(i * hb - 1, 0), 0)),
                  pl.BlockSpec((CONV_HALO, CONV_WIDTH), lambda i: (jnp.minimum((i + 1) * hb, n_hblk - 1), 0)),
                  pl.BlockSpec((CONV_KERNEL, CONV_WIDTH), lambda i: (0, 0)),
                  pl.BlockSpec((1, CONV_WIDTH), lambda i: (0, 0)),
                  pl.BlockSpec((1, CONV_WIDTH), lambda i: (0, 0)),
                  pl.BlockSpec((CONV_WIDTH, D_MODEL), lambda i: (0, 0)),
                  pl.BlockSpec((tm, D_MODEL), lambda i: (i, 0))],
        out_specs=pl.BlockSpec((tm, D_MODEL), lambda i: (i, 0)),
        out_shape=jax.ShapeDtypeStruct((s, D_MODEL), F32),
        scratch_shapes=[pltpu.VMEM((tm + 2 * CONV_HALO, CONV_WIDTH), F32),
                        pltpu.VMEM((tm, CONV_WIDTH), F32)],
        compiler_params=_params(("parallel",), 48),
        name="conv_branch",
    )(u, u, u, w_dw, ln_g.reshape(1, -1), ln_b.reshape(1, -1), w_proj, gates)


def _attn_kernel(qt_ref, k_ref, vt_ref, o_ref, qs_ref, acc_ref, *, tq, tk):
    m_cols = GROUP * tq
    for hh in range(GROUP):
        qs_ref[:, hh * tq:(hh + 1) * tq] = qt_ref[hh * HEAD_DIM:(hh + 1) * HEAD_DIM, :]
    acc_ref[...] = jnp.zeros(acc_ref.shape, F32)
    qt = qs_ref[...]

    def body(c, carry):
        m_prev, l_prev = carry
        c0 = pl.multiple_of(c * tk, tk)
        kc = k_ref[pl.ds(c0, tk), :]
        vtc = vt_ref[:, pl.ds(c0, tk)]
        s = jnp.dot(kc, qt, preferred_element_type=F32)
        m_new = jnp.maximum(m_prev, jnp.max(s, axis=0, keepdims=True))
        alpha = jnp.exp2(m_prev - m_new)
        p = jnp.exp2(s - m_new)
        l_new = alpha * l_prev + jnp.sum(p, axis=0, keepdims=True)
        acc_ref[...] = alpha * acc_ref[...] + jnp.dot(vtc, p.astype(BF16), preferred_element_type=F32)
        return m_new, l_new

    init = (jnp.full((1, m_cols), -1e30, F32), jnp.zeros((1, m_cols), F32))
    _, l_fin = lax.fori_loop(0, k_ref.shape[0] // tk, body, init)
    o_t = acc_ref[...] / l_fin
    for hh in range(GROUP):
        o_ref[:, hh * HEAD_DIM:(hh + 1) * HEAD_DIM] = o_t[:, hh * tq:(hh + 1) * tq].T.astype(o_ref.dtype)


def _attention(qt, k, vt, tq=256, tk=512):
    s = k.shape[0]
    gw = GROUP * HEAD_DIM
    m_cols = GROUP * tq
    return pl.pallas_call(
        functools.partial(_attn_kernel, tq=tq, tk=tk),
        grid=(N_KV_HEADS, s // tq),
        in_specs=[pl.BlockSpec((gw, tq), lambda g, i: (g, i)),
                  pl.BlockSpec((s, HEAD_DIM), lambda g, i: (0, g)),
                  pl.BlockSpec((HEAD_DIM, s), lambda g, i: (g, 0))],
        out_specs=pl.BlockSpec((tq, gw), lambda g, i: (i, g)),
        out_shape=jax.ShapeDtypeStruct((s, Q_W), BF16),
        scratch_shapes=[pltpu.VMEM((HEAD_DIM, m_cols), BF16),
                        pltpu.VMEM((HEAD_DIM, m_cols), F32)],
        compiler_params=_params(("parallel", "parallel"), 40),
        name="gqa_flash",
    )(qt, k, vt)


def _merge_kernel(o_ref, w_ref, mc_ref, ga_ref, out_ref):
    ya = jnp.dot(o_ref[...], w_ref[...], preferred_element_type=F32)
    out_ref[...] = (mc_ref[...] + ga_ref[...].astype(F32) * ya).astype(out_ref.dtype)


def _merge(o, w_ap, m_c, gates, tm=1024, tn=1024):
    s, d = o.shape
    ga_off = D_MODEL // tn
    return pl.pallas_call(
        _merge_kernel,
        grid=(D_MODEL // tn, s // tm),
        in_specs=[pl.BlockSpec((tm, d), lambda j, i: (i, 0)),
                  pl.BlockSpec((d, tn), lambda j, i: (0, j)),
                  pl.BlockSpec((tm, tn), lambda j, i: (i, j)),
                  pl.BlockSpec((tm, tn), lambda j, i: (i, ga_off + j))],
        out_specs=pl.BlockSpec((tm, tn), lambda j, i: (i, j)),
        out_shape=jax.ShapeDtypeStruct((s, D_MODEL), BF16),
        compiler_params=_params(("parallel", "parallel"), 48),
        name="merge_attn_proj",
    )(o, w_ap, m_c, gates)


def _out_kernel(a_ref, w_ref, x_ref, g_ref, x1_ref, h_ref):
    x1 = x_ref[...] + jnp.dot(a_ref[...], w_ref[...], preferred_element_type=F32)
    x1_ref[...] = x1
    h_ref[...] = _rms_rows(x1, g_ref[...]).astype(h_ref.dtype)


def _out_proj(a, w, x, g, tm=256):
    s, d = x.shape
    return pl.pallas_call(
        _out_kernel,
        grid=(s // tm,),
        in_specs=[pl.BlockSpec((tm, d), lambda i: (i, 0)),
                  pl.BlockSpec((d, d), lambda i: (0, 0)),
                  pl.BlockSpec((tm, d), lambda i: (i, 0)),
                  pl.BlockSpec((1, d), lambda i: (0, 0))],
        out_specs=[pl.BlockSpec((tm, d), lambda i: (i, 0)),
                   pl.BlockSpec((tm, d), lambda i: (i, 0))],
        out_shape=[jax.ShapeDtypeStruct((s, d), F32),
                   jax.ShapeDtypeStruct((s, d), BF16)],
        compiler_params=_params(("parallel",), 48),
        name="out_proj_residual",
    )(a, w, x, g.reshape(1, d))


def _ffn2_kernel(a_ref, w_ref, x_ref, o_ref, acc_ref):
    kk = pl.program_id(2)

    @pl.when(kk == 0)
    def _():
        acc_ref[...] = x_ref[...]

    acc_ref[...] += jnp.dot(a_ref[...], w_ref[...], preferred_element_type=F32)

    @pl.when(kk == pl.num_programs(2) - 1)
    def _():
        o_ref[...] = acc_ref[...]


def _ffn2(a, w, x, tm=1024, tn=1024, tk=2048):
    s, kdim = a.shape
    d = x.shape[1]
    return pl.pallas_call(
        _ffn2_kernel,
        grid=(s // tm, d // tn, kdim // tk),
        in_specs=[pl.BlockSpec((tm, tk), lambda i, j, k: (i, k)),
                  pl.BlockSpec((tk, tn), lambda i, j, k: (k, j)),
                  pl.BlockSpec((tm, tn), lambda i, j, k: (i, j))],
        out_specs=pl.BlockSpec((tm, tn), lambda i, j, k: (i, j)),
        out_shape=jax.ShapeDtypeStruct((s, d), F32),
        scratch_shapes=[pltpu.VMEM((tm, tn), F32)],
        compiler_params=_params(("parallel", "parallel", "arbitrary"), 48),
        name="ffn_down_residual",
    )(a, w, x)


def _ple_kernel(x_ref, p_ref, gp_ref, wg_ref, wp_ref, gf_ref, o_ref):
    x = x_ref[...]
    h = _rms_rows(x, gp_ref[...]).astype(BF16)
    gate = _sigmoid(jnp.dot(h, wg_ref[...], preferred_element_type=F32))
    pp = jnp.dot(p_ref[...].astype(BF16), wp_ref[...], preferred_element_type=F32)
    x3 = x + gate * pp
    o_ref[...] = _rms_rows(x3, gf_ref[...])


def _ple_final(x, p, g_ple, w_gate, w_proj, g_final, tm=256):
    s, d = x.shape
    return pl.pallas_call(
        _ple_kernel,
        grid=(s // tm,),
        in_specs=[pl.BlockSpec((tm, d), lambda i: (i, 0)),
                  pl.BlockSpec((tm, PLE_DIM), lambda i: (i, 0)),
                  pl.BlockSpec((1, d), lambda i: (0, 0)),
                  pl.BlockSpec((d, d), lambda i: (0, 0)),
                  pl.BlockSpec((PLE_DIM, d), lambda i: (0, 0)),
                  pl.BlockSpec((1, d), lambda i: (0, 0))],
        out_specs=pl.BlockSpec((tm, d), lambda i: (i, 0)),
        out_shape=jax.ShapeDtypeStruct((s, d), F32),
        compiler_params=_params(("parallel",), 48),
        name="ple_final_norm",
    )(x, p, g_ple.reshape(1, d), w_gate, w_proj, g_final.reshape(1, d))


def _rope_tables():
    t = jnp.arange(SEQ, dtype=jnp.int32)
    row = (t // GRID_W).astype(F32)
    col = (t % GRID_W).astype(F32)
    inv_freq = ROPE_THETA ** (-jnp.arange(0, AXIS_DIM, 2, dtype=F32) / AXIS_DIM)
    ang_row = row[:, None] * inv_freq[None, :]
    ang_col = col[:, None] * inv_freq[None, :]
    cr, sr = jnp.cos(ang_row), jnp.sin(ang_row)
    cc, sc = jnp.cos(ang_col), jnp.sin(ang_col)
    zero = jnp.zeros_like(sr)
    cos_t = jnp.concatenate([cr, cr, cc, cc], axis=-1)
    sin_up = jnp.concatenate([-sr, zero, -sc, zero], axis=-1)
    sin_dn = jnp.concatenate([zero, sr, zero, sc], axis=-1)
    return cos_t, sin_up, sin_dn


def kernel(x, p, norm_mix, w_in, w_dw, conv_ln_g, conv_ln_b, w_conv_proj, q_norm, k_norm,
           w_attn_proj, w_out, norm_ffn, w_ff1, w_ff2, norm_ple, w_ple_gate, w_ple_proj, norm_final):
    depth = w_in.shape[0]
    assert depth == 1, "the final norm is fused into the last layer's kernel"
    tabs = _rope_tables()
    q_scale = (HEAD_DIM ** -0.5) * math.log2(math.e)
    xs = x[0]
    for li in range(depth):
        w_in_b = w_in[li].astype(BF16)
        h = _rmsnorm_cast(xs, norm_mix[li])
        u = _glu_proj(h, w_in_b)
        qt = _qk_proj(h, w_in_b, q_norm[li], tabs, OFF_Q, Q_W, q_scale, True)
        k = _qk_proj(h, w_in_b, k_norm[li], tabs, OFF_K, KV_W, 1.0, False)
        vt = _vt_proj(h, w_in_b)
        gates = _act_proj(h, w_in_b, OFF_G, 2 * D_MODEL, "sigmoid", "gate_proj")
        m_c = _conv_branch(u, w_dw[li], conv_ln_g[li], conv_ln_b[li],
                           w_conv_proj[li].astype(BF16), gates)
        o = _attention(qt, k, vt)
        merged = _merge(o, w_attn_proj[li].astype(BF16), m_c, gates)
        x1, h2 = _out_proj(merged, w_out[li].astype(BF16), xs, norm_ffn[li])
        a = _act_proj(h2, w_ff1[li].astype(BF16), 0, D_FF, "relu2", "ffn_up")
        x2 = _ffn2(a, w_ff2[li].astype(BF16), x1)
        xs = _ple_final(x2, p[li, 0], norm_ple[li], w_ple_gate[li].astype(BF16),
                        w_ple_proj[li].astype(BF16), norm_final)
    return xs[None]
```

```python
import functools
import math

import jax
import jax.numpy as jnp
from jax import lax
from jax.experimental import pallas as pl
from jax.experimental.pallas import tpu as pltpu

D_MODEL = 2048
SEQ = 8192
N_HEADS = 16
N_KV_HEADS = 4
HEAD_DIM = 128
GROUP = N_HEADS // N_KV_HEADS
ROPE_THETA = 10000.0
AXIS_DIM = HEAD_DIM // 2
GRID_W = 64
CONV_WIDTH = D_MODEL // 2
CONV_KERNEL = 31
CONV_HALO = 16
D_FF = 4 * D_MODEL
PLE_DIM = 256
EPS = 1e-6
Q_W = N_HEADS * HEAD_DIM
KV_W = N_KV_HEADS * HEAD_DIM

OFF_CA = 0
OFF_CB = CONV_WIDTH
OFF_Q = 2 * CONV_WIDTH
OFF_K = OFF_Q + Q_W
OFF_V = OFF_K + KV_W
OFF_G = OFF_V + KV_W

BF16 = jnp.bfloat16
F32 = jnp.float32
MIB = 1024 * 1024


def _params(sem, vmem_mib):
    return pltpu.CompilerParams(dimension_semantics=sem, vmem_limit_bytes=vmem_mib * MIB)


def _sigmoid(v):
    return 1.0 / (1.0 + jnp.exp(-v))


def _rms_rows(v, g):
    ms = jnp.mean(v * v, axis=-1, keepdims=True)
    return v * lax.rsqrt(ms + EPS) * g


def _rmsnorm_kernel(x_ref, g_ref, o_ref):
    o_ref[...] = _rms_rows(x_ref[...], g_ref[...]).astype(o_ref.dtype)


def _rmsnorm_cast(x, g, tm=512):
    s, d = x.shape
    return pl.pallas_call(
        _rmsnorm_kernel,
        grid=(s // tm,),
        in_specs=[pl.BlockSpec((tm, d), lambda i: (i, 0)),
                  pl.BlockSpec((1, d), lambda i: (0, 0))],
        out_specs=pl.BlockSpec((tm, d), lambda i: (i, 0)),
        out_shape=jax.ShapeDtypeStruct((s, d), BF16),
        compiler_params=_params(("parallel",), 32),
        name="rmsnorm_cast",
    )(x, g.reshape(1, d))


def _glu_kernel(h_ref, wa_ref, wb_ref, o_ref):
    h = h_ref[...]
    a = jnp.dot(h, wa_ref[...], preferred_element_type=F32)
    b = jnp.dot(h, wb_ref[...], preferred_element_type=F32)
    o_ref[...] = a * _sigmoid(b)


def _glu_proj(h, w, tm=1024, tn=512):
    s, d = h.shape
    nb = CONV_WIDTH // tn
    return pl.pallas_call(
        _glu_kernel,
        grid=(nb, s // tm),
        in_specs=[pl.BlockSpec((tm, d), lambda j, i: (i, 0)),
                  pl.BlockSpec((d, tn), lambda j, i: (0, OFF_CA // tn + j)),
                  pl.BlockSpec((d, tn), lambda j, i: (0, OFF_CB // tn + j))],
        out_specs=pl.BlockSpec((tm, tn), lambda j, i: (i, j)),
        out_shape=jax.ShapeDtypeStruct((s, CONV_WIDTH), F32),
        compiler_params=_params(("parallel", "parallel"), 40),
        name="glu_proj",
    )(h, w, w)


def _qk_kernel(h_ref, w_ref, g_ref, cos_ref, sa_ref, sb_ref, o_ref, *, scale, transpose_out):
    z = jnp.dot(h_ref[...], w_ref[...], preferred_element_type=F32)
    g = g_ref[...]
    cos = cos_ref[...]
    sa = sa_ref[...]
    sb = sb_ref[...]
    for hh in range(z.shape[1] // HEAD_DIM):
        zh = z[:, hh * HEAD_DIM:(hh + 1) * HEAD_DIM]
        ms = jnp.mean(zh * zh, axis=-1, keepdims=True)
        y = zh * lax.rsqrt(ms + EPS) * g
        up = pltpu.roll(y, HEAD_DIM - AXIS_DIM // 2, 1)
        dn = pltpu.roll(y, AXIS_DIM // 2, 1)
        r = y * cos + up * sa + dn * sb
        if scale != 1.0:
            r = r * scale
        if transpose_out:
            o_ref[hh * HEAD_DIM:(hh + 1) * HEAD_DIM, :] = r.T.astype(o_ref.dtype)
        else:
            o_ref[:, hh * HEAD_DIM:(hh + 1) * HEAD_DIM] = r.astype(o_ref.dtype)


def _qk_proj(h, w, gain, tabs, col_off, width, scale, transpose_out, tm=1024):
    s, d = h.shape
    tn = min(width, 1024)
    cos_t, sa_t, sb_t = tabs
    tab_spec = pl.BlockSpec((tm, HEAD_DIM), lambda j, i: (i, 0))
    if transpose_out:
        out_spec = pl.BlockSpec((tn, tm), lambda j, i: (j, i))
        out_shape = jax.ShapeDtypeStruct((width, s), BF16)
    else:
        out_spec = pl.BlockSpec((tm, tn), lambda j, i: (i, j))
        out_shape = jax.ShapeDtypeStruct((s, width), BF16)
    return pl.pallas_call(
        functools.partial(_qk_kernel, scale=scale, transpose_out=transpose_out),
        grid=(width // tn, s // tm),
        in_specs=[pl.BlockSpec((tm, d), lambda j, i: (i, 0)),
                  pl.BlockSpec((d, tn), lambda j, i: (0, col_off // tn + j)),
                  pl.BlockSpec((1, HEAD_DIM), lambda j, i: (0, 0)),
                  tab_spec, tab_spec, tab_spec],
        out_specs=out_spec,
        out_shape=out_shape,
        compiler_params=_params(("parallel", "parallel"), 40),
        name="qk_proj",
    )(h, w, gain.reshape(1, HEAD_DIM), cos_t, sa_t, sb_t)


def _vt_kernel(h_ref, w_ref, o_ref):
    z = jnp.dot(h_ref[...], w_ref[...], preferred_element_type=F32)
    for hh in range(N_KV_HEADS):
        o_ref[hh * HEAD_DIM:(hh + 1) * HEAD_DIM, :] = (
            z[:, hh * HEAD_DIM:(hh + 1) * HEAD_DIM].T.astype(o_ref.dtype))


def _vt_proj(h, w, tm=1024):
    s, d = h.shape
    return pl.pallas_call(
        _vt_kernel,
        grid=(s // tm,),
        in_specs=[pl.BlockSpec((tm, d), lambda i: (i, 0)),
                  pl.BlockSpec((d, KV_W), lambda i: (0, OFF_V // KV_W))],
        out_specs=pl.BlockSpec((KV_W, tm), lambda i: (0, i)),
        out_shape=jax.ShapeDtypeStruct((KV_W, s), BF16),
        compiler_params=_params(("parallel",), 40),
        name="vt_proj",
    )(h, w)


def _act_kernel(h_ref, w_ref, o_ref, *, act):
    z = jnp.dot(h_ref[...], w_ref[...], preferred_element_type=F32)
    if act == "sigmoid":
        z = _sigmoid(z)
    elif act == "relu2":
        z = jnp.square(jnp.maximum(z, 0.0))
    o_ref[...] = z.astype(o_ref.dtype)


def _act_proj(h, w, col_off, width, act, name, tm=1024):
    s, d = h.shape
    tn = min(width, 1024)
    return pl.pallas_call(
        functools.partial(_act_kernel, act=act),
        grid=(width // tn, s // tm),
        in_specs=[pl.BlockSpec((tm, d), lambda j, i: (i, 0)),
                  pl.BlockSpec((d, tn), lambda j, i: (0, col_off // tn + j))],
        out_specs=pl.BlockSpec((tm, tn), lambda j, i: (i, j)),
        out_shape=jax.ShapeDtypeStruct((s, width), BF16),
        compiler_params=_params(("parallel", "parallel"), 40),
        name=name,
    )(h, w)


def _conv_kernel(u_ref, up_ref, un_ref, wdw_ref, lng_ref, lnb_ref, wp_ref, gate_ref,
                 o_ref, buf_ref, cv_ref, *, tm, rows, lanes):
    i = pl.program_id(0)
    last = pl.num_programs(0) - 1
    prev_ok = (i > 0).astype(F32)
    next_ok = (i < last).astype(F32)
    buf_ref[0:CONV_HALO, :] = up_ref[...] * prev_ok
    buf_ref[CONV_HALO:CONV_HALO + tm, :] = u_ref[...]
    buf_ref[CONV_HALO + tm:, :] = un_ref[...] * next_ok

    base = CONV_HALO - CONV_KERNEL // 2
    for c in range(0, CONV_WIDTH, lanes):
        def chunk(r, carry, c=c):
            r0 = pl.multiple_of(r * rows, rows)
            win = buf_ref[pl.ds(r0, rows + 2 * CONV_HALO), c:c + lanes]
            acc = jnp.zeros((rows, lanes), F32)
            for res in range(8):
                taps = [k for k in range(CONV_KERNEL) if (base + k) % 8 == res]
                span = max(taps) + base - res + rows
                shifted = win[res:res + span, :]
                for k in taps:
                    off = base + k - res
                    acc = acc + shifted[off:off + rows, :] * wdw_ref[k:k + 1, c:c + lanes]
            cv_ref[pl.ds(r0, rows), c:c + lanes] = acc
            return carry
        lax.fori_loop(0, tm // rows, chunk, 0)

    cv = cv_ref[...]
    mu = jnp.mean(cv, axis=-1, keepdims=True)
    xc = cv - mu
    var = jnp.mean(xc * xc, axis=-1, keepdims=True)
    y = xc * lax.rsqrt(var + EPS) * lng_ref[...] + lnb_ref[...]
    y = y * _sigmoid(y)
    yc = jnp.dot(y.astype(BF16), wp_ref[...], preferred_element_type=F32)
    o_ref[...] = gate_ref[...].astype(F32) * yc


def _conv_branch(u, w_dw, ln_g, ln_b, w_proj, gates, tm=512, rows=64, lanes=256):
    s = u.shape[0]
    hb = tm // CONV_HALO
    n_hblk = s // CONV_HALO
    return pl.pallas_call(
        functools.partial(_conv_kernel, tm=tm, rows=rows, lanes=lanes),
        grid=(s // tm,),
        in_specs=[pl.BlockSpec((tm, CONV_WIDTH), lambda i: (i, 0)),
                  pl.BlockSpec((CONV_HALO, CONV_WIDTH), lambda i: (jnp.maximum(i * hb - 1, 0), 0)),
                  pl.BlockSpec((CONV_HALO, CONV_WIDTH), lambda i: (jnp.minimum((i + 1) * hb, n_hblk - 1), 0)),
                  pl.BlockSpec((CONV_KERNEL, CONV_WIDTH), lambda i: (0, 0)),
                  pl.BlockSpec((1, CONV_WIDTH), lambda i: (0, 0)),
                  pl.BlockSpec((1, CONV_WIDTH), lambda i: (0, 0)),
                  pl.BlockSpec((CONV_WIDTH, D_MODEL), lambda i: (0, 0)),
                  pl.BlockSpec((tm, D_MODEL), lambda i: (i, 0))],
        out_specs=pl.BlockSpec((tm, D_MODEL), lambda i: (i, 0)),
        out_shape=jax.ShapeDtypeStruct((s, D_MODEL), F32),
        scratch_shapes=[pltpu.VMEM((tm + 2 * CONV_HALO, CONV_WIDTH), F32),
                        pltpu.VMEM((tm, CONV_WIDTH), F32)],
        compiler_params=_params(("parallel",), 48),
        name="conv_branch",
    )(u, u, u, w_dw, ln_g.reshape(1, -1), ln_b.reshape(1, -1), w_proj, gates)


def _attn_kernel(qt_ref, k_ref, vt_ref, o_ref, qs_ref, acc_ref, s0_ref, s1_ref, *, tq, tk):
    m_cols = GROUP * tq
    n_chunks = k_ref.shape[0] // tk
    assert n_chunks % 2 == 0 and n_chunks >= 2
    for hh in range(GROUP):
        qs_ref[:, hh * tq:(hh + 1) * tq] = qt_ref[hh * HEAD_DIM:(hh + 1) * HEAD_DIM, :]
    acc_ref[...] = jnp.zeros(acc_ref.shape, F32)

    def scores(c, dst_ref):
        c0 = pl.multiple_of(c * tk, tk)
        dst_ref[...] = jnp.dot(k_ref[pl.ds(c0, tk), :], qs_ref[...], preferred_element_type=F32)

    def update(c, src_ref, carry):
        m_prev, l_prev = carry
        c0 = pl.multiple_of(c * tk, tk)
        s = src_ref[...]
        m_new = jnp.maximum(m_prev, jnp.max(s, axis=0, keepdims=True))
        alpha = jnp.exp2(m_prev - m_new)
        p = jnp.exp2(s - m_new)
        l_new = alpha * l_prev + jnp.sum(p, axis=0, keepdims=True)
        vtc = vt_ref[:, pl.ds(c0, tk)]
        acc_ref[...] = alpha * acc_ref[...] + jnp.dot(vtc, p.astype(BF16), preferred_element_type=F32)
        return m_new, l_new

    def pair(j, carry):
        c = 2 * j
        scores(c + 1, s1_ref)
        carry = update(c, s0_ref, carry)
        scores(c + 2, s0_ref)
        return update(c + 1, s1_ref, carry)

    scores(0, s0_ref)
    carry = (jnp.full((1, m_cols), -1e30, F32), jnp.zeros((1, m_cols), F32))
    carry = lax.fori_loop(0, n_chunks // 2 - 1, pair, carry)
    scores(n_chunks - 1, s1_ref)
    carry = update(n_chunks - 2, s0_ref, carry)
    _, l_fin = update(n_chunks - 1, s1_ref, carry)
    o_t = acc_ref[...] / l_fin
    for hh in range(GROUP):
        o_ref[:, hh * HEAD_DIM:(hh + 1) * HEAD_DIM] = o_t[:, hh * tq:(hh + 1) * tq].T.astype(o_ref.dtype)


def _attention(qt, k, vt, tq=256, tk=512):
    s = k.shape[0]
    gw = GROUP * HEAD_DIM
    m_cols = GROUP * tq
    return pl.pallas_call(
        functools.partial(_attn_kernel, tq=tq, tk=tk),
        grid=(N_KV_HEADS, s // tq),
        in_specs=[pl.BlockSpec((gw, tq), lambda g, i: (g, i)),
                  pl.BlockSpec((s, HEAD_DIM), lambda g, i: (0, g)),
                  pl.BlockSpec((HEAD_DIM, s), lambda g, i: (g, 0))],
        out_specs=pl.BlockSpec((tq, gw), lambda g, i: (i, g)),
        out_shape=jax.ShapeDtypeStruct((s, Q_W), BF16),
        scratch_shapes=[pltpu.VMEM((HEAD_DIM, m_cols), BF16),
                        pltpu.VMEM((HEAD_DIM, m_cols), F32),
                        pltpu.VMEM((tk, m_cols), F32),
                        pltpu.VMEM((tk, m_cols), F32)],
        compiler_params=_params(("parallel", "parallel"), 40),
        name="gqa_flash",
    )(qt, k, vt)


def _merge_kernel(o_ref, w_ref, mc_ref, ga_ref, out_ref):
    ya = jnp.dot(o_ref[...], w_ref[...], preferred_element_type=F32)
    out_ref[...] = (mc_ref[...] + ga_ref[...].astype(F32) * ya).astype(out_ref.dtype)


def _merge(o, w_ap, m_c, gates, tm=1024, tn=1024):
    s, d = o.shape
    ga_off = D_MODEL // tn
    return pl.pallas_call(
        _merge_kernel,
        grid=(D_MODEL // tn, s // tm),
        in_specs=[pl.BlockSpec((tm, d), lambda j, i: (i, 0)),
                  pl.BlockSpec((d, tn), lambda j, i: (0, j)),
                  pl.BlockSpec((tm, tn), lambda j, i: (i, j)),
                  pl.BlockSpec((tm, tn), lambda j, i: (i, ga_off + j))],
        out_specs=pl.BlockSpec((tm, tn), lambda j, i: (i, j)),
        out_shape=jax.ShapeDtypeStruct((s, D_MODEL), BF16),
        compiler_params=_params(("parallel", "parallel"), 48),
        name="merge_attn_proj",
    )(o, w_ap, m_c, gates)


def _out_kernel(a_ref, w_ref, x_ref, g_ref, x1_ref, h_ref):
    x1 = x_ref[...] + jnp.dot(a_ref[...], w_ref[...], preferred_element_type=F32)
    x1_ref[...] = x1
    h_ref[...] = _rms_rows(x1, g_ref[...]).astype(h_ref.dtype)


def _out_proj(a, w, x, g, tm=256):
    s, d = x.shape
    return pl.pallas_call(
        _out_kernel,
        grid=(s // tm,),
        in_specs=[pl.BlockSpec((tm, d), lambda i: (i, 0)),
                  pl.BlockSpec((d, d), lambda i: (0, 0)),
                  pl.BlockSpec((tm, d), lambda i: (i, 0)),
                  pl.BlockSpec((1, d), lambda i: (0, 0))],
        out_specs=[pl.BlockSpec((tm, d), lambda i: (i, 0)),
                   pl.BlockSpec((tm, d), lambda i: (i, 0))],
        out_shape=[jax.ShapeDtypeStruct((s, d), F32),
                   jax.ShapeDtypeStruct((s, d), BF16)],
        compiler_params=_params(("parallel",), 48),
        name="out_proj_residual",
    )(a, w, x, g.reshape(1, d))


def _ffn2_kernel(a_ref, w_ref, x_ref, o_ref, acc_ref):
    kk = pl.program_id(2)

    @pl.when(kk == 0)
    def _():
        acc_ref[...] = x_ref[...]

    acc_ref[...] += jnp.dot(a_ref[...], w_ref[...], preferred_element_type=F32)

    @pl.when(kk == pl.num_programs(2) - 1)
    def _():
        o_ref[...] = acc_ref[...]


def _ffn2(a, w, x, tm=1024, tn=1024, tk=2048):
    s, kdim = a.shape
    d = x.shape[1]
    return pl.pallas_call(
        _ffn2_kernel,
        grid=(s // tm, d // tn, kdim // tk),
        in_specs=[pl.BlockSpec((tm, tk), lambda i, j, k: (i, k)),
                  pl.BlockSpec((tk, tn), lambda i, j, k: (k, j)),
                  pl.BlockSpec((tm, tn), lambda i, j, k: (i, j))],
        out_specs=pl.BlockSpec((tm, tn), lambda i, j, k: (i, j)),
        out_shape=jax.ShapeDtypeStruct((s, d), F32),
        scratch_shapes=[pltpu.VMEM((tm, tn), F32)],
        compiler_params=_params(("parallel", "parallel", "arbitrary"), 48),
        name="ffn_down_residual",
    )(a, w, x)


def _ple_kernel(x_ref, p_ref, gp_ref, wg_ref, wp_ref, gf_ref, o_ref):
    x = x_ref[...]
    h = _rms_rows(x, gp_ref[...]).astype(BF16)
    gate = _sigmoid(jnp.dot(h, wg_ref[...], preferred_element_type=F32))
    pp = jnp.dot(p_ref[...].astype(BF16), wp_ref[...], preferred_element_type=F32)
    x3 = x + gate * pp
    o_ref[...] = _rms_rows(x3, gf_ref[...])


def _ple_final(x, p, g_ple, w_gate, w_proj, g_final, tm=256):
    s, d = x.shape
    return pl.pallas_call(
        _ple_kernel,
        grid=(s // tm,),
        in_specs=[pl.BlockSpec((tm, d), lambda i: (i, 0)),
                  pl.BlockSpec((tm, PLE_DIM), lambda i: (i, 0)),
                  pl.BlockSpec((1, d), lambda i: (0, 0)),
                  pl.BlockSpec((d, d), lambda i: (0, 0)),
                  pl.BlockSpec((PLE_DIM, d), lambda i: (0, 0)),
                  pl.BlockSpec((1, d), lambda i: (0, 0))],
        out_specs=pl.BlockSpec((tm, d), lambda i: (i, 0)),
        out_shape=jax.ShapeDtypeStruct((s, d), F32),
        compiler_params=_params(("parallel",), 48),
        name="ple_final_norm",
    )(x, p, g_ple.reshape(1, d), w_gate, w_proj, g_final.reshape(1, d))


def _rope_tables():
    t = jnp.arange(SEQ, dtype=jnp.int32)
    row = (t // GRID_W).astype(F32)
    col = (t % GRID_W).astype(F32)
    inv_freq = ROPE_THETA ** (-jnp.arange(0, AXIS_DIM, 2, dtype=F32) / AXIS_DIM)
    ang_row = row[:, None] * inv_freq[None, :]
    ang_col = col[:, None] * inv_freq[None, :]
    cr, sr = jnp.cos(ang_row), jnp.sin(ang_row)
    cc, sc = jnp.cos(ang_col), jnp.sin(ang_col)
    zero = jnp.zeros_like(sr)
    cos_t = jnp.concatenate([cr, cr, cc, cc], axis=-1)
    sin_up = jnp.concatenate([-sr, zero, -sc, zero], axis=-1)
    sin_dn = jnp.concatenate([zero, sr, zero, sc], axis=-1)
    return cos_t, sin_up, sin_dn


def kernel(x, p, norm_mix, w_in, w_dw, conv_ln_g, conv_ln_b, w_conv_proj, q_norm, k_norm,
           w_attn_proj, w_out, norm_ffn, w_ff1, w_ff2, norm_ple, w_ple_gate, w_ple_proj, norm_final):
    depth = w_in.shape[0]
    assert depth == 1, "the final norm is fused into the last layer's kernel"
    tabs = _rope_tables()
    q_scale = (HEAD_DIM ** -0.5) * math.log2(math.e)
    xs = x[0]
    for li in range(depth):
        w_in_b = w_in[li].astype(BF16)
        h = _rmsnorm_cast(xs, norm_mix[li])
        u = _glu_proj(h, w_in_b)
        qt = _qk_proj(h, w_in_b, q_norm[li], tabs, OFF_Q, Q_W, q_scale, True)
        k = _qk_proj(h, w_in_b, k_norm[li], tabs, OFF_K, KV_W, 1.0, False)
        vt = _vt_proj(h, w_in_b)
        gates = _act_proj(h, w_in_b, OFF_G, 2 * D_MODEL, "sigmoid", "gate_proj")
        m_c = _conv_branch(u, w_dw[li], conv_ln_g[li], conv_ln_b[li],
                           w_conv_proj[li].astype(BF16), gates)
        o = _attention(qt, k, vt)
        merged = _merge(o, w_attn_proj[li].astype(BF16), m_c, gates)
        x1, h2 = _out_proj(merged, w_out[li].astype(BF16), xs, norm_ffn[li])
        a = _act_proj(h2, w_ff1[li].astype(BF16), 0, D_FF, "relu2", "ffn_up")
        x2 = _ffn2(a, w_ff2[li].astype(BF16), x1)
        xs = _ple_final(x2, p[li, 0], norm_ple[li], w_ple_gate[li].astype(BF16),
                        w_ple_proj[li].astype(BF16), norm_final)
    return xs[None]
```

```python
import functools
import math

import jax
import jax.numpy as jnp
from jax import lax
from jax.experimental import pallas as pl
from jax.experimental.pallas import tpu as pltpu

D_MODEL = 2048
SEQ = 8192
N_HEADS = 16
N_KV_HEADS = 4
HEAD_DIM = 128
GROUP = N_HEADS // N_KV_HEADS
ROPE_THETA = 10000.0
AXIS_DIM = HEAD_DIM // 2
GRID_W = 64
CONV_WIDTH = D_MODEL // 2
CONV_KERNEL = 31
CONV_HALO = 16
D_FF = 4 * D_MODEL
PLE_DIM = 256
EPS = 1e-6
Q_W = N_HEADS * HEAD_DIM
KV_W = N_KV_HEADS * HEAD_DIM
LANES = 128
SUBLANES = 8
BF16_SUBLANES = 16
V_ROWS = HEAD_DIM + BF16_SUBLANES

OFF_CA = 0
OFF_CB = CONV_WIDTH
OFF_Q = 2 * CONV_WIDTH
OFF_K = OFF_Q + Q_W
OFF_V = OFF_K + KV_W
OFF_G = OFF_V + KV_W

BF16 = jnp.bfloat16
F32 = jnp.float32
MIB = 1024 * 1024


def _params(sem, vmem_mib):
    return pltpu.CompilerParams(dimension_semantics=sem, vmem_limit_bytes=vmem_mib * MIB)


def _sigmoid(v):
    return 1.0 / (1.0 + jnp.exp(-v))


def _rms_rows(v, g):
    ms = jnp.mean(v * v, axis=-1, keepdims=True)
    return v * lax.rsqrt(ms + EPS) * g


def _rmsnorm_kernel(x_ref, g_ref, o_ref, ot_ref):
    y = _rms_rows(x_ref[...], g_ref[...])
    o_ref[...] = y.astype(o_ref.dtype)
    for cb in range(0, y.shape[1], LANES):
        ot_ref[cb:cb + LANES, :] = y[:, cb:cb + LANES].T.astype(ot_ref.dtype)


def _rmsnorm_cast(x, g, tm=512):
    s, d = x.shape
    return pl.pallas_call(
        _rmsnorm_kernel,
        grid=(s // tm,),
        in_specs=[pl.BlockSpec((tm, d), lambda i: (i, 0)),
                  pl.BlockSpec((1, d), lambda i: (0, 0))],
        out_specs=[pl.BlockSpec((tm, d), lambda i: (i, 0)),
                   pl.BlockSpec((d, tm), lambda i: (0, i))],
        out_shape=[jax.ShapeDtypeStruct((s, d), BF16),
                   jax.ShapeDtypeStruct((d, s), BF16)],
        compiler_params=_params(("parallel",), 32),
        name="rmsnorm_cast",
    )(x, g.reshape(1, d))


def _glu_kernel(h_ref, wa_ref, wb_ref, o_ref):
    h = h_ref[...]
    a = jnp.dot(h, wa_ref[...], preferred_element_type=F32)
    b = jnp.dot(h, wb_ref[...], preferred_element_type=F32)
    o_ref[...] = a * _sigmoid(b)


def _glu_proj(h, w, tm=1024, tn=512):
    s, d = h.shape
    nb = CONV_WIDTH // tn
    return pl.pallas_call(
        _glu_kernel,
        grid=(nb, s // tm),
        in_specs=[pl.BlockSpec((tm, d), lambda j, i: (i, 0)),
                  pl.BlockSpec((d, tn), lambda j, i: (0, OFF_CA // tn + j)),
                  pl.BlockSpec((d, tn), lambda j, i: (0, OFF_CB // tn + j))],
        out_specs=pl.BlockSpec((tm, tn), lambda j, i: (i, j)),
        out_shape=jax.ShapeDtypeStruct((s, CONV_WIDTH), F32),
        compiler_params=_params(("parallel", "parallel"), 40),
        name="glu_proj",
    )(h, w, w)


def _k_kernel(h_ref, w_ref, g_ref, cos_ref, sa_ref, sb_ref, o_ref):
    z = jnp.dot(h_ref[...], w_ref[...], preferred_element_type=F32)
    g = g_ref[...]
    cos = cos_ref[...]
    sa = sa_ref[...]
    sb = sb_ref[...]
    for hh in range(z.shape[1] // HEAD_DIM):
        zh = z[:, hh * HEAD_DIM:(hh + 1) * HEAD_DIM]
        ms = jnp.mean(zh * zh, axis=-1, keepdims=True)
        y = zh * lax.rsqrt(ms + EPS) * g
        up = pltpu.roll(y, HEAD_DIM - AXIS_DIM // 2, 1)
        dn = pltpu.roll(y, AXIS_DIM // 2, 1)
        r = y * cos + up * sa + dn * sb
        o_ref[:, hh * HEAD_DIM:(hh + 1) * HEAD_DIM] = r.astype(o_ref.dtype)


def _k_proj(h, w, gain, tabs, tm=1024):
    s, d = h.shape
    cos_t, sa_t, sb_t = tabs
    tab_spec = pl.BlockSpec((tm, HEAD_DIM), lambda i: (i, 0))
    return pl.pallas_call(
        _k_kernel,
        grid=(s // tm,),
        in_specs=[pl.BlockSpec((tm, d), lambda i: (i, 0)),
                  pl.BlockSpec((d, KV_W), lambda i: (0, OFF_K // KV_W)),
                  pl.BlockSpec((1, HEAD_DIM), lambda i: (0, 0)),
                  tab_spec, tab_spec, tab_spec],
        out_specs=pl.BlockSpec((tm, KV_W), lambda i: (i, 0)),
        out_shape=jax.ShapeDtypeStruct((s, KV_W), BF16),
        compiler_params=_params(("parallel",), 40),
        name="k_proj",
    )(h, w, gain.reshape(1, HEAD_DIM), cos_t, sa_t, sb_t)


def _swap_axis_halves(y):
    q = AXIS_DIM // 2
    return jnp.concatenate([y[q:2 * q], y[0:q], y[3 * q:4 * q], y[2 * q:3 * q]], axis=0)


def _qt_kernel(wt_ref, ht_ref, g_ref, cos_ref, sin_ref, o_ref, *, scale):
    zt = jnp.dot(wt_ref[...], ht_ref[...], preferred_element_type=F32)
    g = jnp.broadcast_to(g_ref[...], (HEAD_DIM, zt.shape[1]))
    cos = cos_ref[...]
    sin = sin_ref[...]
    for hh in range(zt.shape[0] // HEAD_DIM):
        zh = zt[hh * HEAD_DIM:(hh + 1) * HEAD_DIM, :]
        ms = jnp.mean(zh * zh, axis=0, keepdims=True)
        y = zh * lax.rsqrt(ms + EPS) * g
        r = (y * cos + _swap_axis_halves(y) * sin) * scale
        o_ref[hh * HEAD_DIM:(hh + 1) * HEAD_DIM, :] = r.astype(o_ref.dtype)


def _qt_proj(w_t, h_t, gain, tabs_t, scale, tm=1024, tn=1024):
    d, s = h_t.shape
    cos_t, sin_t = tabs_t
    tab_spec = pl.BlockSpec((HEAD_DIM, tm), lambda j, i: (0, i))
    return pl.pallas_call(
        functools.partial(_qt_kernel, scale=scale),
        grid=(Q_W // tn, s // tm),
        in_specs=[pl.BlockSpec((tn, d), lambda j, i: (j, 0)),
                  pl.BlockSpec((d, tm), lambda j, i: (0, i)),
                  pl.BlockSpec((HEAD_DIM, 1), lambda j, i: (0, 0)),
                  tab_spec, tab_spec],
        out_specs=pl.BlockSpec((tn, tm), lambda j, i: (j, i)),
        out_shape=jax.ShapeDtypeStruct((Q_W, s), BF16),
        compiler_params=_params(("parallel", "parallel"), 40),
        name="qt_proj",
    )(w_t, h_t, gain.reshape(HEAD_DIM, 1), cos_t, sin_t)


def _vt_kernel(wt_ref, ht_ref, o_ref):
    zt = jnp.dot(wt_ref[...], ht_ref[...], preferred_element_type=F32)
    pad_rows = V_ROWS - HEAD_DIM
    row = lax.broadcasted_iota(jnp.int32, (pad_rows, zt.shape[1]), 0)
    ones_then_zeros = jnp.where(row == 0, 1.0, 0.0).astype(o_ref.dtype)
    for hh in range(N_KV_HEADS):
        o_ref[hh * V_ROWS:hh * V_ROWS + HEAD_DIM, :] = (
            zt[hh * HEAD_DIM:(hh + 1) * HEAD_DIM, :].astype(o_ref.dtype))
        o_ref[hh * V_ROWS + HEAD_DIM:(hh + 1) * V_ROWS, :] = ones_then_zeros


def _vt_proj(w_t, h_t, tm=1024):
    d, s = h_t.shape
    return pl.pallas_call(
        _vt_kernel,
        grid=(s // tm,),
        in_specs=[pl.BlockSpec((KV_W, d), lambda i: (0, 0)),
                  pl.BlockSpec((d, tm), lambda i: (0, i))],
        out_specs=pl.BlockSpec((N_KV_HEADS * V_ROWS, tm), lambda i: (0, i)),
        out_shape=jax.ShapeDtypeStruct((N_KV_HEADS * V_ROWS, s), BF16),
        compiler_params=_params(("parallel",), 40),
        name="vt_proj",
    )(w_t, h_t)


def _act_kernel(h_ref, w_ref, o_ref, *, act):
    z = jnp.dot(h_ref[...], w_ref[...], preferred_element_type=F32)
    if act == "sigmoid":
        z = _sigmoid(z)
    elif act == "relu2":
        z = jnp.square(jnp.maximum(z, 0.0))
    o_ref[...] = z.astype(o_ref.dtype)


def _act_proj(h, w, col_off, width, act, name, tm=1024):
    s, d = h.shape
    tn = min(width, 1024)
    return pl.pallas_call(
        functools.partial(_act_kernel, act=act),
        grid=(width // tn, s // tm),
        in_specs=[pl.BlockSpec((tm, d), lambda j, i: (i, 0)),
                  pl.BlockSpec((d, tn), lambda j, i: (0, col_off // tn + j))],
        out_specs=pl.BlockSpec((tm, tn), lambda j, i: (i, j)),
        out_shape=jax.ShapeDtypeStruct((s, width), BF16),
        compiler_params=_params(("parallel", "parallel"), 40),
        name=name,
    )(h, w)


def _conv_kernel(u_ref, up_ref, un_ref, wdw_ref, lng_ref, lnb_ref, wp_ref, gate_ref,
                 o_ref, buf_ref, cv_ref, sh_ref, *, tm, rows, lanes):
    i = pl.program_id(0)
    last = pl.num_programs(0) - 1
    prev_ok = (i > 0).astype(F32)
    next_ok = (i < last).astype(F32)
    buf_ref[0:CONV_HALO, :] = up_ref[...] * prev_ok
    buf_ref[CONV_HALO:CONV_HALO + tm, :] = u_ref[...]
    buf_ref[CONV_HALO + tm:, :] = un_ref[...] * next_ok

    base = CONV_HALO - CONV_KERNEL // 2
    span = rows + 2 * CONV_HALO - SUBLANES
    for c in range(0, CONV_WIDTH, lanes):
        def chunk(r, carry, c=c):
            r0 = pl.multiple_of(r * rows, rows)
            win = buf_ref[pl.ds(r0, rows + 2 * CONV_HALO), c:c + lanes]
            for res in range(1, SUBLANES):
                sh_ref[res - 1] = win[res:res + span, :]
            acc = jnp.zeros((rows, lanes), F32)
            for k in range(CONV_KERNEL):
                res = (base + k) % SUBLANES
                off = base + k - res
                if res == 0:
                    tap = win[off:off + rows, :]
                else:
                    tap = sh_ref[res - 1, off:off + rows, :]
                acc = acc + tap * wdw_ref[k:k + 1, c:c + lanes]
            cv_ref[pl.ds(r0, rows), c:c + lanes] = acc
            return carry
        lax.fori_loop(0, tm // rows, chunk, 0)

    cv = cv_ref[...]
    mu = jnp.mean(cv, axis=-1, keepdims=True)
    xc = cv - mu
    var = jnp.mean(xc * xc, axis=-1, keepdims=True)
    y = xc * lax.rsqrt(var + EPS) * lng_ref[...] + lnb_ref[...]
    y = y * _sigmoid(y)
    yc = jnp.dot(y.astype(BF16), wp_ref[...], preferred_element_type=F32)
    o_ref[...] = gate_ref[...].astype(F32) * yc


def _conv_branch(u, w_dw, ln_g, ln_b, w_proj, gates, tm=512, rows=64, lanes=256):
    s = u.shape[0]
    hb = tm // CONV_HALO
    n_hblk = s // CONV_HALO
    return pl.pallas_call(
        functools.partial(_conv_kernel, tm=tm, rows=rows, lanes=lanes),
        grid=(s // tm,),
        in_specs=[pl.BlockSpec((tm, CONV_WIDTH), lambda i: (i, 0)),
                  pl.BlockSpec((CONV_HALO, CONV_WIDTH), lambda i: (jnp.maximum(i * hb - 1, 0), 0)),
                  pl.BlockSpec((CONV_HALO, CONV_WIDTH), lambda i: (jnp.minimum((i + 1) * hb, n_hblk - 1), 0)),
                  pl.BlockSpec((CONV_KERNEL, CONV_WIDTH), lambda i: (0, 0)),
                  pl.BlockSpec((1, CONV_WIDTH), lambda i: (0, 0)),
                  pl.BlockSpec((1, CONV_WIDTH), lambda i: (0, 0)),
                  pl.BlockSpec((CONV_WIDTH, D_MODEL), lambda i: (0, 0)),
                  pl.BlockSpec((tm, D_MODEL), lambda i: (i, 0))],
        out_specs=pl.BlockSpec((tm, D_MODEL), lambda i: (i, 0)),
        out_shape=jax.ShapeDtypeStruct((s, D_MODEL), F32),
        scratch_shapes=[pltpu.VMEM((tm + 2 * CONV_HALO, CONV_WIDTH), F32),
                        pltpu.VMEM((tm, CONV_WIDTH), F32),
                        pltpu.VMEM((SUBLANES - 1, rows + 2 * CONV_HALO - SUBLANES, lanes), F32)],
        compiler_params=_params(("parallel",), 48),
        name="conv_branch",
    )(u, u, u, w_dw, ln_g.reshape(1, -1), ln_b.reshape(1, -1), w_proj, gates)


def _attn_kernel(qt_ref, k_ref, vt_ref, o_ref, qs_ref, acc_ref, s0_ref, s1_ref, *, tq, tk):
    m_cols = GROUP * tq
    n_chunks = k_ref.shape[0] // tk
    assert n_chunks % 2 == 0 and n_chunks >= 2
    for hh in range(GROUP):
        qs_ref[:, hh * tq:(hh + 1) * tq] = qt_ref[hh * HEAD_DIM:(hh + 1) * HEAD_DIM, :]
    acc_ref[...] = jnp.zeros(acc_ref.shape, F32)

    def start(c):
        return c * tk if isinstance(c, int) else pl.multiple_of(c * tk, tk)

    def scores(c, dst_ref):
        s = jnp.dot(k_ref[pl.ds(start(c), tk), :], qs_ref[...], preferred_element_type=F32)
        dst_ref[...] = s
        return jnp.max(s, axis=0, keepdims=True)

    def update(c, src_ref, col_max, m_prev):
        m_new = jnp.maximum(m_prev, col_max)
        alpha = jnp.exp2(m_prev - m_new)
        p = jnp.exp2(src_ref[...] - m_new).astype(BF16)
        vtc = vt_ref[:, pl.ds(start(c), tk)]
        acc_ref[...] = alpha * acc_ref[...] + jnp.dot(vtc, p, preferred_element_type=F32)
        return m_new

    def pair(j, carry):
        m_run, cm0 = carry
        c = 2 * j
        cm1 = scores(c + 1, s1_ref)
        m_run = update(c, s0_ref, cm0, m_run)
        cm2 = scores(c + 2, s0_ref)
        m_run = update(c + 1, s1_ref, cm1, m_run)
        return m_run, cm2

    cm0 = scores(0, s0_ref)
    carry = (jnp.full((1, m_cols), -1e30, F32), cm0)
    m_run, cm0 = lax.fori_loop(0, n_chunks // 2 - 1, pair, carry)
    cm1 = scores(n_chunks - 1, s1_ref)
    m_run = update(n_chunks - 2, s0_ref, cm0, m_run)
    update(n_chunks - 1, s1_ref, cm1, m_run)
    o_t = acc_ref[0:HEAD_DIM, :] / acc_ref[HEAD_DIM:HEAD_DIM + 1, :]
    for hh in range(GROUP):
        o_ref[:, hh * HEAD_DIM:(hh + 1) * HEAD_DIM] = o_t[:, hh * tq:(hh + 1) * tq].T.astype(o_ref.dtype)


def _attention(qt, k, vt, tq=256, tk=512):
    s = k.shape[0]
    gw = GROUP * HEAD_DIM
    m_cols = GROUP * tq
    return pl.pallas_call(
        functools.partial(_attn_kernel, tq=tq, tk=tk),
        grid=(N_KV_HEADS, s // tq),
        in_specs=[pl.BlockSpec((gw, tq), lambda g, i: (g, i)),
                  pl.BlockSpec((s, HEAD_DIM), lambda g, i: (0, g)),
                  pl.BlockSpec((V_ROWS, s), lambda g, i: (g, 0))],
        out_specs=pl.BlockSpec((tq, gw), lambda g, i: (i, g)),
        out_shape=jax.ShapeDtypeStruct((s, Q_W), BF16),
        scratch_shapes=[pltpu.VMEM((HEAD_DIM, m_cols), BF16),
                        pltpu.VMEM((V_ROWS, m_cols), F32),
                        pltpu.VMEM((tk, m_cols), F32),
                        pltpu.VMEM((tk, m_cols), F32)],
        compiler_params=_params(("parallel", "parallel"), 40),
        name="gqa_flash",
    )(qt, k, vt)


def _merge_kernel(o_ref, w_ref, mc_ref, ga_ref, out_ref):
    ya = jnp.dot(o_ref[...], w_ref[...], preferred_element_type=F32)
    out_ref[...] = (mc_ref[...] + ga_ref[...].astype(F32) * ya).astype(out_ref.dtype)


def _merge(o, w_ap, m_c, gates, tm=1024, tn=1024):
    s, d = o.shape
    ga_off = D_MODEL // tn
    return pl.pallas_call(
        _merge_kernel,
        grid=(D_MODEL // tn, s // tm),
        in_specs=[pl.BlockSpec((tm, d), lambda j, i: (i, 0)),
                  pl.BlockSpec((d, tn), lambda j, i: (0, j)),
                  pl.BlockSpec((tm, tn), lambda j, i: (i, j)),
                  pl.BlockSpec((tm, tn), lambda j, i: (i, ga_off + j))],
        out_specs=pl.BlockSpec((tm, tn), lambda j, i: (i, j)),
        out_shape=jax.ShapeDtypeStruct((s, D_MODEL), BF16),
        compiler_params=_params(("parallel", "parallel"), 48),
        name="merge_attn_proj",
    )(o, w_ap, m_c, gates)


def _out_kernel(a_ref, w_ref, x_ref, g_ref, x1_ref, h_ref):
    x1 = x_ref[...] + jnp.dot(a_ref[...], w_ref[...], preferred_element_type=F32)
    x1_ref[...] = x1
    h_ref[...] = _rms_rows(x1, g_ref[...]).astype(h_ref.dtype)


def _out_proj(a, w, x, g, tm=256):
    s, d = x.shape
    return pl.pallas_call(
        _out_kernel,
        grid=(s // tm,),
        in_specs=[pl.BlockSpec((tm, d), lambda i: (i, 0)),
                  pl.BlockSpec((d, d), lambda i: (0, 0)),
                  pl.BlockSpec((tm, d), lambda i: (i, 0)),
                  pl.BlockSpec((1, d), lambda i: (0, 0))],
        out_specs=[pl.BlockSpec((tm, d), lambda i: (i, 0)),
                   pl.BlockSpec((tm, d), lambda i: (i, 0))],
        out_shape=[jax.ShapeDtypeStruct((s, d), F32),
                   jax.ShapeDtypeStruct((s, d), BF16)],
        compiler_params=_params(("parallel",), 48),
        name="out_proj_residual",
    )(a, w, x, g.reshape(1, d))


def _ffn2_kernel(a_ref, w_ref, x_ref, o_ref, acc_ref):
    kk = pl.program_id(2)

    @pl.when(kk == 0)
    def _():
        acc_ref[...] = x_ref[...]

    acc_ref[...] += jnp.dot(a_ref[...], w_ref[...], preferred_element_type=F32)

    @pl.when(kk == pl.num_programs(2) - 1)
    def _():
        o_ref[...] = acc_ref[...]


def _ffn2(a, w, x, tm=1024, tn=1024, tk=2048):
    s, kdim = a.shape
    d = x.shape[1]
    return pl.pallas_call(
        _ffn2_kernel,
        grid=(s // tm, d // tn, kdim // tk),
        in_specs=[pl.BlockSpec((tm, tk), lambda i, j, k: (i, k)),
                  pl.BlockSpec((tk, tn), lambda i, j, k: (k, j)),
                  pl.BlockSpec((tm, tn), lambda i, j, k: (i, j))],
        out_specs=pl.BlockSpec((tm, tn), lambda i, j, k: (i, j)),
        out_shape=jax.ShapeDtypeStruct((s, d), F32),
        scratch_shapes=[pltpu.VMEM((tm, tn), F32)],
        compiler_params=_params(("parallel", "parallel", "arbitrary"), 48),
        name="ffn_down_residual",
    )(a, w, x)


def _ple_kernel(x_ref, p_ref, gp_ref, wg_ref, wp_ref, gf_ref, o_ref):
    x = x_ref[...]
    h = _rms_rows(x, gp_ref[...]).astype(BF16)
    gate = _sigmoid(jnp.dot(h, wg_ref[...], preferred_element_type=F32))
    pp = jnp.dot(p_ref[...].astype(BF16), wp_ref[...], preferred_element_type=F32)
    x3 = x + gate * pp
    o_ref[...] = _rms_rows(x3, gf_ref[...])


def _ple_final(x, p, g_ple, w_gate, w_proj, g_final, tm=256):
    s, d = x.shape
    return pl.pallas_call(
        _ple_kernel,
        grid=(s // tm,),
        in_specs=[pl.BlockSpec((tm, d), lambda i: (i, 0)),
                  pl.BlockSpec((tm, PLE_DIM), lambda i: (i, 0)),
                  pl.BlockSpec((1, d), lambda i: (0, 0)),
                  pl.BlockSpec((d, d), lambda i: (0, 0)),
                  pl.BlockSpec((PLE_DIM, d), lambda i: (0, 0)),
                  pl.BlockSpec((1, d), lambda i: (0, 0))],
        out_specs=pl.BlockSpec((tm, d), lambda i: (i, 0)),
        out_shape=jax.ShapeDtypeStruct((s, d), F32),
        compiler_params=_params(("parallel",), 48),
        name="ple_final_norm",
    )(x, p, g_ple.reshape(1, d), w_gate, w_proj, g_final.reshape(1, d))


def _rope_tables():
    n_rows = SEQ // GRID_W
    inv_freq = ROPE_THETA ** (-jnp.arange(0, AXIS_DIM, 2, dtype=F32) / AXIS_DIM)
    ang_row = jnp.arange(n_rows, dtype=jnp.int32).astype(F32)[:, None] * inv_freq[None, :]
    ang_col = jnp.arange(GRID_W, dtype=jnp.int32).astype(F32)[:, None] * inv_freq[None, :]

    def over_t(row_tab, col_tab):
        nf = row_tab.shape[1]
        r = jnp.broadcast_to(row_tab[:, None, :], (n_rows, GRID_W, nf)).reshape(SEQ, nf)
        c = jnp.broadcast_to(col_tab[None, :, :], (n_rows, GRID_W, nf)).reshape(SEQ, nf)
        return r, c

    cr, cc = over_t(jnp.cos(ang_row), jnp.cos(ang_col))
    sr, sc = over_t(jnp.sin(ang_row), jnp.sin(ang_col))
    zero = jnp.zeros_like(sr)
    cos_t = jnp.concatenate([cr, cr, cc, cc], axis=-1)
    sin_up = jnp.concatenate([-sr, zero, -sc, zero], axis=-1)
    sin_dn = jnp.concatenate([zero, sr, zero, sc], axis=-1)
    sin_t = jnp.concatenate([-sr, sr, -sc, sc], axis=-1)
    return (cos_t, sin_up, sin_dn), (cos_t.T, sin_t.T)


def kernel(x, p, norm_mix, w_in, w_dw, conv_ln_g, conv_ln_b, w_conv_proj, q_norm, k_norm,
           w_attn_proj, w_out, norm_ffn, w_ff1, w_ff2, norm_ple, w_ple_gate, w_ple_proj, norm_final):
    depth = w_in.shape[0]
    assert depth == 1, "the final norm is fused into the last layer's kernel"
    tabs, tabs_t = _rope_tables()
    q_scale = (HEAD_DIM ** -0.5) * math.log2(math.e)
    xs = x[0]
    for li in range(depth):
        w_in_b = w_in[li].astype(BF16)
        wq_t = w_in[li][:, OFF_Q:OFF_Q + Q_W].T.astype(BF16)
        wv_t = w_in[li][:, OFF_V:OFF_V + KV_W].T.astype(BF16)
        h, h_t = _rmsnorm_cast(xs, norm_mix[li])
        u = _glu_proj(h, w_in_b)
        qt = _qt_proj(wq_t, h_t, q_norm[li], tabs_t, q_scale)
        k = _k_proj(h, w_in_b, k_norm[li], tabs)
        vt = _vt_proj(wv_t, h_t)
        gates = _act_proj(h, w_in_b, OFF_G, 2 * D_MODEL, "sigmoid", "gate_proj")
        m_c = _conv_branch(u, w_dw[li], conv_ln_g[li], conv_ln_b[li],
                           w_conv_proj[li].astype(BF16), gates)
        o = _attention(qt, k, vt)
        merged = _merge(o, w_attn_proj[li].astype(BF16), m_c, gates)
        x1, h2 = _out_proj(merged, w_out[li].astype(BF16), xs, norm_ffn[li])
        a = _act_proj(h2, w_ff1[li].astype(BF16), 0, D_FF, "relu2", "ffn_up")
        x2 = _ffn2(a, w_ff2[li].astype(BF16), x1)
        xs = _ple_final(x2, p[li, 0], norm_ple[li], w_ple_gate[li].astype(BF16),
                        w_ple_proj[li].astype(BF16), norm_final)
    return xs[None]
```

```python
import functools
import math

import jax
import jax.numpy as jnp
from jax import lax
from jax.experimental import pallas as pl
from jax.experimental.pallas import tpu as pltpu

D_MODEL = 2048
SEQ = 8192
N_HEADS = 16
N_KV_HEADS = 4
HEAD_DIM = 128
GROUP = N_HEADS // N_KV_HEADS
ROPE_THETA = 10000.0
AXIS_DIM = HEAD_DIM // 2
GRID_W = 64
CONV_WIDTH = D_MODEL // 2
CONV_KERNEL = 31
CONV_HALO = 16
D_FF = 4 * D_MODEL
PLE_DIM = 256
EPS = 1e-6
Q_W = N_HEADS * HEAD_DIM
KV_W = N_KV_HEADS * HEAD_DIM
LANES = 128
SUBLANES = 8
BF16_SUBLANES = 16
V_ROWS = HEAD_DIM + BF16_SUBLANES

OFF_CA = 0
OFF_CB = CONV_WIDTH
OFF_Q = 2 * CONV_WIDTH
OFF_K = OFF_Q + Q_W
OFF_V = OFF_K + KV_W
OFF_G = OFF_V + KV_W

Q_SCALE = (HEAD_DIM ** -0.5) * math.log2(math.e)
SCORE_BOUND_SLACK = 1.02
MAX_FIXED_SHIFT = 60.0

BF16 = jnp.bfloat16
F32 = jnp.float32
MIB = 1024 * 1024


def _params(sem, vmem_mib):
    return pltpu.CompilerParams(dimension_semantics=sem, vmem_limit_bytes=vmem_mib * MIB)


def _sigmoid(v):
    return 1.0 / (1.0 + jnp.exp(-v))


def _rms_rows(v, g):
    ms = jnp.mean(v * v, axis=-1, keepdims=True)
    return v * lax.rsqrt(ms + EPS) * g


def _cast_once(step, w_ref, wb_ref, transpose=False):
    @pl.when(step == 0)
    def _():
        w = w_ref[...]
        wb_ref[...] = (w.T if transpose else w).astype(wb_ref.dtype)


def _rmsnorm_kernel(x_ref, g_ref, o_ref, ot_ref):
    y = _rms_rows(x_ref[...], g_ref[...])
    o_ref[...] = y.astype(o_ref.dtype)
    for cb in range(0, y.shape[1], LANES):
        ot_ref[cb:cb + LANES, :] = y[:, cb:cb + LANES].T.astype(ot_ref.dtype)


def _rmsnorm_cast(x, g, tm=512):
    s, d = x.shape
    return pl.pallas_call(
        _rmsnorm_kernel,
        grid=(s // tm,),
        in_specs=[pl.BlockSpec((tm, d), lambda i: (i, 0)),
                  pl.BlockSpec((1, d), lambda i: (0, 0))],
        out_specs=[pl.BlockSpec((tm, d), lambda i: (i, 0)),
                   pl.BlockSpec((d, tm), lambda i: (0, i))],
        out_shape=[jax.ShapeDtypeStruct((s, d), BF16),
                   jax.ShapeDtypeStruct((d, s), BF16)],
        compiler_params=_params(("arbitrary",), 32),
        name="rmsnorm_cast",
    )(x, g.reshape(1, d))


def _glu_kernel(h_ref, wa_ref, wb_ref, o_ref, wab_ref, wbb_ref):
    _cast_once(pl.program_id(1), wa_ref, wab_ref)
    _cast_once(pl.program_id(1), wb_ref, wbb_ref)
    h = h_ref[...]
    a = jnp.dot(h, wab_ref[...], preferred_element_type=F32)
    b = jnp.dot(h, wbb_ref[...], preferred_element_type=F32)
    o_ref[...] = a * _sigmoid(b)


def _glu_proj(h, w, tm=1024, tn=512):
    s, d = h.shape
    nb = CONV_WIDTH // tn
    return pl.pallas_call(
        _glu_kernel,
        grid=(nb, s // tm),
        in_specs=[pl.BlockSpec((tm, d), lambda j, i: (i, 0)),
                  pl.BlockSpec((d, tn), lambda j, i: (0, OFF_CA // tn + j)),
                  pl.BlockSpec((d, tn), lambda j, i: (0, OFF_CB // tn + j))],
        out_specs=pl.BlockSpec((tm, tn), lambda j, i: (i, j)),
        out_shape=jax.ShapeDtypeStruct((s, CONV_WIDTH), F32),
        scratch_shapes=[pltpu.VMEM((d, tn), BF16), pltpu.VMEM((d, tn), BF16)],
        compiler_params=_params(("arbitrary", "arbitrary"), 44),
        name="glu_proj",
    )(h, w, w)


def _k_kernel(h_ref, w_ref, g_ref, cos_ref, sa_ref, sb_ref, o_ref, wb_ref):
    _cast_once(pl.program_id(0), w_ref, wb_ref)
    z = jnp.dot(h_ref[...], wb_ref[...], preferred_element_type=F32)
    g = g_ref[...]
    cos = cos_ref[...]
    sa = sa_ref[...]
    sb = sb_ref[...]
    for hh in range(z.shape[1] // HEAD_DIM):
        zh = z[:, hh * HEAD_DIM:(hh + 1) * HEAD_DIM]
        ms = jnp.mean(zh * zh, axis=-1, keepdims=True)
        y = zh * lax.rsqrt(ms + EPS) * g
        up = pltpu.roll(y, HEAD_DIM - AXIS_DIM // 2, 1)
        dn = pltpu.roll(y, AXIS_DIM // 2, 1)
        r = y * cos + up * sa + dn * sb
        o_ref[:, hh * HEAD_DIM:(hh + 1) * HEAD_DIM] = r.astype(o_ref.dtype)


def _k_proj(h, w, gain, tabs, tm=1024):
    s, d = h.shape
    cos_t, sa_t, sb_t = tabs
    tab_spec = pl.BlockSpec((tm, HEAD_DIM), lambda i: (i, 0))
    return pl.pallas_call(
        _k_kernel,
        grid=(s // tm,),
        in_specs=[pl.BlockSpec((tm, d), lambda i: (i, 0)),
                  pl.BlockSpec((d, KV_W), lambda i: (0, OFF_K // KV_W)),
                  pl.BlockSpec((1, HEAD_DIM), lambda i: (0, 0)),
                  tab_spec, tab_spec, tab_spec],
        out_specs=pl.BlockSpec((tm, KV_W), lambda i: (i, 0)),
        out_shape=jax.ShapeDtypeStruct((s, KV_W), BF16),
        scratch_shapes=[pltpu.VMEM((d, KV_W), BF16)],
        compiler_params=_params(("arbitrary",), 40),
        name="k_proj",
    )(h, w, gain.reshape(1, HEAD_DIM), cos_t, sa_t, sb_t)


def _swap_axis_halves(y):
    q = AXIS_DIM // 2
    return jnp.concatenate([y[q:2 * q], y[0:q], y[3 * q:4 * q], y[2 * q:3 * q]], axis=0)


def _qt_kernel(w_ref, ht_ref, g_ref, cos_ref, sin_ref, o_ref, wt_ref):
    _cast_once(pl.program_id(1), w_ref, wt_ref, transpose=True)
    zt = jnp.dot(wt_ref[...], ht_ref[...], preferred_element_type=F32)
    g = jnp.broadcast_to(g_ref[...], (HEAD_DIM, zt.shape[1]))
    cos = cos_ref[...]
    sin = sin_ref[...]
    for hh in range(zt.shape[0] // HEAD_DIM):
        zh = zt[hh * HEAD_DIM:(hh + 1) * HEAD_DIM, :]
        ms = jnp.mean(zh * zh, axis=0, keepdims=True)
        y = zh * lax.rsqrt(ms + EPS) * g
        r = (y * cos + _swap_axis_halves(y) * sin) * Q_SCALE
        o_ref[hh * HEAD_DIM:(hh + 1) * HEAD_DIM, :] = r.astype(o_ref.dtype)


def _qt_proj(w, h_t, gain, tabs_t, tm=1024, tn=1024):
    d, s = h_t.shape
    cos_t, sin_t = tabs_t
    tab_spec = pl.BlockSpec((HEAD_DIM, tm), lambda j, i: (0, i))
    return pl.pallas_call(
        _qt_kernel,
        grid=(Q_W // tn, s // tm),
        in_specs=[pl.BlockSpec((d, tn), lambda j, i: (0, OFF_Q // tn + j)),
                  pl.BlockSpec((d, tm), lambda j, i: (0, i)),
                  pl.BlockSpec((HEAD_DIM, 1), lambda j, i: (0, 0)),
                  tab_spec, tab_spec],
        out_specs=pl.BlockSpec((tn, tm), lambda j, i: (j, i)),
        out_shape=jax.ShapeDtypeStruct((Q_W, s), BF16),
        scratch_shapes=[pltpu.VMEM((tn, d), BF16)],
        compiler_params=_params(("arbitrary", "arbitrary"), 48),
        name="qt_proj",
    )(w, h_t, gain.reshape(HEAD_DIM, 1), cos_t, sin_t)


def _vt_kernel(w_ref, ht_ref, o_ref, wt_ref):
    _cast_once(pl.program_id(0), w_ref, wt_ref, transpose=True)
    zt = jnp.dot(wt_ref[...], ht_ref[...], preferred_element_type=F32)
    pad_rows = V_ROWS - HEAD_DIM
    row = lax.broadcasted_iota(jnp.int32, (pad_rows, zt.shape[1]), 0)
    ones_then_zeros = jnp.where(row == 0, 1.0, 0.0).astype(o_ref.dtype)
    for hh in range(N_KV_HEADS):
        o_ref[hh * V_ROWS:hh * V_ROWS + HEAD_DIM, :] = (
            zt[hh * HEAD_DIM:(hh + 1) * HEAD_DIM, :].astype(o_ref.dtype))
        o_ref[hh * V_ROWS + HEAD_DIM:(hh + 1) * V_ROWS, :] = ones_then_zeros


def _vt_proj(w, h_t, tm=1024):
    d, s = h_t.shape
    return pl.pallas_call(
        _vt_kernel,
        grid=(s // tm,),
        in_specs=[pl.BlockSpec((d, KV_W), lambda i: (0, OFF_V // KV_W)),
                  pl.BlockSpec((d, tm), lambda i: (0, i))],
        out_specs=pl.BlockSpec((N_KV_HEADS * V_ROWS, tm), lambda i: (0, i)),
        out_shape=jax.ShapeDtypeStruct((N_KV_HEADS * V_ROWS, s), BF16),
        scratch_shapes=[pltpu.VMEM((KV_W, d), BF16)],
        compiler_params=_params(("arbitrary",), 40),
        name="vt_proj",
    )(w, h_t)


def _act_kernel(h_ref, w_ref, o_ref, wb_ref, *, act):
    _cast_once(pl.program_id(1), w_ref, wb_ref)
    z = jnp.dot(h_ref[...], wb_ref[...], preferred_element_type=F32)
    if act == "sigmoid":
        z = _sigmoid(z)
    elif act == "relu2":
        z = jnp.square(jnp.maximum(z, 0.0))
    o_ref[...] = z.astype(o_ref.dtype)


def _act_proj(h, w, col_off, width, act, name, tm=1024, tn=1024):
    s, d = h.shape
    return pl.pallas_call(
        functools.partial(_act_kernel, act=act),
        grid=(width // tn, s // tm),
        in_specs=[pl.BlockSpec((tm, d), lambda j, i: (i, 0)),
                  pl.BlockSpec((d, tn), lambda j, i: (0, col_off // tn + j))],
        out_specs=pl.BlockSpec((tm, tn), lambda j, i: (i, j)),
        out_shape=jax.ShapeDtypeStruct((s, width), BF16),
        scratch_shapes=[pltpu.VMEM((d, tn), BF16)],
        compiler_params=_params(("arbitrary", "arbitrary"), 48),
        name=name,
    )(h, w)


def _conv_kernel(u_ref, up_ref, un_ref, wdw_ref, lng_ref, lnb_ref, wp_ref, gate_ref,
                 o_ref, buf_ref, cv_ref, sh_ref, wpb_ref, *, tm, rows, lanes):
    i = pl.program_id(0)
    last = pl.num_programs(0) - 1
    _cast_once(i, wp_ref, wpb_ref)
    prev_ok = (i > 0).astype(F32)
    next_ok = (i < last).astype(F32)
    buf_ref[0:CONV_HALO, :] = up_ref[...] * prev_ok
    buf_ref[CONV_HALO:CONV_HALO + tm, :] = u_ref[...]
    buf_ref[CONV_HALO + tm:, :] = un_ref[...] * next_ok

    base = CONV_HALO - CONV_KERNEL // 2
    span = rows + 2 * CONV_HALO - SUBLANES
    for c in range(0, CONV_WIDTH, lanes):
        def chunk(r, carry, c=c):
            r0 = pl.multiple_of(r * rows, rows)
            win = buf_ref[pl.ds(r0, rows + 2 * CONV_HALO), c:c + lanes]
            for res in range(1, SUBLANES):
                sh_ref[res - 1] = win[res:res + span, :]
            acc = jnp.zeros((rows, lanes), F32)
            for k in range(CONV_KERNEL):
                res = (base + k) % SUBLANES
                off = base + k - res
                if res == 0:
                    tap = win[off:off + rows, :]
                else:
                    tap = sh_ref[res - 1, off:off + rows, :]
                acc = acc + tap * wdw_ref[k:k + 1, c:c + lanes]
            cv_ref[pl.ds(r0, rows), c:c + lanes] = acc
            return carry
        lax.fori_loop(0, tm // rows, chunk, 0)

    cv = cv_ref[...]
    mu = jnp.mean(cv, axis=-1, keepdims=True)
    xc = cv - mu
    var = jnp.mean(xc * xc, axis=-1, keepdims=True)
    y = xc * lax.rsqrt(var + EPS) * lng_ref[...] + lnb_ref[...]
    y = y * _sigmoid(y)
    yc = jnp.dot(y.astype(BF16), wpb_ref[...], preferred_element_type=F32)
    o_ref[...] = gate_ref[...].astype(F32) * yc


def _conv_branch(u, w_dw, ln_g, ln_b, w_proj, gates, tm=512, rows=64, lanes=256):
    s = u.shape[0]
    hb = tm // CONV_HALO
    n_hblk = s // CONV_HALO
    return pl.pallas_call(
        functools.partial(_conv_kernel, tm=tm, rows=rows, lanes=lanes),
        grid=(s // tm,),
        in_specs=[pl.BlockSpec((tm, CONV_WIDTH), lambda i: (i, 0)),
                  pl.BlockSpec((CONV_HALO, CONV_WIDTH), lambda i: (jnp.maximum(i * hb - 1, 0), 0)),
                  pl.BlockSpec((CONV_HALO, CONV_WIDTH), lambda i: (jnp.minimum((i + 1) * hb, n_hblk - 1), 0)),
                  pl.BlockSpec((CONV_KERNEL, CONV_WIDTH), lambda i: (0, 0)),
                  pl.BlockSpec((1, CONV_WIDTH), lambda i: (0, 0)),
                  pl.BlockSpec((1, CONV_WIDTH), lambda i: (0, 0)),
                  pl.BlockSpec((CONV_WIDTH, D_MODEL), lambda i: (0, 0)),
                  pl.BlockSpec((tm, D_MODEL), lambda i: (i, 0))],
        out_specs=pl.BlockSpec((tm, D_MODEL), lambda i: (i, 0)),
        out_shape=jax.ShapeDtypeStruct((s, D_MODEL), F32),
        scratch_shapes=[pltpu.VMEM((tm + 2 * CONV_HALO, CONV_WIDTH), F32),
                        pltpu.VMEM((tm, CONV_WIDTH), F32),
                        pltpu.VMEM((SUBLANES - 1, rows + 2 * CONV_HALO - SUBLANES, lanes), F32),
                        pltpu.VMEM((CONV_WIDTH, D_MODEL), BF16)],
        compiler_params=_params(("arbitrary",), 56),
        name="conv_branch",
    )(u, u, u, w_dw, ln_g.reshape(1, -1), ln_b.reshape(1, -1), w_proj, gates)


def _stage_queries(qt_ref, qs_ref, tq):
    for hh in range(GROUP):
        qs_ref[:, hh * tq:(hh + 1) * tq] = qt_ref[hh * HEAD_DIM:(hh + 1) * HEAD_DIM, :]


def _store_attention_out(acc_ref, o_ref, tq):
    o_t = acc_ref[0:HEAD_DIM, :] / acc_ref[HEAD_DIM:HEAD_DIM + 1, :]
    for hh in range(GROUP):
        o_ref[:, hh * HEAD_DIM:(hh + 1) * HEAD_DIM] = o_t[:, hh * tq:(hh + 1) * tq].T.astype(o_ref.dtype)


def _chunk_start(c, tk):
    return c * tk if isinstance(c, int) else pl.multiple_of(c * tk, tk)


def _attn_online_kernel(qt_ref, k_ref, vt_ref, o_ref, qs_ref, acc_ref, s0_ref, s1_ref, *, tq, tk):
    m_cols = GROUP * tq
    n_chunks = k_ref.shape[0] // tk
    assert n_chunks % 2 == 0 and n_chunks >= 2
    _stage_queries(qt_ref, qs_ref, tq)
    acc_ref[...] = jnp.zeros(acc_ref.shape, F32)

    def scores(c, dst_ref):
        s = jnp.dot(k_ref[pl.ds(_chunk_start(c, tk), tk), :], qs_ref[...], preferred_element_type=F32)
        dst_ref[...] = s
        return jnp.max(s, axis=0, keepdims=True)

    def update(c, src_ref, col_max, m_prev):
        m_new = jnp.maximum(m_prev, col_max)
        alpha = jnp.exp2(m_prev - m_new)
        p = jnp.exp2(src_ref[...] - m_new).astype(BF16)
        vtc = vt_ref[:, pl.ds(_chunk_start(c, tk), tk)]
        acc_ref[...] = alpha * acc_ref[...] + jnp.dot(vtc, p, preferred_element_type=F32)
        return m_new

    def pair(j, carry):
        m_run, cm0 = carry
        c = 2 * j
        cm1 = scores(c + 1, s1_ref)
        m_run = update(c, s0_ref, cm0, m_run)
        cm2 = scores(c + 2, s0_ref)
        m_run = update(c + 1, s1_ref, cm1, m_run)
        return m_run, cm2

    cm0 = scores(0, s0_ref)
    carry = (jnp.full((1, m_cols), -1e30, F32), cm0)
    m_run, cm0 = lax.fori_loop(0, n_chunks // 2 - 1, pair, carry)
    cm1 = scores(n_chunks - 1, s1_ref)
    m_run = update(n_chunks - 2, s0_ref, cm0, m_run)
    update(n_chunks - 1, s1_ref, cm1, m_run)
    _store_attention_out(acc_ref, o_ref, tq)


def _attn_bounded_kernel(b_ref, qt_ref, k_ref, vt_ref, o_ref, qs_ref, acc_ref, p0_ref, p1_ref, *, tq, tk):
    m_cols = GROUP * tq
    n_chunks = k_ref.shape[0] // tk
    assert n_chunks % 2 == 0 and n_chunks >= 2
    _stage_queries(qt_ref, qs_ref, tq)
    acc_ref[...] = jnp.zeros(acc_ref.shape, F32)
    bound = b_ref[0, 0]

    def probs(c, dst_ref, l_run):
        s = jnp.dot(k_ref[pl.ds(_chunk_start(c, tk), tk), :], qs_ref[...], preferred_element_type=F32)
        p = jnp.exp2(s - bound)
        dst_ref[...] = p.astype(BF16)
        return l_run + jnp.sum(p, axis=0, keepdims=True)

    def accumulate(c, src_ref):
        vtc = vt_ref[0:HEAD_DIM, pl.ds(_chunk_start(c, tk), tk)]
        acc_ref[...] += jnp.dot(vtc, src_ref[...], preferred_element_type=F32)

    def pair(j, l_run):
        c = 2 * j
        l_run = probs(c + 1, p1_ref, l_run)
        accumulate(c, p0_ref)
        l_run = probs(c + 2, p0_ref, l_run)
        accumulate(c + 1, p1_ref)
        return l_run

    l_run = probs(0, p0_ref, jnp.zeros((1, m_cols), F32))
    l_run = lax.fori_loop(0, n_chunks // 2 - 1, pair, l_run)
    l_run = probs(n_chunks - 1, p1_ref, l_run)
    accumulate(n_chunks - 2, p0_ref)
    accumulate(n_chunks - 1, p1_ref)
    o_t = acc_ref[...] / l_run
    for hh in range(GROUP):
        o_ref[:, hh * HEAD_DIM:(hh + 1) * HEAD_DIM] = o_t[:, hh * tq:(hh + 1) * tq].T.astype(o_ref.dtype)


def _attention(qt, k, vt, q_gain, k_gain, tq_online=256, tq_bounded=512, tk=512):
    s = k.shape[0]
    gw = GROUP * HEAD_DIM

    def call(body, tq, extra_specs, acc_rows, buf_dtype, name, *args):
        m_cols = GROUP * tq
        return pl.pallas_call(
            functools.partial(body, tq=tq, tk=tk),
            grid=(N_KV_HEADS, s // tq),
            in_specs=extra_specs + [pl.BlockSpec((gw, tq), lambda g, i: (g, i)),
                                    pl.BlockSpec((s, HEAD_DIM), lambda g, i: (0, g)),
                                    pl.BlockSpec((V_ROWS, s), lambda g, i: (g, 0))],
            out_specs=pl.BlockSpec((tq, gw), lambda g, i: (i, g)),
            out_shape=jax.ShapeDtypeStruct((s, Q_W), BF16),
            scratch_shapes=[pltpu.VMEM((HEAD_DIM, m_cols), BF16),
                            pltpu.VMEM((acc_rows, m_cols), F32),
                            pltpu.VMEM((tk, m_cols), buf_dtype),
                            pltpu.VMEM((tk, m_cols), buf_dtype)],
            compiler_params=_params(("arbitrary", "arbitrary"), 48),
            name=name)(*args)

    def online(_):
        return call(_attn_online_kernel, tq_online, [], V_ROWS, F32, "gqa_online", qt, k, vt)

    def bounded(b):
        return call(_attn_bounded_kernel, tq_bounded, [pl.BlockSpec(memory_space=pltpu.SMEM)],
                    HEAD_DIM, BF16, "gqa_bounded", b, qt, k, vt)

    bound = (HEAD_DIM * Q_SCALE * SCORE_BOUND_SLACK
             * jnp.max(jnp.abs(q_gain)) * jnp.max(jnp.abs(k_gain))).astype(F32)
    return lax.cond(bound <= MAX_FIXED_SHIFT, bounded, online, bound.reshape(1, 1))


def _merge_kernel(o_ref, w_ref, mc_ref, ga_ref, out_ref, wb_ref):
    _cast_once(pl.program_id(1), w_ref, wb_ref)
    ya = jnp.dot(o_ref[...], wb_ref[...], preferred_element_type=F32)
    out_ref[...] = (mc_ref[...] + ga_ref[...].astype(F32) * ya).astype(out_ref.dtype)


def _merge(o, w_ap, m_c, gates, tm=1024, tn=1024):
    s, d = o.shape
    ga_off = D_MODEL // tn
    return pl.pallas_call(
        _merge_kernel,
        grid=(D_MODEL // tn, s // tm),
        in_specs=[pl.BlockSpec((tm, d), lambda j, i: (i, 0)),
                  pl.BlockSpec((d, tn), lambda j, i: (0, j)),
                  pl.BlockSpec((tm, tn), lambda j, i: (i, j)),
                  pl.BlockSpec((tm, tn), lambda j, i: (i, ga_off + j))],
        out_specs=pl.BlockSpec((tm, tn), lambda j, i: (i, j)),
        out_shape=jax.ShapeDtypeStruct((s, D_MODEL), BF16),
        scratch_shapes=[pltpu.VMEM((d, tn), BF16)],
        compiler_params=_params(("arbitrary", "arbitrary"), 56),
        name="merge_attn_proj",
    )(o, w_ap, m_c, gates)


def _out_kernel(a_ref, w_ref, x_ref, g_ref, x1_ref, h_ref, wb_ref):
    _cast_once(pl.program_id(0), w_ref, wb_ref)
    x1 = x_ref[...] + jnp.dot(a_ref[...], wb_ref[...], preferred_element_type=F32)
    x1_ref[...] = x1
    h_ref[...] = _rms_rows(x1, g_ref[...]).astype(h_ref.dtype)


def _out_proj(a, w, x, g, tm=256):
    s, d = x.shape
    return pl.pallas_call(
        _out_kernel,
        grid=(s // tm,),
        in_specs=[pl.BlockSpec((tm, d), lambda i: (i, 0)),
                  pl.BlockSpec((d, d), lambda i: (0, 0), pipeline_mode=pl.Buffered(1)),
                  pl.BlockSpec((tm, d), lambda i: (i, 0)),
                  pl.BlockSpec((1, d), lambda i: (0, 0))],
        out_specs=[pl.BlockSpec((tm, d), lambda i: (i, 0)),
                   pl.BlockSpec((tm, d), lambda i: (i, 0))],
        out_shape=[jax.ShapeDtypeStruct((s, d), F32),
                   jax.ShapeDtypeStruct((s, d), BF16)],
        scratch_shapes=[pltpu.VMEM((d, d), BF16)],
        compiler_params=_params(("arbitrary",), 56),
        name="out_proj_residual",
    )(a, w, x, g.reshape(1, d))


def _ffn2_kernel(a_ref, w_ref, x_ref, o_ref, acc_ref):
    kk = pl.program_id(2)

    @pl.when(kk == 0)
    def _():
        acc_ref[...] = x_ref[...]

    acc_ref[...] += jnp.dot(a_ref[...], w_ref[...], preferred_element_type=F32)

    @pl.when(kk == pl.num_programs(2) - 1)
    def _():
        o_ref[...] = acc_ref[...]


def _ffn2(a, w, x, tm=1024, tn=1024, tk=2048):
    s, kdim = a.shape
    d = x.shape[1]
    return pl.pallas_call(
        _ffn2_kernel,
        grid=(s // tm, d // tn, kdim // tk),
        in_specs=[pl.BlockSpec((tm, tk), lambda i, j, k: (i, k)),
                  pl.BlockSpec((tk, tn), lambda i, j, k: (k, j)),
                  pl.BlockSpec((tm, tn), lambda i, j, k: (i, j))],
        out_specs=pl.BlockSpec((tm, tn), lambda i, j, k: (i, j)),
        out_shape=jax.ShapeDtypeStruct((s, d), F32),
        scratch_shapes=[pltpu.VMEM((tm, tn), F32)],
        compiler_params=_params(("parallel", "parallel", "arbitrary"), 48),
        name="ffn_down_residual",
    )(a, w, x)


def _ple_kernel(x_ref, p_ref, gp_ref, wg_ref, wp_ref, gf_ref, o_ref, wgb_ref, wpb_ref):
    _cast_once(pl.program_id(0), wg_ref, wgb_ref)
    _cast_once(pl.program_id(0), wp_ref, wpb_ref)
    x = x_ref[...]
    h = _rms_rows(x, gp_ref[...]).astype(BF16)
    gate = _sigmoid(jnp.dot(h, wgb_ref[...], preferred_element_type=F32))
    pp = jnp.dot(p_ref[...].astype(BF16), wpb_ref[...], preferred_element_type=F32)
    x3 = x + gate * pp
    o_ref[...] = _rms_rows(x3, gf_ref[...])


def _ple_final(x, p, g_ple, w_gate, w_proj, g_final, tm=256):
    s, d = x.shape
    return pl.pallas_call(
        _ple_kernel,
        grid=(s // tm,),
        in_specs=[pl.BlockSpec((tm, d), lambda i: (i, 0)),
                  pl.BlockSpec((tm, PLE_DIM), lambda i: (i, 0)),
                  pl.BlockSpec((1, d), lambda i: (0, 0)),
                  pl.BlockSpec((d, d), lambda i: (0, 0), pipeline_mode=pl.Buffered(1)),
                  pl.BlockSpec((PLE_DIM, d), lambda i: (0, 0), pipeline_mode=pl.Buffered(1)),
                  pl.BlockSpec((1, d), lambda i: (0, 0))],
        out_specs=pl.BlockSpec((tm, d), lambda i: (i, 0)),
        out_shape=jax.ShapeDtypeStruct((s, d), F32),
        scratch_shapes=[pltpu.VMEM((d, d), BF16), pltpu.VMEM((PLE_DIM, d), BF16)],
        compiler_params=_params(("arbitrary",), 56),
        name="ple_final_norm",
    )(x, p, g_ple.reshape(1, d), w_gate, w_proj, g_final.reshape(1, d))


def _rope_tables():
    n_rows = SEQ // GRID_W
    inv_freq = ROPE_THETA ** (-jnp.arange(0, AXIS_DIM, 2, dtype=F32) / AXIS_DIM)
    ang_row = jnp.arange(n_rows, dtype=jnp.int32).astype(F32)[:, None] * inv_freq[None, :]
    ang_col = jnp.arange(GRID_W, dtype=jnp.int32).astype(F32)[:, None] * inv_freq[None, :]

    def over_t(row_tab, col_tab):
        nf = row_tab.shape[1]
        r = jnp.broadcast_to(row_tab[:, None, :], (n_rows, GRID_W, nf)).reshape(SEQ, nf)
        c = jnp.broadcast_to(col_tab[None, :, :], (n_rows, GRID_W, nf)).reshape(SEQ, nf)
        return r, c

    cr, cc = over_t(jnp.cos(ang_row), jnp.cos(ang_col))
    sr, sc = over_t(jnp.sin(ang_row), jnp.sin(ang_col))
    zero = jnp.zeros_like(sr)
    cos_t = jnp.concatenate([cr, cr, cc, cc], axis=-1)
    sin_up = jnp.concatenate([-sr, zero, -sc, zero], axis=-1)
    sin_dn = jnp.concatenate([zero, sr, zero, sc], axis=-1)
    sin_t = jnp.concatenate([-sr, sr, -sc, sc], axis=-1)
    return (cos_t, sin_up, sin_dn), (cos_t.T, sin_t.T)


def kernel(x, p, norm_mix, w_in, w_dw, conv_ln_g, conv_ln_b, w_conv_proj, q_norm, k_norm,
           w_attn_proj, w_out, norm_ffn, w_ff1, w_ff2, norm_ple, w_ple_gate, w_ple_proj, norm_final):
    depth = w_in.shape[0]
    assert depth == 1, "the final norm is fused into the last layer's kernel"
    tabs, tabs_t = _rope_tables()
    xs = x[0]
    for li in range(depth):
        w_i = w_in[li]
        h, h_t = _rmsnorm_cast(xs, norm_mix[li])
        u = _glu_proj(h, w_i)
        qt = _qt_proj(w_i, h_t, q_norm[li], tabs_t)
        k = _k_proj(h, w_i, k_norm[li], tabs)
        vt = _vt_proj(w_i, h_t)
        gates = _act_proj(h, w_i, OFF_G, 2 * D_MODEL, "sigmoid", "gate_proj")
        m_c = _conv_branch(u, w_dw[li], conv_ln_g[li], conv_ln_b[li], w_conv_proj[li], gates)
        o = _attention(qt, k, vt, q_norm[li], k_norm[li])
        merged = _merge(o, w_attn_proj[li], m_c, gates)
        x1, h2 = _out_proj(merged, w_out[li], xs, norm_ffn[li])
        a = _act_proj(h2, w_ff1[li], 0, D_FF, "relu2", "ffn_up")
        x2 = _ffn2(a, w_ff2[li].astype(BF16), x1)
        xs = _ple_final(x2, p[li, 0], norm_ple[li], w_ple_gate[li], w_ple_proj[li], norm_final)
    return xs[None]
```

```python
import functools
import math

import jax
import jax.numpy as jnp
from jax import lax
from jax.experimental import pallas as pl
from jax.experimental.pallas import tpu as pltpu

D_MODEL = 2048
SEQ = 8192
N_HEADS = 16
N_KV_HEADS = 4
HEAD_DIM = 128
GROUP = N_HEADS // N_KV_HEADS
ROPE_THETA = 10000.0
AXIS_DIM = HEAD_DIM // 2
GRID_W = 64
CONV_WIDTH = D_MODEL // 2
CONV_KERNEL = 31
CONV_HALO = 16
D_FF = 4 * D_MODEL
PLE_DIM = 256
EPS = 1e-6
Q_W = N_HEADS * HEAD_DIM
KV_W = N_KV_HEADS * HEAD_DIM
LANES = 128
SUBLANES = 8
BF16_SUBLANES = 16
V_ROWS = HEAD_DIM + BF16_SUBLANES

OFF_CA = 0
OFF_CB = CONV_WIDTH
OFF_Q = 2 * CONV_WIDTH
OFF_K = OFF_Q + Q_W
OFF_V = OFF_K + KV_W
OFF_G = OFF_V + KV_W

Q_SCALE = (HEAD_DIM ** -0.5) * math.log2(math.e)
SCORE_BOUND_SLACK = 1.02
MAX_FIXED_SHIFT = 60.0

BF16 = jnp.bfloat16
F32 = jnp.float32
MIB = 1024 * 1024


def _params(sem, vmem_mib):
    return pltpu.CompilerParams(dimension_semantics=sem, vmem_limit_bytes=vmem_mib * MIB)


def _sigmoid(v):
    return 1.0 / (1.0 + jnp.exp(-v))


def _rms_rows(v, g):
    ms = jnp.mean(v * v, axis=-1, keepdims=True)
    return v * lax.rsqrt(ms + EPS) * g


def _cast_once(step, w_ref, wb_ref, transpose=False):
    @pl.when(step == 0)
    def _():
        w = w_ref[...]
        wb_ref[...] = (w.T if transpose else w).astype(wb_ref.dtype)


def _rmsnorm_kernel(x_ref, g_ref, o_ref, ot_ref):
    y = _rms_rows(x_ref[...], g_ref[...])
    o_ref[...] = y.astype(o_ref.dtype)
    for cb in range(0, y.shape[1], LANES):
        ot_ref[cb:cb + LANES, :] = y[:, cb:cb + LANES].T.astype(ot_ref.dtype)


def _rmsnorm_cast(x, g, tm=512):
    s, d = x.shape
    return pl.pallas_call(
        _rmsnorm_kernel,
        grid=(s // tm,),
        in_specs=[pl.BlockSpec((tm, d), lambda i: (i, 0)),
                  pl.BlockSpec((1, d), lambda i: (0, 0))],
        out_specs=[pl.BlockSpec((tm, d), lambda i: (i, 0)),
                   pl.BlockSpec((d, tm), lambda i: (0, i))],
        out_shape=[jax.ShapeDtypeStruct((s, d), BF16),
                   jax.ShapeDtypeStruct((d, s), BF16)],
        compiler_params=_params(("arbitrary",), 32),
        name="rmsnorm_cast",
    )(x, g.reshape(1, d))


def _glu_kernel(h_ref, wa_ref, wb_ref, o_ref, wab_ref, wbb_ref):
    _cast_once(pl.program_id(1), wa_ref, wab_ref)
    _cast_once(pl.program_id(1), wb_ref, wbb_ref)
    h = h_ref[...]
    a = jnp.dot(h, wab_ref[...], preferred_element_type=F32)
    b = jnp.dot(h, wbb_ref[...], preferred_element_type=F32)
    o_ref[...] = a * _sigmoid(b)


def _glu_proj(h, w, tm=1024, tn=512):
    s, d = h.shape
    nb = CONV_WIDTH // tn
    return pl.pallas_call(
        _glu_kernel,
        grid=(nb, s // tm),
        in_specs=[pl.BlockSpec((tm, d), lambda j, i: (i, 0)),
                  pl.BlockSpec((d, tn), lambda j, i: (0, OFF_CA // tn + j)),
                  pl.BlockSpec((d, tn), lambda j, i: (0, OFF_CB // tn + j))],
        out_specs=pl.BlockSpec((tm, tn), lambda j, i: (i, j)),
        out_shape=jax.ShapeDtypeStruct((s, CONV_WIDTH), F32),
        scratch_shapes=[pltpu.VMEM((d, tn), BF16), pltpu.VMEM((d, tn), BF16)],
        compiler_params=_params(("arbitrary", "arbitrary"), 44),
        name="glu_proj",
    )(h, w, w)


def _swap_axis_halves(y):
    q = AXIS_DIM // 2
    return jnp.concatenate([y[q:2 * q], y[0:q], y[3 * q:4 * q], y[2 * q:3 * q]], axis=0)


def _qk_kernel(w_ref, ht_ref, g_ref, cos_ref, sin_ref, o_ref, wt_ref, *, scale, row_major_out):
    _cast_once(pl.program_id(1), w_ref, wt_ref, transpose=True)
    zt = jnp.dot(wt_ref[...], ht_ref[...], preferred_element_type=F32)
    g = jnp.broadcast_to(g_ref[...], (HEAD_DIM, zt.shape[1]))
    cos = cos_ref[...]
    sin = sin_ref[...]
    for hh in range(zt.shape[0] // HEAD_DIM):
        zh = zt[hh * HEAD_DIM:(hh + 1) * HEAD_DIM, :]
        ms = jnp.mean(zh * zh, axis=0, keepdims=True)
        y = zh * lax.rsqrt(ms + EPS) * g
        r = y * cos + _swap_axis_halves(y) * sin
        if scale != 1.0:
            r = r * scale
        if row_major_out:
            o_ref[:, hh * HEAD_DIM:(hh + 1) * HEAD_DIM] = r.T.astype(o_ref.dtype)
        else:
            o_ref[hh * HEAD_DIM:(hh + 1) * HEAD_DIM, :] = r.astype(o_ref.dtype)


def _qk_proj(w, h_t, gain, tabs_t, col_off, width, scale, row_major_out, name, tm=1024):
    d, s = h_t.shape
    tn = min(width, 1024)
    cos_t, sin_t = tabs_t
    tab_spec = pl.BlockSpec((HEAD_DIM, tm), lambda j, i: (0, i))
    if row_major_out:
        out_spec = pl.BlockSpec((tm, tn), lambda j, i: (i, j))
        out_shape = jax.ShapeDtypeStruct((s, width), BF16)
    else:
        out_spec = pl.BlockSpec((tn, tm), lambda j, i: (j, i))
        out_shape = jax.ShapeDtypeStruct((width, s), BF16)
    return pl.pallas_call(
        functools.partial(_qk_kernel, scale=scale, row_major_out=row_major_out),
        grid=(width // tn, s // tm),
        in_specs=[pl.BlockSpec((d, tn), lambda j, i: (0, col_off // tn + j)),
                  pl.BlockSpec((d, tm), lambda j, i: (0, i)),
                  pl.BlockSpec((HEAD_DIM, 1), lambda j, i: (0, 0)),
                  tab_spec, tab_spec],
        out_specs=out_spec,
        out_shape=out_shape,
        scratch_shapes=[pltpu.VMEM((tn, d), BF16)],
        compiler_params=_params(("arbitrary", "arbitrary"), 48),
        name=name,
    )(w, h_t, gain.reshape(HEAD_DIM, 1), cos_t, sin_t)


def _vt_kernel(w_ref, ht_ref, o_ref, wt_ref):
    _cast_once(pl.program_id(0), w_ref, wt_ref, transpose=True)
    zt = jnp.dot(wt_ref[...], ht_ref[...], preferred_element_type=F32)
    pad_rows = V_ROWS - HEAD_DIM
    row = lax.broadcasted_iota(jnp.int32, (pad_rows, zt.shape[1]), 0)
    ones_then_zeros = jnp.where(row == 0, 1.0, 0.0).astype(o_ref.dtype)
    for hh in range(N_KV_HEADS):
        o_ref[hh * V_ROWS:hh * V_ROWS + HEAD_DIM, :] = (
            zt[hh * HEAD_DIM:(hh + 1) * HEAD_DIM, :].astype(o_ref.dtype))
        o_ref[hh * V_ROWS + HEAD_DIM:(hh + 1) * V_ROWS, :] = ones_then_zeros


def _vt_proj(w, h_t, tm=1024):
    d, s = h_t.shape
    return pl.pallas_call(
        _vt_kernel,
        grid=(s // tm,),
        in_specs=[pl.BlockSpec((d, KV_W), lambda i: (0, OFF_V // KV_W)),
                  pl.BlockSpec((d, tm), lambda i: (0, i))],
        out_specs=pl.BlockSpec((N_KV_HEADS * V_ROWS, tm), lambda i: (0, i)),
        out_shape=jax.ShapeDtypeStruct((N_KV_HEADS * V_ROWS, s), BF16),
        scratch_shapes=[pltpu.VMEM((KV_W, d), BF16)],
        compiler_params=_params(("arbitrary",), 40),
        name="vt_proj",
    )(w, h_t)


def _act_kernel(h_ref, w_ref, o_ref, wb_ref, *, act):
    _cast_once(pl.program_id(1), w_ref, wb_ref)
    z = jnp.dot(h_ref[...], wb_ref[...], preferred_element_type=F32)
    if act == "sigmoid":
        z = _sigmoid(z)
    elif act == "relu2":
        z = jnp.square(jnp.maximum(z, 0.0))
    o_ref[...] = z.astype(o_ref.dtype)


def _act_proj(h, w, col_off, width, act, name, tm=2048, tn=1024):
    s, d = h.shape
    return pl.pallas_call(
        functools.partial(_act_kernel, act=act),
        grid=(width // tn, s // tm),
        in_specs=[pl.BlockSpec((tm, d), lambda j, i: (i, 0)),
                  pl.BlockSpec((d, tn), lambda j, i: (0, col_off // tn + j))],
        out_specs=pl.BlockSpec((tm, tn), lambda j, i: (i, j)),
        out_shape=jax.ShapeDtypeStruct((s, width), BF16),
        scratch_shapes=[pltpu.VMEM((d, tn), BF16)],
        compiler_params=_params(("arbitrary", "arbitrary"), 58),
        name=name,
    )(h, w)


def _conv_kernel(u_ref, up_ref, un_ref, wdw_ref, lng_ref, lnb_ref, wp_ref, gate_ref,
                 o_ref, buf_ref, cv_ref, sh_ref, wpb_ref, *, tm, rows, lanes):
    i = pl.program_id(0)
    last = pl.num_programs(0) - 1
    _cast_once(i, wp_ref, wpb_ref)
    prev_ok = (i > 0).astype(F32)
    next_ok = (i < last).astype(F32)
    buf_ref[0:CONV_HALO, :] = up_ref[...] * prev_ok
    buf_ref[CONV_HALO:CONV_HALO + tm, :] = u_ref[...]
    buf_ref[CONV_HALO + tm:, :] = un_ref[...] * next_ok

    base = CONV_HALO - CONV_KERNEL // 2
    span = rows + 2 * CONV_HALO - SUBLANES
    for c in range(0, CONV_WIDTH, lanes):
        def chunk(r, carry, c=c):
            r0 = pl.multiple_of(r * rows, rows)
            win = buf_ref[pl.ds(r0, rows + 2 * CONV_HALO), c:c + lanes]
            for res in range(1, SUBLANES):
                sh_ref[res - 1] = win[res:res + span, :]
            acc = jnp.zeros((rows, lanes), F32)
            for k in range(CONV_KERNEL):
                res = (base + k) % SUBLANES
                off = base + k - res
                if res == 0:
                    tap = win[off:off + rows, :]
                else:
                    tap = sh_ref[res - 1, off:off + rows, :]
                acc = acc + tap * wdw_ref[k:k + 1, c:c + lanes]
            cv_ref[pl.ds(r0, rows), c:c + lanes] = acc
            return carry
        lax.fori_loop(0, tm // rows, chunk, 0)

    cv = cv_ref[...]
    mu = jnp.mean(cv, axis=-1, keepdims=True)
    xc = cv - mu
    var = jnp.mean(xc * xc, axis=-1, keepdims=True)
    y = xc * lax.rsqrt(var + EPS) * lng_ref[...] + lnb_ref[...]
    y = y * _sigmoid(y)
    yc = jnp.dot(y.astype(BF16), wpb_ref[...], preferred_element_type=F32)
    o_ref[...] = gate_ref[...].astype(F32) * yc


def _conv_branch(u, w_dw, ln_g, ln_b, w_proj, gates, tm=512, rows=64, lanes=256):
    s = u.shape[0]
    hb = tm // CONV_HALO
    n_hblk = s // CONV_HALO
    return pl.pallas_call(
        functools.partial(_conv_kernel, tm=tm, rows=rows, lanes=lanes),
        grid=(s // tm,),
        in_specs=[pl.BlockSpec((tm, CONV_WIDTH), lambda i: (i, 0)),
                  pl.BlockSpec((CONV_HALO, CONV_WIDTH), lambda i: (jnp.maximum(i * hb - 1, 0), 0)),
                  pl.BlockSpec((CONV_HALO, CONV_WIDTH), lambda i: (jnp.minimum((i + 1) * hb, n_hblk - 1), 0)),
                  pl.BlockSpec((CONV_KERNEL, CONV_WIDTH), lambda i: (0, 0)),
                  pl.BlockSpec((1, CONV_WIDTH), lambda i: (0, 0)),
                  pl.BlockSpec((1, CONV_WIDTH), lambda i: (0, 0)),
                  pl.BlockSpec((CONV_WIDTH, D_MODEL), lambda i: (0, 0)),
                  pl.BlockSpec((tm, D_MODEL), lambda i: (i, 0))],
        out_specs=pl.BlockSpec((tm, D_MODEL), lambda i: (i, 0)),
        out_shape=jax.ShapeDtypeStruct((s, D_MODEL), F32),
        scratch_shapes=[pltpu.VMEM((tm + 2 * CONV_HALO, CONV_WIDTH), F32),
                        pltpu.VMEM((tm, CONV_WIDTH), F32),
                        pltpu.VMEM((SUBLANES - 1, rows + 2 * CONV_HALO - SUBLANES, lanes), F32),
                        pltpu.VMEM((CONV_WIDTH, D_MODEL), BF16)],
        compiler_params=_params(("arbitrary",), 56),
        name="conv_branch",
    )(u, u, u, w_dw, ln_g.reshape(1, -1), ln_b.reshape(1, -1), w_proj, gates)


def _stage_queries(qt_ref, qs_ref, tq):
    for hh in range(GROUP):
        qs_ref[:, hh * tq:(hh + 1) * tq] = qt_ref[hh * HEAD_DIM:(hh + 1) * HEAD_DIM, :]


def _store_attention_out(acc_ref, o_ref, tq):
    o_t = acc_ref[0:HEAD_DIM, :] / acc_ref[HEAD_DIM:HEAD_DIM + 1, :]
    for hh in range(GROUP):
        o_ref[:, hh * HEAD_DIM:(hh + 1) * HEAD_DIM] = o_t[:, hh * tq:(hh + 1) * tq].T.astype(o_ref.dtype)


def _chunk_start(c, tk):
    return c * tk if isinstance(c, int) else pl.multiple_of(c * tk, tk)


def _attn_online_kernel(qt_ref, k_ref, vt_ref, o_ref, qs_ref, acc_ref, s0_ref, s1_ref, *, tq, tk):
    m_cols = GROUP * tq
    n_chunks = k_ref.shape[0] // tk
    assert n_chunks % 2 == 0 and n_chunks >= 2
    _stage_queries(qt_ref, qs_ref, tq)
    acc_ref[...] = jnp.zeros(acc_ref.shape, F32)

    def scores(c, dst_ref):
        s = jnp.dot(k_ref[pl.ds(_chunk_start(c, tk), tk), :], qs_ref[...], preferred_element_type=F32)
        dst_ref[...] = s
        return jnp.max(s, axis=0, keepdims=True)

    def update(c, src_ref, col_max, m_prev):
        m_new = jnp.maximum(m_prev, col_max)
        alpha = jnp.exp2(m_prev - m_new)
        p = jnp.exp2(src_ref[...] - m_new).astype(BF16)
        vtc = vt_ref[:, pl.ds(_chunk_start(c, tk), tk)]
        acc_ref[...] = alpha * acc_ref[...] + jnp.dot(vtc, p, preferred_element_type=F32)
        return m_new

    def pair(j, carry):
        m_run, cm0 = carry
        c = 2 * j
        cm1 = scores(c + 1, s1_ref)
        m_run = update(c, s0_ref, cm0, m_run)
        cm2 = scores(c + 2, s0_ref)
        m_run = update(c + 1, s1_ref, cm1, m_run)
        return m_run, cm2

    cm0 = scores(0, s0_ref)
    carry = (jnp.full((1, m_cols), -1e30, F32), cm0)
    m_run, cm0 = lax.fori_loop(0, n_chunks // 2 - 1, pair, carry)
    cm1 = scores(n_chunks - 1, s1_ref)
    m_run = update(n_chunks - 2, s0_ref, cm0, m_run)
    update(n_chunks - 1, s1_ref, cm1, m_run)
    _store_attention_out(acc_ref, o_ref, tq)


def _attn_bounded_kernel(b_ref, qt_ref, k_ref, vt_ref, o_ref, qs_ref, acc_ref, p0_ref, p1_ref, *, tq, tk):
    m_cols = GROUP * tq
    n_chunks = k_ref.shape[0] // tk
    assert n_chunks % 2 == 0 and n_chunks >= 2
    _stage_queries(qt_ref, qs_ref, tq)
    acc_ref[...] = jnp.zeros(acc_ref.shape, F32)
    bound = b_ref[0, 0]

    def probs(c, dst_ref, l_run):
        s = jnp.dot(k_ref[pl.ds(_chunk_start(c, tk), tk), :], qs_ref[...], preferred_element_type=F32)
        p = jnp.exp2(s - bound)
        dst_ref[...] = p.astype(BF16)
        return l_run + jnp.sum(p, axis=0, keepdims=True)

    def accumulate(c, src_ref):
        vtc = vt_ref[0:HEAD_DIM, pl.ds(_chunk_start(c, tk), tk)]
        acc_ref[...] += jnp.dot(vtc, src_ref[...], preferred_element_type=F32)

    def pair(j, l_run):
        c = 2 * j
        l_run = probs(c + 1, p1_ref, l_run)
        accumulate(c, p0_ref)
        l_run = probs(c + 2, p0_ref, l_run)
        accumulate(c + 1, p1_ref)
        return l_run

    l_run = probs(0, p0_ref, jnp.zeros((1, m_cols), F32))
    l_run = lax.fori_loop(0, n_chunks // 2 - 1, pair, l_run)
    l_run = probs(n_chunks - 1, p1_ref, l_run)
    accumulate(n_chunks - 2, p0_ref)
    accumulate(n_chunks - 1, p1_ref)
    o_t = acc_ref[...] / l_run
    for hh in range(GROUP):
        o_ref[:, hh * HEAD_DIM:(hh + 1) * HEAD_DIM] = o_t[:, hh * tq:(hh + 1) * tq].T.astype(o_ref.dtype)


def _attention(qt, k, vt, q_gain, k_gain, tq_online=256, tq_bounded=1024, tk=512):
    s = k.shape[0]
    gw = GROUP * HEAD_DIM

    def call(body, tq, extra_specs, acc_rows, buf_dtype, name, *args):
        m_cols = GROUP * tq
        return pl.pallas_call(
            functools.partial(body, tq=tq, tk=tk),
            grid=(N_KV_HEADS, s // tq),
            in_specs=extra_specs + [pl.BlockSpec((gw, tq), lambda g, i: (g, i)),
                                    pl.BlockSpec((s, HEAD_DIM), lambda g, i: (0, g)),
                                    pl.BlockSpec((V_ROWS, s), lambda g, i: (g, 0))],
            out_specs=pl.BlockSpec((tq, gw), lambda g, i: (i, g)),
            out_shape=jax.ShapeDtypeStruct((s, Q_W), BF16),
            scratch_shapes=[pltpu.VMEM((HEAD_DIM, m_cols), BF16),
                            pltpu.VMEM((acc_rows, m_cols), F32),
                            pltpu.VMEM((tk, m_cols), buf_dtype),
                            pltpu.VMEM((tk, m_cols), buf_dtype)],
            compiler_params=_params(("arbitrary", "arbitrary"), 48),
            name=name)(*args)

    def online(_):
        return call(_attn_online_kernel, tq_online, [], V_ROWS, F32, "gqa_online", qt, k, vt)

    def bounded(b):
        return call(_attn_bounded_kernel, tq_bounded, [pl.BlockSpec(memory_space=pltpu.SMEM)],
                    HEAD_DIM, BF16, "gqa_bounded", b, qt, k, vt)

    bound = (HEAD_DIM * Q_SCALE * SCORE_BOUND_SLACK
             * jnp.max(jnp.abs(q_gain)) * jnp.max(jnp.abs(k_gain))).astype(F32)
    return lax.cond(bound <= MAX_FIXED_SHIFT, bounded, online, bound.reshape(1, 1))


def _merge_kernel(o_ref, w_ref, mc_ref, ga_ref, out_ref, wb_ref):
    _cast_once(pl.program_id(1), w_ref, wb_ref)
    ya = jnp.dot(o_ref[...], wb_ref[...], preferred_element_type=F32)
    out_ref[...] = (mc_ref[...] + ga_ref[...].astype(F32) * ya).astype(out_ref.dtype)


def _merge(o, w_ap, m_c, gates, tm=1024, tn=1024):
    s, d = o.shape
    ga_off = D_MODEL // tn
    return pl.pallas_call(
        _merge_kernel,
        grid=(D_MODEL // tn, s // tm),
        in_specs=[pl.BlockSpec((tm, d), lambda j, i: (i, 0)),
                  pl.BlockSpec((d, tn), lambda j, i: (0, j)),
                  pl.BlockSpec((tm, tn), lambda j, i: (i, j)),
                  pl.BlockSpec((tm, tn), lambda j, i: (i, ga_off + j))],
        out_specs=pl.BlockSpec((tm, tn), lambda j, i: (i, j)),
        out_shape=jax.ShapeDtypeStruct((s, D_MODEL), BF16),
        scratch_shapes=[pltpu.VMEM((d, tn), BF16)],
        compiler_params=_params(("arbitrary", "arbitrary"), 56),
        name="merge_attn_proj",
    )(o, w_ap, m_c, gates)


def _out_kernel(a_ref, w_ref, x_ref, g_ref, x1_ref, h_ref, wb_ref):
    _cast_once(pl.program_id(0), w_ref, wb_ref)
    x1 = x_ref[...] + jnp.dot(a_ref[...], wb_ref[...], preferred_element_type=F32)
    x1_ref[...] = x1
    h_ref[...] = _rms_rows(x1, g_ref[...]).astype(h_ref.dtype)


def _out_proj(a, w, x, g, tm=256):
    s, d = x.shape
    return pl.pallas_call(
        _out_kernel,
        grid=(s // tm,),
        in_specs=[pl.BlockSpec((tm, d), lambda i: (i, 0)),
                  pl.BlockSpec((d, d), lambda i: (0, 0), pipeline_mode=pl.Buffered(1)),
                  pl.BlockSpec((tm, d), lambda i: (i, 0)),
                  pl.BlockSpec((1, d), lambda i: (0, 0))],
        out_specs=[pl.BlockSpec((tm, d), lambda i: (i, 0)),
                   pl.BlockSpec((tm, d), lambda i: (i, 0))],
        out_shape=[jax.ShapeDtypeStruct((s, d), F32),
                   jax.ShapeDtypeStruct((s, d), BF16)],
        scratch_shapes=[pltpu.VMEM((d, d), BF16)],
        compiler_params=_params(("arbitrary",), 56),
        name="out_proj_residual",
    )(a, w, x, g.reshape(1, d))


def _ffn2_kernel(a_ref, w_ref, x_ref, o_ref, acc_ref):
    kk = pl.program_id(2)

    @pl.when(kk == 0)
    def _():
        acc_ref[...] = x_ref[...]

    acc_ref[...] += jnp.dot(a_ref[...], w_ref[...], preferred_element_type=F32)

    @pl.when(kk == pl.num_programs(2) - 1)
    def _():
        o_ref[...] = acc_ref[...]


def _ffn2(a, w, x, tm=1024, tn=1024, tk=2048):
    s, kdim = a.shape
    d = x.shape[1]
    return pl.pallas_call(
        _ffn2_kernel,
        grid=(s // tm, d // tn, kdim // tk),
        in_specs=[pl.BlockSpec((tm, tk), lambda i, j, k: (i, k)),
                  pl.BlockSpec((tk, tn), lambda i, j, k: (k, j)),
                  pl.BlockSpec((tm, tn), lambda i, j, k: (i, j))],
        out_specs=pl.BlockSpec((tm, tn), lambda i, j, k: (i, j)),
        out_shape=jax.ShapeDtypeStruct((s, d), F32),
        scratch_shapes=[pltpu.VMEM((tm, tn), F32)],
        compiler_params=_params(("parallel", "parallel", "arbitrary"), 48),
        name="ffn_down_residual",
    )(a, w, x)


def _ple_kernel(x_ref, p_ref, gp_ref, wg_ref, wp_ref, gf_ref, o_ref, wgb_ref, wpb_ref):
    _cast_once(pl.program_id(0), wg_ref, wgb_ref)
    _cast_once(pl.program_id(0), wp_ref, wpb_ref)
    x = x_ref[...]
    h = _rms_rows(x, gp_ref[...]).astype(BF16)
    gate = _sigmoid(jnp.dot(h, wgb_ref[...], preferred_element_type=F32))
    pp = jnp.dot(p_ref[...].astype(BF16), wpb_ref[...], preferred_element_type=F32)
    x3 = x + gate * pp
    o_ref[...] = _rms_rows(x3, gf_ref[...])


def _ple_final(x, p, g_ple, w_gate, w_proj, g_final, tm=256):
    s, d = x.shape
    return pl.pallas_call(
        _ple_kernel,
        grid=(s // tm,),
        in_specs=[pl.BlockSpec((tm, d), lambda i: (i, 0)),
                  pl.BlockSpec((tm, PLE_DIM), lambda i: (i, 0)),
                  pl.BlockSpec((1, d), lambda i: (0, 0)),
                  pl.BlockSpec((d, d), lambda i: (0, 0), pipeline_mode=pl.Buffered(1)),
                  pl.BlockSpec((PLE_DIM, d), lambda i: (0, 0), pipeline_mode=pl.Buffered(1)),
                  pl.BlockSpec((1, d), lambda i: (0, 0))],
        out_specs=pl.BlockSpec((tm, d), lambda i: (i, 0)),
        out_shape=jax.ShapeDtypeStruct((s, d), F32),
        scratch_shapes=[pltpu.VMEM((d, d), BF16), pltpu.VMEM((PLE_DIM, d), BF16)],
        compiler_params=_params(("arbitrary",), 56),
        name="ple_final_norm",
    )(x, p, g_ple.reshape(1, d), w_gate, w_proj, g_final.reshape(1, d))


def _rope_tables():
    n_rows = SEQ // GRID_W
    inv_freq = ROPE_THETA ** (-jnp.arange(0, AXIS_DIM, 2, dtype=F32) / AXIS_DIM)
    ang_row = inv_freq[:, None] * jnp.arange(n_rows, dtype=jnp.int32).astype(F32)[None, :]
    ang_col = inv_freq[:, None] * jnp.arange(GRID_W, dtype=jnp.int32).astype(F32)[None, :]
    nf = inv_freq.shape[0]

    def over_t(row_tab, col_tab):
        r = jnp.broadcast_to(row_tab[:, :, None], (nf, n_rows, GRID_W)).reshape(nf, SEQ)
        c = jnp.broadcast_to(col_tab[:, None, :], (nf, n_rows, GRID_W)).reshape(nf, SEQ)
        return r, c

    cr, cc = over_t(jnp.cos(ang_row), jnp.cos(ang_col))
    sr, sc = over_t(jnp.sin(ang_row), jnp.sin(ang_col))
    cos_t = jnp.concatenate([cr, cr, cc, cc], axis=0)
    sin_t = jnp.concatenate([-sr, sr, -sc, sc], axis=0)
    return cos_t, sin_t


def kernel(x, p, norm_mix, w_in, w_dw, conv_ln_g, conv_ln_b, w_conv_proj, q_norm, k_norm,
           w_attn_proj, w_out, norm_ffn, w_ff1, w_ff2, norm_ple, w_ple_gate, w_ple_proj, norm_final):
    depth = w_in.shape[0]
    assert depth == 1, "the final norm is fused into the last layer's kernel"
    tabs_t = _rope_tables()
    xs = x[0]
    for li in range(depth):
        w_i = w_in[li]
        h, h_t = _rmsnorm_cast(xs, norm_mix[li])
        u = _glu_proj(h, w_i)
        qt = _qk_proj(w_i, h_t, q_norm[li], tabs_t, OFF_Q, Q_W, Q_SCALE, False, "qt_proj")
        k = _qk_proj(w_i, h_t, k_norm[li], tabs_t, OFF_K, KV_W, 1.0, True, "k_proj")
        vt = _vt_proj(w_i, h_t)
        gates = _act_proj(h, w_i, OFF_G, 2 * D_MODEL, "sigmoid", "gate_proj")
        m_c = _conv_branch(u, w_dw[li], conv_ln_g[li], conv_ln_b[li], w_conv_proj[li], gates)
        o = _attention(qt, k, vt, q_norm[li], k_norm[li])
        merged = _merge(o, w_attn_proj[li], m_c, gates)
        x1, h2 = _out_proj(merged, w_out[li], xs, norm_ffn[li])
        a = _act_proj(h2, w_ff1[li], 0, D_FF, "relu2", "ffn_up")
        x2 = _ffn2(a, w_ff2[li].astype(BF16), x1)
        xs = _ple_final(x2, p[li, 0], norm_ple[li], w_ple_gate[li], w_ple_proj[li], norm_final)
    return xs[None]
```

```python
import functools
import math

import jax
import jax.numpy as jnp
from jax import lax
from jax.experimental import pallas as pl
from jax.experimental.pallas import tpu as pltpu

D_MODEL = 2048
SEQ = 8192
N_HEADS = 16
N_KV_HEADS = 4
HEAD_DIM = 128
GROUP = N_HEADS // N_KV_HEADS
ROPE_THETA = 10000.0
AXIS_DIM = HEAD_DIM // 2
GRID_W = 64
CONV_WIDTH = D_MODEL // 2
CONV_KERNEL = 31
CONV_HALO = 16
CONV_ROWS = 64
CONV_LANES = 128
D_FF = 4 * D_MODEL
PLE_DIM = 256
EPS = 1e-6
Q_W = N_HEADS * HEAD_DIM
KV_W = N_KV_HEADS * HEAD_DIM
LANES = 128
SUBLANES = 8
BF16_SUBLANES = 16
V_ROWS = HEAD_DIM + BF16_SUBLANES

OFF_CA = 0
OFF_CB = CONV_WIDTH
OFF_Q = 2 * CONV_WIDTH
OFF_K = OFF_Q + Q_W
OFF_V = OFF_K + KV_W
OFF_G = OFF_V + KV_W

Q_SCALE = (HEAD_DIM ** -0.5) * math.log2(math.e)
SCORE_BOUND_SLACK = 1.02
MAX_FIXED_SHIFT = 60.0

BF16 = jnp.bfloat16
F32 = jnp.float32
MIB = 1024 * 1024


def _params(sem, vmem_mib):
    return pltpu.CompilerParams(dimension_semantics=sem, vmem_limit_bytes=vmem_mib * MIB)


def _sigmoid(v):
    return 1.0 / (1.0 + jnp.exp(-v))


def _rms_rows(v, g):
    ms = jnp.mean(v * v, axis=-1, keepdims=True)
    return v * lax.rsqrt(ms + EPS) * g


def _cast_once(step, w_ref, wb_ref, transpose=False):
    @pl.when(step == 0)
    def _():
        w = w_ref[...]
        wb_ref[...] = (w.T if transpose else w).astype(wb_ref.dtype)


def _rmsnorm_kernel(x_ref, g_ref, o_ref, ot_ref):
    y = _rms_rows(x_ref[...], g_ref[...])
    o_ref[...] = y.astype(o_ref.dtype)
    for cb in range(0, y.shape[1], LANES):
        ot_ref[cb:cb + LANES, :] = y[:, cb:cb + LANES].T.astype(ot_ref.dtype)


def _rmsnorm_cast(x, g, tm=512):
    s, d = x.shape
    return pl.pallas_call(
        _rmsnorm_kernel,
        grid=(s // tm,),
        in_specs=[pl.BlockSpec((tm, d), lambda i: (i, 0)),
                  pl.BlockSpec((1, d), lambda i: (0, 0))],
        out_specs=[pl.BlockSpec((tm, d), lambda i: (i, 0)),
                   pl.BlockSpec((d, tm), lambda i: (0, i))],
        out_shape=[jax.ShapeDtypeStruct((s, d), BF16),
                   jax.ShapeDtypeStruct((d, s), BF16)],
        compiler_params=_params(("arbitrary",), 32),
        name="rmsnorm_cast",
    )(x, g.reshape(1, d))


def _glu_kernel(h_ref, wa_ref, wb_ref, o_ref, wab_ref, wbb_ref):
    _cast_once(pl.program_id(1), wa_ref, wab_ref)
    _cast_once(pl.program_id(1), wb_ref, wbb_ref)
    h = h_ref[...]
    a = jnp.dot(h, wab_ref[...], preferred_element_type=F32)
    b = jnp.dot(h, wbb_ref[...], preferred_element_type=F32)
    o_ref[...] = a * _sigmoid(b)


def _glu_proj(h, w, tm=1024, tn=512):
    s, d = h.shape
    nb = CONV_WIDTH // tn
    return pl.pallas_call(
        _glu_kernel,
        grid=(nb, s // tm),
        in_specs=[pl.BlockSpec((tm, d), lambda j, i: (i, 0)),
                  pl.BlockSpec((d, tn), lambda j, i: (0, OFF_CA // tn + j)),
                  pl.BlockSpec((d, tn), lambda j, i: (0, OFF_CB // tn + j))],
        out_specs=pl.BlockSpec((tm, tn), lambda j, i: (i, j)),
        out_shape=jax.ShapeDtypeStruct((s, CONV_WIDTH), F32),
        scratch_shapes=[pltpu.VMEM((d, tn), BF16), pltpu.VMEM((d, tn), BF16)],
        compiler_params=_params(("arbitrary", "arbitrary"), 44),
        name="glu_proj",
    )(h, w, w)


def _swap_axis_halves(y):
    q = AXIS_DIM // 2
    return jnp.concatenate([y[q:2 * q], y[0:q], y[3 * q:4 * q], y[2 * q:3 * q]], axis=0)


def _qk_kernel(w_ref, ht_ref, g_ref, cos_ref, sin_ref, o_ref, wt_ref, *, scale, row_major_out):
    _cast_once(pl.program_id(1), w_ref, wt_ref, transpose=True)
    zt = jnp.dot(wt_ref[...], ht_ref[...], preferred_element_type=F32)
    g = jnp.broadcast_to(g_ref[...], (HEAD_DIM, zt.shape[1]))
    cos = cos_ref[...]
    sin = sin_ref[...]
    for hh in range(zt.shape[0] // HEAD_DIM):
        zh = zt[hh * HEAD_DIM:(hh + 1) * HEAD_DIM, :]
        ms = jnp.mean(zh * zh, axis=0, keepdims=True)
        y = zh * lax.rsqrt(ms + EPS) * g
        r = y * cos + _swap_axis_halves(y) * sin
        if scale != 1.0:
            r = r * scale
        if row_major_out:
            o_ref[:, hh * HEAD_DIM:(hh + 1) * HEAD_DIM] = r.T.astype(o_ref.dtype)
        else:
            o_ref[hh * HEAD_DIM:(hh + 1) * HEAD_DIM, :] = r.astype(o_ref.dtype)


def _qk_proj(w, h_t, gain, tabs_t, col_off, width, scale, row_major_out, name, tm=1024):
    d, s = h_t.shape
    tn = min(width, 1024)
    cos_t, sin_t = tabs_t
    tab_spec = pl.BlockSpec((HEAD_DIM, tm), lambda j, i: (0, i))
    if row_major_out:
        out_spec = pl.BlockSpec((tm, tn), lambda j, i: (i, j))
        out_shape = jax.ShapeDtypeStruct((s, width), BF16)
    else:
        out_spec = pl.BlockSpec((tn, tm), lambda j, i: (j, i))
        out_shape = jax.ShapeDtypeStruct((width, s), BF16)
    return pl.pallas_call(
        functools.partial(_qk_kernel, scale=scale, row_major_out=row_major_out),
        grid=(width // tn, s // tm),
        in_specs=[pl.BlockSpec((d, tn), lambda j, i: (0, col_off // tn + j)),
                  pl.BlockSpec((d, tm), lambda j, i: (0, i)),
                  pl.BlockSpec((HEAD_DIM, 1), lambda j, i: (0, 0)),
                  tab_spec, tab_spec],
        out_specs=out_spec,
        out_shape=out_shape,
        scratch_shapes=[pltpu.VMEM((tn, d), BF16)],
        compiler_params=_params(("arbitrary", "arbitrary"), 48),
        name=name,
    )(w, h_t, gain.reshape(HEAD_DIM, 1), cos_t, sin_t)


def _vt_kernel(w_ref, ht_ref, o_ref, wt_ref):
    _cast_once(pl.program_id(0), w_ref, wt_ref, transpose=True)
    zt = jnp.dot(wt_ref[...], ht_ref[...], preferred_element_type=F32)
    pad_rows = V_ROWS - HEAD_DIM
    row = lax.broadcasted_iota(jnp.int32, (pad_rows, zt.shape[1]), 0)
    ones_then_zeros = jnp.where(row == 0, 1.0, 0.0).astype(o_ref.dtype)
    for hh in range(N_KV_HEADS):
        o_ref[hh * V_ROWS:hh * V_ROWS + HEAD_DIM, :] = (
            zt[hh * HEAD_DIM:(hh + 1) * HEAD_DIM, :].astype(o_ref.dtype))
        o_ref[hh * V_ROWS + HEAD_DIM:(hh + 1) * V_ROWS, :] = ones_then_zeros


def _vt_proj(w, h_t, tm=1024):
    d, s = h_t.shape
    return pl.pallas_call(
        _vt_kernel,
        grid=(s // tm,),
        in_specs=[pl.BlockSpec((d, KV_W), lambda i: (0, OFF_V // KV_W)),
                  pl.BlockSpec((d, tm), lambda i: (0, i))],
        out_specs=pl.BlockSpec((N_KV_HEADS * V_ROWS, tm), lambda i: (0, i)),
        out_shape=jax.ShapeDtypeStruct((N_KV_HEADS * V_ROWS, s), BF16),
        scratch_shapes=[pltpu.VMEM((KV_W, d), BF16)],
        compiler_params=_params(("arbitrary",), 40),
        name="vt_proj",
    )(w, h_t)


def _act_kernel(h_ref, w_ref, o_ref, wb_ref, *, act):
    _cast_once(pl.program_id(1), w_ref, wb_ref)
    z = jnp.dot(h_ref[...], wb_ref[...], preferred_element_type=F32)
    if act == "sigmoid":
        z = _sigmoid(z)
    elif act == "relu2":
        z = jnp.square(jnp.maximum(z, 0.0))
    o_ref[...] = z.astype(o_ref.dtype)


def _act_proj(h, w, col_off, width, act, name, tm=2048, tn=1024):
    s, d = h.shape
    return pl.pallas_call(
        functools.partial(_act_kernel, act=act),
        grid=(width // tn, s // tm),
        in_specs=[pl.BlockSpec((tm, d), lambda j, i: (i, 0)),
                  pl.BlockSpec((d, tn), lambda j, i: (0, col_off // tn + j))],
        out_specs=pl.BlockSpec((tm, tn), lambda j, i: (i, j)),
        out_shape=jax.ShapeDtypeStruct((s, width), BF16),
        scratch_shapes=[pltpu.VMEM((d, tn), BF16)],
        compiler_params=_params(("arbitrary", "arbitrary"), 58),
        name=name,
    )(h, w)


def _conv_in_specs(tm, tile_of):
    hb = tm // CONV_HALO
    n_hblk = SEQ // CONV_HALO
    return [pl.BlockSpec((tm, CONV_WIDTH), lambda *g: (tile_of(*g), 0)),
            pl.BlockSpec((CONV_HALO, CONV_WIDTH), lambda *g: (jnp.maximum(tile_of(*g) * hb - 1, 0), 0)),
            pl.BlockSpec((CONV_HALO, CONV_WIDTH), lambda *g: (jnp.minimum((tile_of(*g) + 1) * hb, n_hblk - 1), 0)),
            pl.BlockSpec((CONV_KERNEL, CONV_WIDTH), lambda *g: (0, 0))]


def _fill_conv_window(u_ref, up_ref, un_ref, buf_ref, tile, n_tiles):
    tm = u_ref.shape[0]
    buf_ref[0:CONV_HALO, :] = up_ref[...] * (tile > 0).astype(F32)
    buf_ref[CONV_HALO:CONV_HALO + tm, :] = u_ref[...]
    buf_ref[CONV_HALO + tm:, :] = un_ref[...] * (tile < n_tiles - 1).astype(F32)


def _conv_chunk(buf_ref, sh_ref, wdw_ref, cv_ref, idx):
    lane_blocks = CONV_WIDTH // CONV_LANES
    if isinstance(idx, int):
        r0, c0 = (idx // lane_blocks) * CONV_ROWS, (idx % lane_blocks) * CONV_LANES
    else:
        r0 = pl.multiple_of((idx // lane_blocks) * CONV_ROWS, CONV_ROWS)
        c0 = pl.multiple_of((idx % lane_blocks) * CONV_LANES, CONV_LANES)
    base = CONV_HALO - CONV_KERNEL // 2
    span = CONV_ROWS + 2 * CONV_HALO - SUBLANES
    win = buf_ref[pl.ds(r0, CONV_ROWS + 2 * CONV_HALO), pl.ds(c0, CONV_LANES)]
    for res in range(1, SUBLANES):
        sh_ref[res - 1] = win[res:res + span, :]
    acc = jnp.zeros((CONV_ROWS, CONV_LANES), F32)
    for k in range(CONV_KERNEL):
        res = (base + k) % SUBLANES
        off = base + k - res
        if res == 0:
            tap = win[off:off + CONV_ROWS, :]
        else:
            tap = sh_ref[res - 1, off:off + CONV_ROWS, :]
        acc = acc + tap * wdw_ref[k:k + 1, pl.ds(c0, CONV_LANES)]
    cv_ref[pl.ds(r0, CONV_ROWS), pl.ds(c0, CONV_LANES)] = acc


def _conv_scratch(tm):
    return [pltpu.VMEM((tm + 2 * CONV_HALO, CONV_WIDTH), F32),
            pltpu.VMEM((SUBLANES - 1, CONV_ROWS + 2 * CONV_HALO - SUBLANES, CONV_LANES), F32)]


def _conv_only_kernel(u_ref, up_ref, un_ref, wdw_ref, cv_ref, buf_ref, sh_ref):
    _fill_conv_window(u_ref, up_ref, un_ref, buf_ref, pl.program_id(0), pl.num_programs(0))
    n_chunks = (u_ref.shape[0] // CONV_ROWS) * (CONV_WIDTH // CONV_LANES)

    def body(idx, carry):
        _conv_chunk(buf_ref, sh_ref, wdw_ref, cv_ref, idx)
        return carry
    lax.fori_loop(0, n_chunks, body, 0)


def _conv_only(u, w_dw, tm=512):
    s = u.shape[0]
    return pl.pallas_call(
        _conv_only_kernel,
        grid=(s // tm,),
        in_specs=_conv_in_specs(tm, lambda i: i),
        out_specs=pl.BlockSpec((tm, CONV_WIDTH), lambda i: (i, 0)),
        out_shape=jax.ShapeDtypeStruct((s, CONV_WIDTH), F32),
        scratch_shapes=_conv_scratch(tm),
        compiler_params=_params(("arbitrary",), 32),
        name="conv_only",
    )(u, u, u, w_dw)


def _conv_post_kernel(cv_ref, lng_ref, lnb_ref, wp_ref, gate_ref, o_ref, wpb_ref):
    _cast_once(pl.program_id(0), wp_ref, wpb_ref)
    cv = cv_ref[...]
    mu = jnp.mean(cv, axis=-1, keepdims=True)
    xc = cv - mu
    var = jnp.mean(xc * xc, axis=-1, keepdims=True)
    y = xc * lax.rsqrt(var + EPS) * lng_ref[...] + lnb_ref[...]
    y = y * _sigmoid(y)
    yc = jnp.dot(y.astype(BF16), wpb_ref[...], preferred_element_type=F32)
    o_ref[...] = gate_ref[...].astype(F32) * yc


def _conv_post(cv, ln_g, ln_b, w_proj, gates, tm=512):
    s = cv.shape[0]
    return pl.pallas_call(
        _conv_post_kernel,
        grid=(s // tm,),
        in_specs=[pl.BlockSpec((tm, CONV_WIDTH), lambda i: (i, 0)),
                  pl.BlockSpec((1, CONV_WIDTH), lambda i: (0, 0)),
                  pl.BlockSpec((1, CONV_WIDTH), lambda i: (0, 0)),
                  pl.BlockSpec((CONV_WIDTH, D_MODEL), lambda i: (0, 0), pipeline_mode=pl.Buffered(1)),
                  pl.BlockSpec((tm, D_MODEL), lambda i: (i, 0))],
        out_specs=pl.BlockSpec((tm, D_MODEL), lambda i: (i, 0)),
        out_shape=jax.ShapeDtypeStruct((s, D_MODEL), F32),
        scratch_shapes=[pltpu.VMEM((CONV_WIDTH, D_MODEL), BF16)],
        compiler_params=_params(("arbitrary",), 48),
        name="conv_post",
    )(cv, ln_g.reshape(1, -1), ln_b.reshape(1, -1), w_proj, gates)


def _stage_queries(qt_ref, qs_ref, tq):
    for hh in range(GROUP):
        qs_ref[:, hh * tq:(hh + 1) * tq] = qt_ref[hh * HEAD_DIM:(hh + 1) * HEAD_DIM, :]


def _store_attention_out(o_t, o_ref, tq):
    for hh in range(GROUP):
        o_ref[:, hh * HEAD_DIM:(hh + 1) * HEAD_DIM] = o_t[:, hh * tq:(hh + 1) * tq].T.astype(o_ref.dtype)


def _chunk_start(c, tk):
    return c * tk if isinstance(c, int) else pl.multiple_of(c * tk, tk)


def _attn_online_kernel(qt_ref, k_ref, vt_ref, o_ref, qs_ref, acc_ref, s0_ref, s1_ref, *, tq, tk):
    m_cols = GROUP * tq
    n_chunks = k_ref.shape[0] // tk
    assert n_chunks % 2 == 0 and n_chunks >= 2
    _stage_queries(qt_ref, qs_ref, tq)
    acc_ref[...] = jnp.zeros(acc_ref.shape, F32)

    def scores(c, dst_ref):
        s = jnp.dot(k_ref[pl.ds(_chunk_start(c, tk), tk), :], qs_ref[...], preferred_element_type=F32)
        dst_ref[...] = s
        return jnp.max(s, axis=0, keepdims=True)

    def update(c, src_ref, col_max, m_prev):
        m_new = jnp.maximum(m_prev, col_max)
        alpha = jnp.exp2(m_prev - m_new)
        p = jnp.exp2(src_ref[...] - m_new).astype(BF16)
        vtc = vt_ref[:, pl.ds(_chunk_start(c, tk), tk)]
        acc_ref[...] = alpha * acc_ref[...] + jnp.dot(vtc, p, preferred_element_type=F32)
        return m_new

    def pair(j, carry):
        m_run, cm0 = carry
        c = 2 * j
        cm1 = scores(c + 1, s1_ref)
        m_run = update(c, s0_ref, cm0, m_run)
        cm2 = scores(c + 2, s0_ref)
        m_run = update(c + 1, s1_ref, cm1, m_run)
        return m_run, cm2

    cm0 = scores(0, s0_ref)
    carry = (jnp.full((1, m_cols), -1e30, F32), cm0)
    m_run, cm0 = lax.fori_loop(0, n_chunks // 2 - 1, pair, carry)
    cm1 = scores(n_chunks - 1, s1_ref)
    m_run = update(n_chunks - 2, s0_ref, cm0, m_run)
    update(n_chunks - 1, s1_ref, cm1, m_run)
    _store_attention_out(acc_ref[0:HEAD_DIM, :] / acc_ref[HEAD_DIM:HEAD_DIM + 1, :], o_ref, tq)


def _attn_bounded_kernel(b_ref, qt_ref, k_ref, vt_ref, u_ref, up_ref, un_ref, wdw_ref, w2_ref,
                         o_ref, cv_ref, w2b_ref, qs_ref, acc_ref, p0_ref, p1_ref, buf_ref, sh_ref,
                         *, tq, tk):
    m_cols = GROUP * tq
    n_chunks = k_ref.shape[0] // tk
    assert n_chunks % 2 == 0 and n_chunks >= 2
    conv_chunks = (u_ref.shape[0] // CONV_ROWS) * (CONV_WIDTH // CONV_LANES)
    assert conv_chunks % n_chunks == 0
    per = conv_chunks // n_chunks
    tile = pl.program_id(0) * pl.num_programs(1) + pl.program_id(1)
    n_tiles = pl.num_programs(0) * pl.num_programs(1)
    w2b_ref[...] = w2_ref[...].astype(w2b_ref.dtype)
    _fill_conv_window(u_ref, up_ref, un_ref, buf_ref, tile, n_tiles)
    _stage_queries(qt_ref, qs_ref, tq)
    acc_ref[...] = jnp.zeros(acc_ref.shape, F32)
    bound = b_ref[0, 0]

    def probs(c, dst_ref, l_run):
        s = jnp.dot(k_ref[pl.ds(_chunk_start(c, tk), tk), :], qs_ref[...], preferred_element_type=F32)
        p = jnp.exp2(s - bound)
        dst_ref[...] = p.astype(BF16)
        return l_run + jnp.sum(p, axis=0, keepdims=True)

    def accumulate(c, src_ref):
        vtc = vt_ref[0:HEAD_DIM, pl.ds(_chunk_start(c, tk), tk)]
        acc_ref[...] += jnp.dot(vtc, src_ref[...], preferred_element_type=F32)

    def conv(c):
        for e in range(per):
            _conv_chunk(buf_ref, sh_ref, wdw_ref, cv_ref, c * per + e)

    def pair(j, l_run):
        c = 2 * j
        l_run = probs(c + 1, p1_ref, l_run)
        accumulate(c, p0_ref)
        conv(c)
        l_run = probs(c + 2, p0_ref, l_run)
        accumulate(c + 1, p1_ref)
        conv(c + 1)
        return l_run

    l_run = probs(0, p0_ref, jnp.zeros((1, m_cols), F32))
    l_run = lax.fori_loop(0, n_chunks // 2 - 1, pair, l_run)
    l_run = probs(n_chunks - 1, p1_ref, l_run)
    accumulate(n_chunks - 2, p0_ref)
    conv(n_chunks - 2)
    accumulate(n_chunks - 1, p1_ref)
    conv(n_chunks - 1)
    _store_attention_out(acc_ref[...] / l_run, o_ref, tq)


def _attention_and_side_work(qt, k, vt, q_gain, k_gain, u, w_dw, w_ff2,
                             tq_online=256, tq_bounded=1024, tk=512):
    s = k.shape[0]
    gw = GROUP * HEAD_DIM
    attn_specs = lambda tq: [pl.BlockSpec((gw, tq), lambda g, i: (g, i)),
                             pl.BlockSpec((s, HEAD_DIM), lambda g, i: (0, g)),
                             pl.BlockSpec((V_ROWS, s), lambda g, i: (g, 0))]
    out_spec = lambda tq: pl.BlockSpec((tq, gw), lambda g, i: (i, g))
    o_shape = jax.ShapeDtypeStruct((s, Q_W), BF16)

    def online(_):
        tq = tq_online
        m_cols = GROUP * tq
        o = pl.pallas_call(
            functools.partial(_attn_online_kernel, tq=tq, tk=tk),
            grid=(N_KV_HEADS, s // tq),
            in_specs=attn_specs(tq),
            out_specs=out_spec(tq),
            out_shape=o_shape,
            scratch_shapes=[pltpu.VMEM((HEAD_DIM, m_cols), BF16),
                            pltpu.VMEM((V_ROWS, m_cols), F32),
                            pltpu.VMEM((tk, m_cols), F32),
                            pltpu.VMEM((tk, m_cols), F32)],
            compiler_params=_params(("arbitrary", "arbitrary"), 48),
            name="gqa_online")(qt, k, vt)
        return o, _conv_only(u, w_dw), w_ff2.astype(BF16)

    def bounded(b):
        tq = tq_bounded
        m_cols = GROUP * tq
        n_i = s // tq
        n_tiles = N_KV_HEADS * n_i
        tm_c = s // n_tiles
        w2_rows = w_ff2.shape[0] // n_tiles
        tile_of = lambda g, i: g * n_i + i
        return pl.pallas_call(
            functools.partial(_attn_bounded_kernel, tq=tq, tk=tk),
            grid=(N_KV_HEADS, n_i),
            in_specs=([pl.BlockSpec(memory_space=pltpu.SMEM)] + attn_specs(tq)
                      + _conv_in_specs(tm_c, tile_of)
                      + [pl.BlockSpec((w2_rows, w_ff2.shape[1]), lambda g, i: (tile_of(g, i), 0))]),
            out_specs=[out_spec(tq),
                       pl.BlockSpec((tm_c, CONV_WIDTH), lambda g, i: (tile_of(g, i), 0)),
                       pl.BlockSpec((w2_rows, w_ff2.shape[1]), lambda g, i: (tile_of(g, i), 0))],
            out_shape=[o_shape,
                       jax.ShapeDtypeStruct((s, CONV_WIDTH), F32),
                       jax.ShapeDtypeStruct(w_ff2.shape, BF16)],
            scratch_shapes=[pltpu.VMEM((HEAD_DIM, m_cols), BF16),
                            pltpu.VMEM((HEAD_DIM, m_cols), F32),
                            pltpu.VMEM((tk, m_cols), BF16),
                            pltpu.VMEM((tk, m_cols), BF16)] + _conv_scratch(tm_c),
            compiler_params=_params(("arbitrary", "arbitrary"), 56),
            name="gqa_bounded")(b, qt, k, vt, u, u, u, w_dw, w_ff2)

    bound = (HEAD_DIM * Q_SCALE * SCORE_BOUND_SLACK
             * jnp.max(jnp.abs(q_gain)) * jnp.max(jnp.abs(k_gain))).astype(F32)
    return lax.cond(bound <= MAX_FIXED_SHIFT, bounded, online, bound.reshape(1, 1))


def _merge_kernel(o_ref, w_ref, mc_ref, ga_ref, out_ref, wb_ref):
    _cast_once(pl.program_id(1), w_ref, wb_ref)
    ya = jnp.dot(o_ref[...], wb_ref[...], preferred_element_type=F32)
    out_ref[...] = (mc_ref[...] + ga_ref[...].astype(F32) * ya).astype(out_ref.dtype)


def _merge(o, w_ap, m_c, gates, tm=1024, tn=1024):
    s, d = o.shape
    ga_off = D_MODEL // tn
    return pl.pallas_call(
        _merge_kernel,
        grid=(D_MODEL // tn, s // tm),
        in_specs=[pl.BlockSpec((tm, d), lambda j, i: (i, 0)),
                  pl.BlockSpec((d, tn), lambda j, i: (0, j)),
                  pl.BlockSpec((tm, tn), lambda j, i: (i, j)),
                  pl.BlockSpec((tm, tn), lambda j, i: (i, ga_off + j))],
        out_specs=pl.BlockSpec((tm, tn), lambda j, i: (i, j)),
        out_shape=jax.ShapeDtypeStruct((s, D_MODEL), BF16),
        scratch_shapes=[pltpu.VMEM((d, tn), BF16)],
        compiler_params=_params(("arbitrary", "arbitrary"), 56),
        name="merge_attn_proj",
    )(o, w_ap, m_c, gates)


def _out_kernel(a_ref, w_ref, x_ref, g_ref, x1_ref, h_ref, wb_ref):
    _cast_once(pl.program_id(0), w_ref, wb_ref)
    x1 = x_ref[...] + jnp.dot(a_ref[...], wb_ref[...], preferred_element_type=F32)
    x1_ref[...] = x1
    h_ref[...] = _rms_rows(x1, g_ref[...]).astype(h_ref.dtype)


def _out_proj(a, w, x, g, tm=256):
    s, d = x.shape
    return pl.pallas_call(
        _out_kernel,
        grid=(s // tm,),
        in_specs=[pl.BlockSpec((tm, d), lambda i: (i, 0)),
                  pl.BlockSpec((d, d), lambda i: (0, 0), pipeline_mode=pl.Buffered(1)),
                  pl.BlockSpec((tm, d), lambda i: (i, 0)),
                  pl.BlockSpec((1, d), lambda i: (0, 0))],
        out_specs=[pl.BlockSpec((tm, d), lambda i: (i, 0)),
                   pl.BlockSpec((tm, d), lambda i: (i, 0))],
        out_shape=[jax.ShapeDtypeStruct((s, d), F32),
                   jax.ShapeDtypeStruct((s, d), BF16)],
        scratch_shapes=[pltpu.VMEM((d, d), BF16)],
        compiler_params=_params(("arbitrary",), 56),
        name="out_proj_residual",
    )(a, w, x, g.reshape(1, d))


def _ffn2_kernel(a_ref, w_ref, x_ref, o_ref, acc_ref):
    kk = pl.program_id(2)

    @pl.when(kk == 0)
    def _():
        acc_ref[...] = x_ref[...]

    acc_ref[...] += jnp.dot(a_ref[...], w_ref[...], preferred_element_type=F32)

    @pl.when(kk == pl.num_programs(2) - 1)
    def _():
        o_ref[...] = acc_ref[...]


def _ffn2(a, w, x, tm=1024, tn=1024, tk=2048):
    s, kdim = a.shape
    d = x.shape[1]
    return pl.pallas_call(
        _ffn2_kernel,
        grid=(s // tm, d // tn, kdim // tk),
        in_specs=[pl.BlockSpec((tm, tk), lambda i, j, k: (i, k)),
                  pl.BlockSpec((tk, tn), lambda i, j, k: (k, j)),
                  pl.BlockSpec((tm, tn), lambda i, j, k: (i, j))],
        out_specs=pl.BlockSpec((tm, tn), lambda i, j, k: (i, j)),
        out_shape=jax.ShapeDtypeStruct((s, d), F32),
        scratch_shapes=[pltpu.VMEM((tm, tn), F32)],
        compiler_params=_params(("parallel", "parallel", "arbitrary"), 48),
        name="ffn_down_residual",
    )(a, w, x)


def _ple_kernel(x_ref, p_ref, gp_ref, wg_ref, wp_ref, gf_ref, o_ref, wgb_ref, wpb_ref):
    _cast_once(pl.program_id(0), wg_ref, wgb_ref)
    _cast_once(pl.program_id(0), wp_ref, wpb_ref)
    x = x_ref[...]
    h = _rms_rows(x, gp_ref[...]).astype(BF16)
    gate = _sigmoid(jnp.dot(h, wgb_ref[...], preferred_element_type=F32))
    pp = jnp.dot(p_ref[...].astype(BF16), wpb_ref[...], preferred_element_type=F32)
    x3 = x + gate * pp
    o_ref[...] = _rms_rows(x3, gf_ref[...])


def _ple_final(x, p, g_ple, w_gate, w_proj, g_final, tm=256):
    s, d = x.shape
    return pl.pallas_call(
        _ple_kernel,
        grid=(s // tm,),
        in_specs=[pl.BlockSpec((tm, d), lambda i: (i, 0)),
                  pl.BlockSpec((tm, PLE_DIM), lambda i: (i, 0)),
                  pl.BlockSpec((1, d), lambda i: (0, 0)),
                  pl.BlockSpec((d, d), lambda i: (0, 0), pipeline_mode=pl.Buffered(1)),
                  pl.BlockSpec((PLE_DIM, d), lambda i: (0, 0), pipeline_mode=pl.Buffered(1)),
                  pl.BlockSpec((1, d), lambda i: (0, 0))],
        out_specs=pl.BlockSpec((tm, d), lambda i: (i, 0)),
        out_shape=jax.ShapeDtypeStruct((s, d), F32),
        scratch_shapes=[pltpu.VMEM((d, d), BF16), pltpu.VMEM((PLE_DIM, d), BF16)],
        compiler_params=_params(("arbitrary",), 56),
        name="ple_final_norm",
    )(x, p, g_ple.reshape(1, d), w_gate, w_proj, g_final.reshape(1, d))


def _rope_tables():
    n_rows = SEQ // GRID_W
    inv_freq = ROPE_THETA ** (-jnp.arange(0, AXIS_DIM, 2, dtype=F32) / AXIS_DIM)
    ang_row = inv_freq[:, None] * jnp.arange(n_rows, dtype=jnp.int32).astype(F32)[None, :]
    ang_col = inv_freq[:, None] * jnp.arange(GRID_W, dtype=jnp.int32).astype(F32)[None, :]
    nf = inv_freq.shape[0]

    def over_t(row_tab, col_tab):
        r = jnp.broadcast_to(row_tab[:, :, None], (nf, n_rows, GRID_W)).reshape(nf, SEQ)
        c = jnp.broadcast_to(col_tab[:, None, :], (nf, n_rows, GRID_W)).reshape(nf, SEQ)
        return r, c

    cr, cc = over_t(jnp.cos(ang_row), jnp.cos(ang_col))
    sr, sc = over_t(jnp.sin(ang_row), jnp.sin(ang_col))
    cos_t = jnp.concatenate([cr, cr, cc, cc], axis=0)
    sin_t = jnp.concatenate([-sr, sr, -sc, sc], axis=0)
    return cos_t, sin_t


def kernel(x, p, norm_mix, w_in, w_dw, conv_ln_g, conv_ln_b, w_conv_proj, q_norm, k_norm,
           w_attn_proj, w_out, norm_ffn, w_ff1, w_ff2, norm_ple, w_ple_gate, w_ple_proj, norm_final):
    depth = w_in.shape[0]
    assert depth == 1, "the final norm is fused into the last layer's kernel"
    tabs_t = _rope_tables()
    xs = x[0]
    for li in range(depth):
        w_i = w_in[li]
        h, h_t = _rmsnorm_cast(xs, norm_mix[li])
        u = _glu_proj(h, w_i)
        qt = _qk_proj(w_i, h_t, q_norm[li], tabs_t, OFF_Q, Q_W, Q_SCALE, False, "qt_proj")
        k = _qk_proj(w_i, h_t, k_norm[li], tabs_t, OFF_K, KV_W, 1.0, True, "k_proj")
        vt = _vt_proj(w_i, h_t)
        gates = _act_proj(h, w_i, OFF_G, 2 * D_MODEL, "sigmoid", "gate_proj")
        o, cv, w_ff2_b = _attention_and_side_work(qt, k, vt, q_norm[li], k_norm[li], u, w_dw[li], w_ff2[li])
        m_c = _conv_post(cv, conv_ln_g[li], conv_ln_b[li], w_conv_proj[li], gates)
        merged = _merge(o, w_attn_proj[li], m_c, gates)
        x1, h2 = _out_proj(merged, w_out[li], xs, norm_ffn[li])
        a = _act_proj(h2, w_ff1[li], 0, D_FF, "relu2", "ffn_up")
        x2 = _ffn2(a, w_ff2_b, x1)
        xs = _ple_final(x2, p[li, 0], norm_ple[li], w_ple_gate[li], w_ple_proj[li], norm_final)
    return xs[None]
```

```python
import functools
import math

import jax
import jax.numpy as jnp
from jax import lax
from jax.experimental import pallas as pl
from jax.experimental.pallas import tpu as pltpu

D_MODEL = 2048
SEQ = 8192
N_HEADS = 16
N_KV_HEADS = 4
HEAD_DIM = 128
GROUP = N_HEADS // N_KV_HEADS
ROPE_THETA = 10000.0
AXIS_DIM = HEAD_DIM // 2
GRID_W = 64
CONV_WIDTH = D_MODEL // 2
CONV_KERNEL = 31
CONV_HALO = 16
CONV_ROWS = 64
CONV_LANES = 256
D_FF = 4 * D_MODEL
PLE_DIM = 256
EPS = 1e-6
Q_W = N_HEADS * HEAD_DIM
KV_W = N_KV_HEADS * HEAD_DIM
LANES = 128
SUBLANES = 8
BF16_SUBLANES = 16
V_ROWS = HEAD_DIM + BF16_SUBLANES
K_COLS = 2 * HEAD_DIM

OFF_CA = 0
OFF_CB = CONV_WIDTH
OFF_Q = 2 * CONV_WIDTH
OFF_K = OFF_Q + Q_W
OFF_V = OFF_K + KV_W
OFF_G = OFF_V + KV_W

Q_SCALE = (HEAD_DIM ** -0.5) * math.log2(math.e)
SCORE_BOUND_SLACK = 1.02
MAX_FIXED_SHIFT = 60.0

BF16 = jnp.bfloat16
F32 = jnp.float32
MIB = 1024 * 1024


def _params(sem, vmem_mib):
    return pltpu.CompilerParams(dimension_semantics=sem, vmem_limit_bytes=vmem_mib * MIB)


def _sigmoid(v):
    return 1.0 / (1.0 + jnp.exp(-v))


def _rms_rows(v, g):
    ms = jnp.mean(v * v, axis=-1, keepdims=True)
    return v * lax.rsqrt(ms + EPS) * g


def _cast_once(step, w_ref, wb_ref, transpose=False):
    @pl.when(step == 0)
    def _():
        w = w_ref[...]
        wb_ref[...] = (w.T if transpose else w).astype(wb_ref.dtype)


def _rmsnorm_kernel(x_ref, g_ref, o_ref, ot_ref):
    y = _rms_rows(x_ref[...], g_ref[...])
    o_ref[...] = y.astype(o_ref.dtype)
    for cb in range(0, y.shape[1], LANES):
        ot_ref[cb:cb + LANES, :] = y[:, cb:cb + LANES].T.astype(ot_ref.dtype)


def _rmsnorm_cast(x, g, tm=512):
    s, d = x.shape
    return pl.pallas_call(
        _rmsnorm_kernel,
        grid=(s // tm,),
        in_specs=[pl.BlockSpec((tm, d), lambda i: (i, 0)),
                  pl.BlockSpec((1, d), lambda i: (0, 0))],
        out_specs=[pl.BlockSpec((tm, d), lambda i: (i, 0)),
                   pl.BlockSpec((d, tm), lambda i: (0, i))],
        out_shape=[jax.ShapeDtypeStruct((s, d), BF16),
                   jax.ShapeDtypeStruct((d, s), BF16)],
        compiler_params=_params(("arbitrary",), 32),
        name="rmsnorm_cast",
    )(x, g.reshape(1, d))


def _glu_kernel(h_ref, wa_ref, wb_ref, o_ref, wab_ref, wbb_ref):
    _cast_once(pl.program_id(1), wa_ref, wab_ref)
    _cast_once(pl.program_id(1), wb_ref, wbb_ref)
    h = h_ref[...]
    a = jnp.dot(h, wab_ref[...], preferred_element_type=F32)
    b = jnp.dot(h, wbb_ref[...], preferred_element_type=F32)
    o_ref[...] = a * _sigmoid(b)


def _glu_proj(h, w, tm=1024, tn=512):
    s, d = h.shape
    nb = CONV_WIDTH // tn
    return pl.pallas_call(
        _glu_kernel,
        grid=(nb, s // tm),
        in_specs=[pl.BlockSpec((tm, d), lambda j, i: (i, 0)),
                  pl.BlockSpec((d, tn), lambda j, i: (0, OFF_CA // tn + j)),
                  pl.BlockSpec((d, tn), lambda j, i: (0, OFF_CB // tn + j))],
        out_specs=pl.BlockSpec((tm, tn), lambda j, i: (i, j)),
        out_shape=jax.ShapeDtypeStruct((s, CONV_WIDTH), F32),
        scratch_shapes=[pltpu.VMEM((d, tn), BF16), pltpu.VMEM((d, tn), BF16)],
        compiler_params=_params(("arbitrary", "arbitrary"), 44),
        name="glu_proj",
    )(h, w, w)


def _swap_axis_halves(y):
    q = AXIS_DIM // 2
    return jnp.concatenate([y[q:2 * q], y[0:q], y[3 * q:4 * q], y[2 * q:3 * q]], axis=0)


def _qk_kernel(w_ref, ht_ref, g_ref, cos_ref, sin_ref, o_ref, wt_ref, *, scale, row_major_out):
    _cast_once(pl.program_id(1), w_ref, wt_ref, transpose=True)
    zt = jnp.dot(wt_ref[...], ht_ref[...], preferred_element_type=F32)
    g = jnp.broadcast_to(g_ref[...], (HEAD_DIM, zt.shape[1]))
    cos = cos_ref[...]
    sin = sin_ref[...]
    for hh in range(zt.shape[0] // HEAD_DIM):
        zh = zt[hh * HEAD_DIM:(hh + 1) * HEAD_DIM, :]
        ms = jnp.mean(zh * zh, axis=0, keepdims=True)
        y = zh * lax.rsqrt(ms + EPS) * g
        r = y * cos + _swap_axis_halves(y) * sin
        if scale != 1.0:
            r = r * scale
        if row_major_out:
            o_ref[:, hh * K_COLS:hh * K_COLS + HEAD_DIM] = r.T.astype(o_ref.dtype)
            lane = lax.broadcasted_iota(jnp.int32, (zt.shape[1], K_COLS - HEAD_DIM), 1)
            o_ref[:, hh * K_COLS + HEAD_DIM:(hh + 1) * K_COLS] = (
                jnp.where(lane == 0, 1.0, 0.0).astype(o_ref.dtype))
        else:
            o_ref[hh * HEAD_DIM:(hh + 1) * HEAD_DIM, :] = r.astype(o_ref.dtype)


def _qk_proj(w, h_t, gain, tabs_t, col_off, width, scale, row_major_out, name, tm=1024):
    d, s = h_t.shape
    tn = min(width, 1024)
    cos_t, sin_t = tabs_t
    tab_spec = pl.BlockSpec((HEAD_DIM, tm), lambda j, i: (0, i))
    if row_major_out:
        heads_per_tile = tn // HEAD_DIM
        out_spec = pl.BlockSpec((tm, heads_per_tile * K_COLS), lambda j, i: (i, j))
        out_shape = jax.ShapeDtypeStruct((s, width // HEAD_DIM * K_COLS), BF16)
    else:
        out_spec = pl.BlockSpec((tn, tm), lambda j, i: (j, i))
        out_shape = jax.ShapeDtypeStruct((width, s), BF16)
    return pl.pallas_call(
        functools.partial(_qk_kernel, scale=scale, row_major_out=row_major_out),
        grid=(width // tn, s // tm),
        in_specs=[pl.BlockSpec((d, tn), lambda j, i: (0, col_off // tn + j)),
                  pl.BlockSpec((d, tm), lambda j, i: (0, i)),
                  pl.BlockSpec((HEAD_DIM, 1), lambda j, i: (0, 0)),
                  tab_spec, tab_spec],
        out_specs=out_spec,
        out_shape=out_shape,
        scratch_shapes=[pltpu.VMEM((tn, d), BF16)],
        compiler_params=_params(("arbitrary", "arbitrary"), 48),
        name=name,
    )(w, h_t, gain.reshape(HEAD_DIM, 1), cos_t, sin_t)


def _vt_kernel(w_ref, ht_ref, o_ref, wt_ref):
    _cast_once(pl.program_id(0), w_ref, wt_ref, transpose=True)
    zt = jnp.dot(wt_ref[...], ht_ref[...], preferred_element_type=F32)
    pad_rows = V_ROWS - HEAD_DIM
    row = lax.broadcasted_iota(jnp.int32, (pad_rows, zt.shape[1]), 0)
    ones_then_zeros = jnp.where(row == 0, 1.0, 0.0).astype(o_ref.dtype)
    for hh in range(N_KV_HEADS):
        o_ref[hh * V_ROWS:hh * V_ROWS + HEAD_DIM, :] = (
            zt[hh * HEAD_DIM:(hh + 1) * HEAD_DIM, :].astype(o_ref.dtype))
        o_ref[hh * V_ROWS + HEAD_DIM:(hh + 1) * V_ROWS, :] = ones_then_zeros


def _vt_proj(w, h_t, tm=1024):
    d, s = h_t.shape
    return pl.pallas_call(
        _vt_kernel,
        grid=(s // tm,),
        in_specs=[pl.BlockSpec((d, KV_W), lambda i: (0, OFF_V // KV_W)),
                  pl.BlockSpec((d, tm), lambda i: (0, i))],
        out_specs=pl.BlockSpec((N_KV_HEADS * V_ROWS, tm), lambda i: (0, i)),
        out_shape=jax.ShapeDtypeStruct((N_KV_HEADS * V_ROWS, s), BF16),
        scratch_shapes=[pltpu.VMEM((KV_W, d), BF16)],
        compiler_params=_params(("arbitrary",), 40),
        name="vt_proj",
    )(w, h_t)


def _act_kernel(h_ref, w_ref, o_ref, wb_ref, *, act):
    _cast_once(pl.program_id(1), w_ref, wb_ref)
    z = jnp.dot(h_ref[...], wb_ref[...], preferred_element_type=F32)
    if act == "sigmoid":
        z = _sigmoid(z)
    elif act == "relu2":
        z = jnp.square(jnp.maximum(z, 0.0))
    o_ref[...] = z.astype(o_ref.dtype)


def _act_proj(h, w, col_off, width, act, name, tm=2048, tn=1024):
    s, d = h.shape
    return pl.pallas_call(
        functools.partial(_act_kernel, act=act),
        grid=(width // tn, s // tm),
        in_specs=[pl.BlockSpec((tm, d), lambda j, i: (i, 0)),
                  pl.BlockSpec((d, tn), lambda j, i: (0, col_off // tn + j))],
        out_specs=pl.BlockSpec((tm, tn), lambda j, i: (i, j)),
        out_shape=jax.ShapeDtypeStruct((s, width), BF16),
        scratch_shapes=[pltpu.VMEM((d, tn), BF16)],
        compiler_params=_params(("arbitrary", "arbitrary"), 58),
        name=name,
    )(h, w)


def _conv_chunk(buf_ref, sh_ref, wdw_ref, cv_ref, r0, c0):
    base = CONV_HALO - CONV_KERNEL // 2
    span = CONV_ROWS + 2 * CONV_HALO - SUBLANES
    win = buf_ref[pl.ds(r0, CONV_ROWS + 2 * CONV_HALO), pl.ds(c0, CONV_LANES)]
    for res in range(1, SUBLANES):
        sh_ref[res - 1] = win[res:res + span, :]
    acc = jnp.zeros((CONV_ROWS, CONV_LANES), F32)
    for k in range(CONV_KERNEL):
        res = (base + k) % SUBLANES
        off = base + k - res
        if res == 0:
            tap = win[off:off + CONV_ROWS, :]
        else:
            tap = sh_ref[res - 1, off:off + CONV_ROWS, :]
        acc = acc + tap * wdw_ref[k:k + 1, pl.ds(c0, CONV_LANES)]
    cv_ref[pl.ds(r0, CONV_ROWS), pl.ds(c0, CONV_LANES)] = acc


def _conv_kernel(u_ref, up_ref, un_ref, wdw_ref, lng_ref, lnb_ref, wp_ref, gate_ref,
                 o_ref, buf_ref, cv_ref, sh_ref, wpb_ref):
    i = pl.program_id(0)
    tm = u_ref.shape[0]
    _cast_once(i, wp_ref, wpb_ref)
    buf_ref[0:CONV_HALO, :] = up_ref[...] * (i > 0).astype(F32)
    buf_ref[CONV_HALO:CONV_HALO + tm, :] = u_ref[...]
    buf_ref[CONV_HALO + tm:, :] = un_ref[...] * (i < pl.num_programs(0) - 1).astype(F32)

    for c0 in range(0, CONV_WIDTH, CONV_LANES):
        def body(r, carry, c0=c0):
            _conv_chunk(buf_ref, sh_ref, wdw_ref, cv_ref, pl.multiple_of(r * CONV_ROWS, CONV_ROWS), c0)
            return carry
        lax.fori_loop(0, tm // CONV_ROWS, body, 0)

    cv = cv_ref[...]
    mu = jnp.mean(cv, axis=-1, keepdims=True)
    xc = cv - mu
    var = jnp.mean(xc * xc, axis=-1, keepdims=True)
    y = xc * lax.rsqrt(var + EPS) * lng_ref[...] + lnb_ref[...]
    y = y * _sigmoid(y)
    yc = jnp.dot(y.astype(BF16), wpb_ref[...], preferred_element_type=F32)
    o_ref[...] = gate_ref[...].astype(F32) * yc


def _conv_branch(u, w_dw, ln_g, ln_b, w_proj, gates, tm=512):
    s = u.shape[0]
    hb = tm // CONV_HALO
    n_hblk = s // CONV_HALO
    return pl.pallas_call(
        _conv_kernel,
        grid=(s // tm,),
        in_specs=[pl.BlockSpec((tm, CONV_WIDTH), lambda i: (i, 0)),
                  pl.BlockSpec((CONV_HALO, CONV_WIDTH), lambda i: (jnp.maximum(i * hb - 1, 0), 0)),
                  pl.BlockSpec((CONV_HALO, CONV_WIDTH), lambda i: (jnp.minimum((i + 1) * hb, n_hblk - 1), 0)),
                  pl.BlockSpec((CONV_KERNEL, CONV_WIDTH), lambda i: (0, 0)),
                  pl.BlockSpec((1, CONV_WIDTH), lambda i: (0, 0)),
                  pl.BlockSpec((1, CONV_WIDTH), lambda i: (0, 0)),
                  pl.BlockSpec((CONV_WIDTH, D_MODEL), lambda i: (0, 0), pipeline_mode=pl.Buffered(1)),
                  pl.BlockSpec((tm, D_MODEL), lambda i: (i, 0))],
        out_specs=pl.BlockSpec((tm, D_MODEL), lambda i: (i, 0)),
        out_shape=jax.ShapeDtypeStruct((s, D_MODEL), F32),
        scratch_shapes=[pltpu.VMEM((tm + 2 * CONV_HALO, CONV_WIDTH), F32),
                        pltpu.VMEM((tm, CONV_WIDTH), F32),
                        pltpu.VMEM((SUBLANES - 1, CONV_ROWS + 2 * CONV_HALO - SUBLANES, CONV_LANES), F32),
                        pltpu.VMEM((CONV_WIDTH, D_MODEL), BF16)],
        compiler_params=_params(("arbitrary",), 48),
        name="conv_branch",
    )(u, u, u, w_dw, ln_g.reshape(1, -1), ln_b.reshape(1, -1), w_proj, gates)


def _stage_queries(qt_ref, qs_ref, tq, shift):
    for hh in range(GROUP):
        qs_ref[0:HEAD_DIM, hh * tq:(hh + 1) * tq] = qt_ref[hh * HEAD_DIM:(hh + 1) * HEAD_DIM, :]
    row = lax.broadcasted_iota(jnp.int32, (K_COLS - HEAD_DIM, qs_ref.shape[1]), 0)
    qs_ref[HEAD_DIM:K_COLS, :] = jnp.where(row == 0, -shift, 0.0).astype(qs_ref.dtype)


def _store_attention_out(o_t, o_ref, tq):
    for hh in range(GROUP):
        o_ref[:, hh * HEAD_DIM:(hh + 1) * HEAD_DIM] = o_t[:, hh * tq:(hh + 1) * tq].T.astype(o_ref.dtype)


def _chunk_start(c, tk):
    return c * tk if isinstance(c, int) else pl.multiple_of(c * tk, tk)


def _attn_online_kernel(b_ref, qt_ref, k_ref, vt_ref, o_ref, qs_ref, acc_ref, s0_ref, s1_ref, *, tq, tk):
    m_cols = GROUP * tq
    n_chunks = k_ref.shape[0] // tk
    assert n_chunks % 2 == 0 and n_chunks >= 2
    _stage_queries(qt_ref, qs_ref, tq, b_ref[0, 0])
    acc_ref[...] = jnp.zeros(acc_ref.shape, F32)

    def scores(c, dst_ref):
        s = jnp.dot(k_ref[pl.ds(_chunk_start(c, tk), tk), :], qs_ref[...], preferred_element_type=F32)
        dst_ref[...] = s
        return jnp.max(s, axis=0, keepdims=True)

    def update(c, src_ref, col_max, m_prev):
        m_new = jnp.maximum(m_prev, col_max)
        alpha = jnp.exp2(m_prev - m_new)
        p = jnp.exp2(src_ref[...] - m_new).astype(BF16)
        vtc = vt_ref[:, pl.ds(_chunk_start(c, tk), tk)]
        acc_ref[...] = alpha * acc_ref[...] + jnp.dot(vtc, p, preferred_element_type=F32)
        return m_new

    def pair(j, carry):
        m_run, cm0 = carry
        c = 2 * j
        cm1 = scores(c + 1, s1_ref)
        m_run = update(c, s0_ref, cm0, m_run)
        cm2 = scores(c + 2, s0_ref)
        m_run = update(c + 1, s1_ref, cm1, m_run)
        return m_run, cm2

    cm0 = scores(0, s0_ref)
    carry = (jnp.full((1, m_cols), -1e30, F32), cm0)
    m_run, cm0 = lax.fori_loop(0, n_chunks // 2 - 1, pair, carry)
    cm1 = scores(n_chunks - 1, s1_ref)
    m_run = update(n_chunks - 2, s0_ref, cm0, m_run)
    update(n_chunks - 1, s1_ref, cm1, m_run)
    _store_attention_out(acc_ref[0:HEAD_DIM, :] / acc_ref[HEAD_DIM:HEAD_DIM + 1, :], o_ref, tq)


def _attn_bounded_kernel(b_ref, qt_ref, k_ref, vt_ref, w2_ref, o_ref, w2b_ref,
                         qs_ref, acc_ref, p0_ref, p1_ref, *, tq, tk):
    m_cols = GROUP * tq
    n_chunks = k_ref.shape[0] // tk
    assert n_chunks % 2 == 0 and n_chunks >= 2
    w2b_ref[...] = w2_ref[...].astype(w2b_ref.dtype)
    _stage_queries(qt_ref, qs_ref, tq, b_ref[0, 0])
    acc_ref[...] = jnp.zeros(acc_ref.shape, F32)

    def probs(c, dst_ref, l_run):
        s = jnp.dot(k_ref[pl.ds(_chunk_start(c, tk), tk), :], qs_ref[...], preferred_element_type=F32)
        p = jnp.exp2(s)
        dst_ref[...] = p.astype(BF16)
        return l_run + jnp.sum(p, axis=0, keepdims=True)

    def accumulate(c, src_ref):
        vtc = vt_ref[0:HEAD_DIM, pl.ds(_chunk_start(c, tk), tk)]
        acc_ref[...] += jnp.dot(vtc, src_ref[...], preferred_element_type=F32)

    def pair(j, l_run):
        c = 2 * j
        l_run = probs(c + 1, p1_ref, l_run)
        accumulate(c, p0_ref)
        l_run = probs(c + 2, p0_ref, l_run)
        accumulate(c + 1, p1_ref)
        return l_run

    l_run = probs(0, p0_ref, jnp.zeros((1, m_cols), F32))
    l_run = lax.fori_loop(0, n_chunks // 2 - 1, pair, l_run)
    l_run = probs(n_chunks - 1, p1_ref, l_run)
    accumulate(n_chunks - 2, p0_ref)
    accumulate(n_chunks - 1, p1_ref)
    _store_attention_out(acc_ref[...] / l_run, o_ref, tq)


def _attention(qt, k, vt, q_gain, k_gain, w_ff2, tq_online=256, tq_bounded=1024, tk=512):
    s = k.shape[0]
    gw = GROUP * HEAD_DIM
    smem = pl.BlockSpec(memory_space=pltpu.SMEM)
    attn_specs = lambda tq: [smem,
                             pl.BlockSpec((gw, tq), lambda g, i: (g, i)),
                             pl.BlockSpec((s, K_COLS), lambda g, i: (0, g)),
                             pl.BlockSpec((V_ROWS, s), lambda g, i: (g, 0))]
    out_spec = lambda tq: pl.BlockSpec((tq, gw), lambda g, i: (i, g))
    o_shape = jax.ShapeDtypeStruct((s, Q_W), BF16)

    def online(b):
        tq = tq_online
        m_cols = GROUP * tq
        o = pl.pallas_call(
            functools.partial(_attn_online_kernel, tq=tq, tk=tk),
            grid=(N_KV_HEADS, s // tq),
            in_specs=attn_specs(tq),
            out_specs=out_spec(tq),
            out_shape=o_shape,
            scratch_shapes=[pltpu.VMEM((K_COLS, m_cols), BF16),
                            pltpu.VMEM((V_ROWS, m_cols), F32),
                            pltpu.VMEM((tk, m_cols), F32),
                            pltpu.VMEM((tk, m_cols), F32)],
            compiler_params=_params(("arbitrary", "arbitrary"), 48),
            name="gqa_online")(b, qt, k, vt)
        return o, w_ff2.astype(BF16)

    def bounded(b):
        tq = tq_bounded
        m_cols = GROUP * tq
        n_i = s // tq
        w2_rows = w_ff2.shape[0] // (N_KV_HEADS * n_i)
        w2_spec = pl.BlockSpec((w2_rows, w_ff2.shape[1]), lambda g, i: (g * n_i + i, 0))
        return pl.pallas_call(
            functools.partial(_attn_bounded_kernel, tq=tq, tk=tk),
            grid=(N_KV_HEADS, n_i),
            in_specs=attn_specs(tq) + [w2_spec],
            out_specs=[out_spec(tq), w2_spec],
            out_shape=[o_shape, jax.ShapeDtypeStruct(w_ff2.shape, BF16)],
            scratch_shapes=[pltpu.VMEM((K_COLS, m_cols), BF16),
                            pltpu.VMEM((HEAD_DIM, m_cols), F32),
                            pltpu.VMEM((tk, m_cols), BF16),
                            pltpu.VMEM((tk, m_cols), BF16)],
            compiler_params=_params(("arbitrary", "arbitrary"), 52),
            name="gqa_bounded")(b, qt, k, vt, w_ff2)

    bound = (HEAD_DIM * Q_SCALE * SCORE_BOUND_SLACK
             * jnp.max(jnp.abs(q_gain)) * jnp.max(jnp.abs(k_gain))).astype(F32)
    return lax.cond(bound <= MAX_FIXED_SHIFT, bounded, online, bound.reshape(1, 1))


def _merge_kernel(o_ref, w_ref, mc_ref, ga_ref, out_ref, wb_ref):
    _cast_once(pl.program_id(1), w_ref, wb_ref)
    ya = jnp.dot(o_ref[...], wb_ref[...], preferred_element_type=F32)
    out_ref[...] = (mc_ref[...] + ga_ref[...].astype(F32) * ya).astype(out_ref.dtype)


def _merge(o, w_ap, m_c, gates, tm=1024, tn=1024):
    s, d = o.shape
    ga_off = D_MODEL // tn
    return pl.pallas_call(
        _merge_kernel,
        grid=(D_MODEL // tn, s // tm),
        in_specs=[pl.BlockSpec((tm, d), lambda j, i: (i, 0)),
                  pl.BlockSpec((d, tn), lambda j, i: (0, j)),
                  pl.BlockSpec((tm, tn), lambda j, i: (i, j)),
                  pl.BlockSpec((tm, tn), lambda j, i: (i, ga_off + j))],
        out_specs=pl.BlockSpec((tm, tn), lambda j, i: (i, j)),
        out_shape=jax.ShapeDtypeStruct((s, D_MODEL), BF16),
        scratch_shapes=[pltpu.VMEM((d, tn), BF16)],
        compiler_params=_params(("arbitrary", "arbitrary"), 56),
        name="merge_attn_proj",
    )(o, w_ap, m_c, gates)


def _out_kernel(a_ref, w_ref, x_ref, g_ref, x1_ref, h_ref, wb_ref):
    _cast_once(pl.program_id(0), w_ref, wb_ref)
    x1 = x_ref[...] + jnp.dot(a_ref[...], wb_ref[...], preferred_element_type=F32)
    x1_ref[...] = x1
    h_ref[...] = _rms_rows(x1, g_ref[...]).astype(h_ref.dtype)


def _out_proj(a, w, x, g, tm=256):
    s, d = x.shape
    return pl.pallas_call(
        _out_kernel,
        grid=(s // tm,),
        in_specs=[pl.BlockSpec((tm, d), lambda i: (i, 0)),
                  pl.BlockSpec((d, d), lambda i: (0, 0), pipeline_mode=pl.Buffered(1)),
                  pl.BlockSpec((tm, d), lambda i: (i, 0)),
                  pl.BlockSpec((1, d), lambda i: (0, 0))],
        out_specs=[pl.BlockSpec((tm, d), lambda i: (i, 0)),
                   pl.BlockSpec((tm, d), lambda i: (i, 0))],
        out_shape=[jax.ShapeDtypeStruct((s, d), F32),
                   jax.ShapeDtypeStruct((s, d), BF16)],
        scratch_shapes=[pltpu.VMEM((d, d), BF16)],
        compiler_params=_params(("arbitrary",), 56),
        name="out_proj_residual",
    )(a, w, x, g.reshape(1, d))


def _ffn2_kernel(a_ref, w_ref, x_ref, o_ref, acc_ref):
    kk = pl.program_id(2)

    @pl.when(kk == 0)
    def _():
        acc_ref[...] = x_ref[...]

    acc_ref[...] += jnp.dot(a_ref[...], w_ref[...], preferred_element_type=F32)

    @pl.when(kk == pl.num_programs(2) - 1)
    def _():
        o_ref[...] = acc_ref[...]


def _ffn2(a, w, x, tm=1024, tn=1024, tk=2048):
    s, kdim = a.shape
    d = x.shape[1]
    return pl.pallas_call(
        _ffn2_kernel,
        grid=(s // tm, d // tn, kdim // tk),
        in_specs=[pl.BlockSpec((tm, tk), lambda i, j, k: (i, k)),
                  pl.BlockSpec((tk, tn), lambda i, j, k: (k, j)),
                  pl.BlockSpec((tm, tn), lambda i, j, k: (i, j))],
        out_specs=pl.BlockSpec((tm, tn), lambda i, j, k: (i, j)),
        out_shape=jax.ShapeDtypeStruct((s, d), F32),
        scratch_shapes=[pltpu.VMEM((tm, tn), F32)],
        compiler_params=_params(("parallel", "parallel", "arbitrary"), 48),
        name="ffn_down_residual",
    )(a, w, x)


def _ple_kernel(x_ref, p_ref, gp_ref, wg_ref, wp_ref, gf_ref, o_ref, wgb_ref, wpb_ref):
    _cast_once(pl.program_id(0), wg_ref, wgb_ref)
    _cast_once(pl.program_id(0), wp_ref, wpb_ref)
    x = x_ref[...]
    h = _rms_rows(x, gp_ref[...]).astype(BF16)
    gate = _sigmoid(jnp.dot(h, wgb_ref[...], preferred_element_type=F32))
    pp = jnp.dot(p_ref[...].astype(BF16), wpb_ref[...], preferred_element_type=F32)
    x3 = x + gate * pp
    o_ref[...] = _rms_rows(x3, gf_ref[...])


def _ple_final(x, p, g_ple, w_gate, w_proj, g_final, tm=256):
    s, d = x.shape
    return pl.pallas_call(
        _ple_kernel,
        grid=(s // tm,),
        in_specs=[pl.BlockSpec((tm, d), lambda i: (i, 0)),
                  pl.BlockSpec((tm, PLE_DIM), lambda i: (i, 0)),
                  pl.BlockSpec((1, d), lambda i: (0, 0)),
                  pl.BlockSpec((d, d), lambda i: (0, 0), pipeline_mode=pl.Buffered(1)),
                  pl.BlockSpec((PLE_DIM, d), lambda i: (0, 0), pipeline_mode=pl.Buffered(1)),
                  pl.BlockSpec((1, d), lambda i: (0, 0))],
        out_specs=pl.BlockSpec((tm, d), lambda i: (i, 0)),
        out_shape=jax.ShapeDtypeStruct((s, d), F32),
        scratch_shapes=[pltpu.VMEM((d, d), BF16), pltpu.VMEM((PLE_DIM, d), BF16)],
        compiler_params=_params(("arbitrary",), 56),
        name="ple_final_norm",
    )(x, p, g_ple.reshape(1, d), w_gate, w_proj, g_final.reshape(1, d))


def _rope_tables():
    n_rows = SEQ // GRID_W
    inv_freq = ROPE_THETA ** (-jnp.arange(0, AXIS_DIM, 2, dtype=F32) / AXIS_DIM)
    ang_row = inv_freq[:, None] * jnp.arange(n_rows, dtype=jnp.int32).astype(F32)[None, :]
    ang_col = inv_freq[:, None] * jnp.arange(GRID_W, dtype=jnp.int32).astype(F32)[None, :]
    nf = inv_freq.shape[0]

    def over_t(row_tab, col_tab):
        r = jnp.broadcast_to(row_tab[:, :, None], (nf, n_rows, GRID_W)).reshape(nf, SEQ)
        c = jnp.broadcast_to(col_tab[:, None, :], (nf, n_rows, GRID_W)).reshape(nf, SEQ)
        return r, c

    cr, cc = over_t(jnp.cos(ang_row), jnp.cos(ang_col))
    sr, sc = over_t(jnp.sin(ang_row), jnp.sin(ang_col))
    cos_t = jnp.concatenate([cr, cr, cc, cc], axis=0)
    sin_t = jnp.concatenate([-sr, sr, -sc, sc], axis=0)
    return cos_t, sin_t


def kernel(x, p, norm_mix, w_in, w_dw, conv_ln_g, conv_ln_b, w_conv_proj, q_norm, k_norm,
           w_attn_proj, w_out, norm_ffn, w_ff1, w_ff2, norm_ple, w_ple_gate, w_ple_proj, norm_final):
    depth = w_in.shape[0]
    assert depth == 1, "the final norm is fused into the last layer's kernel"
    tabs_t = _rope_tables()
    xs = x[0]
    for li in range(depth):
        w_i = w_in[li]
        h, h_t = _rmsnorm_cast(xs, norm_mix[li])
        u = _glu_proj(h, w_i)
        qt = _qk_proj(w_i, h_t, q_norm[li], tabs_t, OFF_Q, Q_W, Q_SCALE, False, "qt_proj")
        k = _qk_proj(w_i, h_t, k_norm[li], tabs_t, OFF_K, KV_W, 1.0, True, "k_proj")
        vt = _vt_proj(w_i, h_t)
        gates = _act_proj(h, w_i, OFF_G, 2 * D_MODEL, "sigmoid", "gate_proj")
        m_c = _conv_branch(u, w_dw[li], conv_ln_g[li], conv_ln_b[li], w_conv_proj[li], gates)
        o, w_ff2_b = _attention(qt, k, vt, q_norm[li], k_norm[li], w_ff2[li])
        merged = _merge(o, w_attn_proj[li], m_c, gates)
        x1, h2 = _out_proj(merged, w_out[li], xs, norm_ffn[li])
        a = _act_proj(h2, w_ff1[li], 0, D_FF, "relu2", "ffn_up")
        x2 = _ffn2(a, w_ff2_b, x1)
        xs = _ple_final(x2, p[li, 0], norm_ple[li], w_ple_gate[li], w_ple_proj[li], norm_final)
    return xs[None]
```

```python
import functools
import math

import jax
import jax.numpy as jnp
from jax import lax
from jax.experimental import pallas as pl
from jax.experimental.pallas import tpu as pltpu

D_MODEL = 2048
SEQ = 8192
N_HEADS = 16
N_KV_HEADS = 4
HEAD_DIM = 128
GROUP = N_HEADS // N_KV_HEADS
ROPE_THETA = 10000.0
AXIS_DIM = HEAD_DIM // 2
GRID_W = 64
CONV_WIDTH = D_MODEL // 2
CONV_KERNEL = 31
CONV_HALO = 16
CONV_ROWS = 128
D_FF = 4 * D_MODEL
PLE_DIM = 256
EPS = 1e-6
Q_W = N_HEADS * HEAD_DIM
KV_W = N_KV_HEADS * HEAD_DIM
LANES = 128
SUBLANES = 8
BF16_SUBLANES = 16
V_ROWS = HEAD_DIM + BF16_SUBLANES
K_COLS = 2 * HEAD_DIM

OFF_CA = 0
OFF_CB = CONV_WIDTH
OFF_Q = 2 * CONV_WIDTH
OFF_K = OFF_Q + Q_W
OFF_V = OFF_K + KV_W
OFF_G = OFF_V + KV_W

Q_SCALE = (HEAD_DIM ** -0.5) * math.log2(math.e)
SCORE_BOUND_SLACK = 1.02
MAX_FIXED_SHIFT = 60.0

BF16 = jnp.bfloat16
F32 = jnp.float32
MIB = 1024 * 1024


def _params(sem, vmem_mib):
    return pltpu.CompilerParams(dimension_semantics=sem, vmem_limit_bytes=vmem_mib * MIB)


def _sigmoid(v):
    return 1.0 / (1.0 + jnp.exp(-v))


def _rms_rows(v, g):
    ms = jnp.mean(v * v, axis=-1, keepdims=True)
    return v * lax.rsqrt(ms + EPS) * g


def _cast_once(step, w_ref, wb_ref, transpose=False):
    @pl.when(step == 0)
    def _():
        w = w_ref[...]
        wb_ref[...] = (w.T if transpose else w).astype(wb_ref.dtype)


def _rmsnorm_kernel(x_ref, g_ref, o_ref, ot_ref):
    y = _rms_rows(x_ref[...], g_ref[...])
    o_ref[...] = y.astype(o_ref.dtype)
    for cb in range(0, y.shape[1], LANES):
        ot_ref[cb:cb + LANES, :] = y[:, cb:cb + LANES].T.astype(ot_ref.dtype)


def _rmsnorm_cast(x, g, tm=512):
    s, d = x.shape
    return pl.pallas_call(
        _rmsnorm_kernel,
        grid=(s // tm,),
        in_specs=[pl.BlockSpec((tm, d), lambda i: (i, 0)),
                  pl.BlockSpec((1, d), lambda i: (0, 0))],
        out_specs=[pl.BlockSpec((tm, d), lambda i: (i, 0)),
                   pl.BlockSpec((d, tm), lambda i: (0, i))],
        out_shape=[jax.ShapeDtypeStruct((s, d), BF16),
                   jax.ShapeDtypeStruct((d, s), BF16)],
        compiler_params=_params(("arbitrary",), 32),
        name="rmsnorm_cast",
    )(x, g.reshape(1, d))


def _glu_kernel(h_ref, wa_ref, wb_ref, o_ref, wab_ref, wbb_ref):
    _cast_once(pl.program_id(1), wa_ref, wab_ref)
    _cast_once(pl.program_id(1), wb_ref, wbb_ref)
    h = h_ref[...]
    a = jnp.dot(h, wab_ref[...], preferred_element_type=F32)
    b = jnp.dot(h, wbb_ref[...], preferred_element_type=F32)
    o_ref[...] = a * _sigmoid(b)


def _glu_proj(h, w, tm=1024, tn=512):
    s, d = h.shape
    nb = CONV_WIDTH // tn
    return pl.pallas_call(
        _glu_kernel,
        grid=(nb, s // tm),
        in_specs=[pl.BlockSpec((tm, d), lambda j, i: (i, 0)),
                  pl.BlockSpec((d, tn), lambda j, i: (0, OFF_CA // tn + j)),
                  pl.BlockSpec((d, tn), lambda j, i: (0, OFF_CB // tn + j))],
        out_specs=pl.BlockSpec((tm, tn), lambda j, i: (i, j)),
        out_shape=jax.ShapeDtypeStruct((s, CONV_WIDTH), F32),
        scratch_shapes=[pltpu.VMEM((d, tn), BF16), pltpu.VMEM((d, tn), BF16)],
        compiler_params=_params(("arbitrary", "arbitrary"), 44),
        name="glu_proj",
    )(h, w, w)


def _swap_axis_halves(y):
    q = AXIS_DIM // 2
    return jnp.concatenate([y[q:2 * q], y[0:q], y[3 * q:4 * q], y[2 * q:3 * q]], axis=0)


def _qk_kernel(w_ref, ht_ref, g_ref, cos_ref, sin_ref, o_ref, wt_ref, *, scale, row_major_out):
    _cast_once(pl.program_id(1), w_ref, wt_ref, transpose=True)
    zt = jnp.dot(wt_ref[...], ht_ref[...], preferred_element_type=F32)
    g = jnp.broadcast_to(g_ref[...], (HEAD_DIM, zt.shape[1]))
    cos = cos_ref[...]
    sin = sin_ref[...]
    for hh in range(zt.shape[0] // HEAD_DIM):
        zh = zt[hh * HEAD_DIM:(hh + 1) * HEAD_DIM, :]
        ms = jnp.mean(zh * zh, axis=0, keepdims=True)
        y = zh * lax.rsqrt(ms + EPS) * g
        r = y * cos + _swap_axis_halves(y) * sin
        if scale != 1.0:
            r = r * scale
        if row_major_out:
            o_ref[:, hh * K_COLS:hh * K_COLS + HEAD_DIM] = r.T.astype(o_ref.dtype)
            lane = lax.broadcasted_iota(jnp.int32, (zt.shape[1], K_COLS - HEAD_DIM), 1)
            o_ref[:, hh * K_COLS + HEAD_DIM:(hh + 1) * K_COLS] = (
                jnp.where(lane == 0, 1.0, 0.0).astype(o_ref.dtype))
        else:
            o_ref[hh * HEAD_DIM:(hh + 1) * HEAD_DIM, :] = r.astype(o_ref.dtype)


def _qk_proj(w, h_t, gain, tabs_t, col_off, width, scale, row_major_out, name, tm=1024):
    d, s = h_t.shape
    tn = min(width, 1024)
    cos_t, sin_t = tabs_t
    tab_spec = pl.BlockSpec((HEAD_DIM, tm), lambda j, i: (0, i))
    if row_major_out:
        heads_per_tile = tn // HEAD_DIM
        out_spec = pl.BlockSpec((tm, heads_per_tile * K_COLS), lambda j, i: (i, j))
        out_shape = jax.ShapeDtypeStruct((s, width // HEAD_DIM * K_COLS), BF16)
    else:
        out_spec = pl.BlockSpec((tn, tm), lambda j, i: (j, i))
        out_shape = jax.ShapeDtypeStruct((width, s), BF16)
    return pl.pallas_call(
        functools.partial(_qk_kernel, scale=scale, row_major_out=row_major_out),
        grid=(width // tn, s // tm),
        in_specs=[pl.BlockSpec((d, tn), lambda j, i: (0, col_off // tn + j)),
                  pl.BlockSpec((d, tm), lambda j, i: (0, i)),
                  pl.BlockSpec((HEAD_DIM, 1), lambda j, i: (0, 0)),
                  tab_spec, tab_spec],
        out_specs=out_spec,
        out_shape=out_shape,
        scratch_shapes=[pltpu.VMEM((tn, d), BF16)],
        compiler_params=_params(("arbitrary", "arbitrary"), 48),
        name=name,
    )(w, h_t, gain.reshape(HEAD_DIM, 1), cos_t, sin_t)


def _vt_kernel(w_ref, ht_ref, o_ref, wt_ref):
    _cast_once(pl.program_id(0), w_ref, wt_ref, transpose=True)
    zt = jnp.dot(wt_ref[...], ht_ref[...], preferred_element_type=F32)
    pad_rows = V_ROWS - HEAD_DIM
    row = lax.broadcasted_iota(jnp.int32, (pad_rows, zt.shape[1]), 0)
    ones_then_zeros = jnp.where(row == 0, 1.0, 0.0).astype(o_ref.dtype)
    for hh in range(N_KV_HEADS):
        o_ref[hh * V_ROWS:hh * V_ROWS + HEAD_DIM, :] = (
            zt[hh * HEAD_DIM:(hh + 1) * HEAD_DIM, :].astype(o_ref.dtype))
        o_ref[hh * V_ROWS + HEAD_DIM:(hh + 1) * V_ROWS, :] = ones_then_zeros


def _vt_proj(w, h_t, tm=1024):
    d, s = h_t.shape
    return pl.pallas_call(
        _vt_kernel,
        grid=(s // tm,),
        in_specs=[pl.BlockSpec((d, KV_W), lambda i: (0, OFF_V // KV_W)),
                  pl.BlockSpec((d, tm), lambda i: (0, i))],
        out_specs=pl.BlockSpec((N_KV_HEADS * V_ROWS, tm), lambda i: (0, i)),
        out_shape=jax.ShapeDtypeStruct((N_KV_HEADS * V_ROWS, s), BF16),
        scratch_shapes=[pltpu.VMEM((KV_W, d), BF16)],
        compiler_params=_params(("arbitrary",), 40),
        name="vt_proj",
    )(w, h_t)


def _act_kernel(h_ref, w_ref, o_ref, wb_ref, *, act):
    _cast_once(pl.program_id(1), w_ref, wb_ref)
    z = jnp.dot(h_ref[...], wb_ref[...], preferred_element_type=F32)
    if act == "sigmoid":
        z = _sigmoid(z)
    elif act == "relu2":
        z = jnp.square(jnp.maximum(z, 0.0))
    o_ref[...] = z.astype(o_ref.dtype)


def _act_proj(h, w, col_off, width, act, name, tm=2048, tn=1024):
    s, d = h.shape
    return pl.pallas_call(
        functools.partial(_act_kernel, act=act),
        grid=(width // tn, s // tm),
        in_specs=[pl.BlockSpec((tm, d), lambda j, i: (i, 0)),
                  pl.BlockSpec((d, tn), lambda j, i: (0, col_off // tn + j))],
        out_specs=pl.BlockSpec((tm, tn), lambda j, i: (i, j)),
        out_shape=jax.ShapeDtypeStruct((s, width), BF16),
        scratch_shapes=[pltpu.VMEM((d, tn), BF16)],
        compiler_params=_params(("arbitrary", "arbitrary"), 58),
        name=name,
    )(h, w)


def _conv_kernel(u_ref, up_ref, un_ref, wdw_ref, lng_ref, lnb_ref, wp_ref, gate_ref,
                 o_ref, buf_ref, cv_ref, wpb_ref):
    i = pl.program_id(0)
    tm = u_ref.shape[0]
    n_slabs = CONV_WIDTH // LANES
    half = CONV_ROWS // 2
    base = CONV_HALO - CONV_KERNEL // 2
    _cast_once(i, wp_ref, wpb_ref)
    prev_ok = (i > 0).astype(F32)
    next_ok = (i < pl.num_programs(0) - 1).astype(F32)
    for sl in range(n_slabs):
        lanes = slice(sl * LANES, (sl + 1) * LANES)
        buf_ref[sl, 0:CONV_HALO, :] = up_ref[:, lanes] * prev_ok
        buf_ref[sl, CONV_HALO:CONV_HALO + tm, :] = u_ref[:, lanes]
        buf_ref[sl, CONV_HALO + tm:, :] = un_ref[:, lanes] * next_ok

    for sl in range(n_slabs):
        def body(r, carry, sl=sl):
            r0 = r * CONV_ROWS
            for par in range(2):
                acc = jnp.zeros((half, LANES), F32)
                for k in range(CONV_KERNEL):
                    tap = buf_ref[sl, pl.ds(r0 + (par + base + k), half, stride=2), :]
                    acc = acc + tap * wdw_ref[k:k + 1, sl * LANES:(sl + 1) * LANES]
                cv_ref[sl, pl.ds(r0 + par, half, stride=2), :] = acc
            return carry
        lax.fori_loop(0, tm // CONV_ROWS, body, 0)

    cv = jnp.concatenate([cv_ref[sl] for sl in range(n_slabs)], axis=1)
    mu = jnp.mean(cv, axis=-1, keepdims=True)
    xc = cv - mu
    var = jnp.mean(xc * xc, axis=-1, keepdims=True)
    y = xc * lax.rsqrt(var + EPS) * lng_ref[...] + lnb_ref[...]
    y = y * _sigmoid(y)
    yc = jnp.dot(y.astype(BF16), wpb_ref[...], preferred_element_type=F32)
    o_ref[...] = gate_ref[...].astype(F32) * yc


def _conv_branch(u, w_dw, ln_g, ln_b, w_proj, gates, tm=512):
    s = u.shape[0]
    hb = tm // CONV_HALO
    n_hblk = s // CONV_HALO
    return pl.pallas_call(
        _conv_kernel,
        grid=(s // tm,),
        in_specs=[pl.BlockSpec((tm, CONV_WIDTH), lambda i: (i, 0)),
                  pl.BlockSpec((CONV_HALO, CONV_WIDTH), lambda i: (jnp.maximum(i * hb - 1, 0), 0)),
                  pl.BlockSpec((CONV_HALO, CONV_WIDTH), lambda i: (jnp.minimum((i + 1) * hb, n_hblk - 1), 0)),
                  pl.BlockSpec((CONV_KERNEL, CONV_WIDTH), lambda i: (0, 0)),
                  pl.BlockSpec((1, CONV_WIDTH), lambda i: (0, 0)),
                  pl.BlockSpec((1, CONV_WIDTH), lambda i: (0, 0)),
                  pl.BlockSpec((CONV_WIDTH, D_MODEL), lambda i: (0, 0), pipeline_mode=pl.Buffered(1)),
                  pl.BlockSpec((tm, D_MODEL), lambda i: (i, 0))],
        out_specs=pl.BlockSpec((tm, D_MODEL), lambda i: (i, 0)),
        out_shape=jax.ShapeDtypeStruct((s, D_MODEL), F32),
        scratch_shapes=[pltpu.VMEM((CONV_WIDTH // LANES, tm + 2 * CONV_HALO, LANES), F32),
                        pltpu.VMEM((CONV_WIDTH // LANES, tm, LANES), F32),
                        pltpu.VMEM((CONV_WIDTH, D_MODEL), BF16)],
        compiler_params=_params(("arbitrary",), 48),
        name="conv_branch",
    )(u, u, u, w_dw, ln_g.reshape(1, -1), ln_b.reshape(1, -1), w_proj, gates)


def _stage_queries(qt_ref, qs_ref, tq, shift):
    for hh in range(GROUP):
        qs_ref[0:HEAD_DIM, hh * tq:(hh + 1) * tq] = qt_ref[hh * HEAD_DIM:(hh + 1) * HEAD_DIM, :]
    row = lax.broadcasted_iota(jnp.int32, (K_COLS - HEAD_DIM, qs_ref.shape[1]), 0)
    qs_ref[HEAD_DIM:K_COLS, :] = jnp.where(row == 0, -shift, 0.0).astype(qs_ref.dtype)


def _store_attention_out(o_t, o_ref, tq):
    for hh in range(GROUP):
        o_ref[:, hh * HEAD_DIM:(hh + 1) * HEAD_DIM] = o_t[:, hh * tq:(hh + 1) * tq].T.astype(o_ref.dtype)


def _chunk_start(c, tk):
    return c * tk if isinstance(c, int) else pl.multiple_of(c * tk, tk)


def _attn_online_kernel(b_ref, qt_ref, k_ref, vt_ref, o_ref, qs_ref, acc_ref, s0_ref, s1_ref, *, tq, tk):
    m_cols = GROUP * tq
    n_chunks = k_ref.shape[0] // tk
    assert n_chunks % 2 == 0 and n_chunks >= 2
    _stage_queries(qt_ref, qs_ref, tq, b_ref[0, 0])
    acc_ref[...] = jnp.zeros(acc_ref.shape, F32)

    def scores(c, dst_ref):
        s = jnp.dot(k_ref[pl.ds(_chunk_start(c, tk), tk), :], qs_ref[...], preferred_element_type=F32)
        dst_ref[...] = s
        return jnp.max(s, axis=0, keepdims=True)

    def update(c, src_ref, col_max, m_prev):
        m_new = jnp.maximum(m_prev, col_max)
        alpha = jnp.exp2(m_prev - m_new)
        p = jnp.exp2(src_ref[...] - m_new).astype(BF16)
        vtc = vt_ref[:, pl.ds(_chunk_start(c, tk), tk)]
        acc_ref[...] = alpha * acc_ref[...] + jnp.dot(vtc, p, preferred_element_type=F32)
        return m_new

    def pair(j, carry):
        m_run, cm0 = carry
        c = 2 * j
        cm1 = scores(c + 1, s1_ref)
        m_run = update(c, s0_ref, cm0, m_run)
        cm2 = scores(c + 2, s0_ref)
        m_run = update(c + 1, s1_ref, cm1, m_run)
        return m_run, cm2

    cm0 = scores(0, s0_ref)
    carry = (jnp.full((1, m_cols), -1e30, F32), cm0)
    m_run, cm0 = lax.fori_loop(0, n_chunks // 2 - 1, pair, carry)
    cm1 = scores(n_chunks - 1, s1_ref)
    m_run = update(n_chunks - 2, s0_ref, cm0, m_run)
    update(n_chunks - 1, s1_ref, cm1, m_run)
    _store_attention_out(acc_ref[0:HEAD_DIM, :] / acc_ref[HEAD_DIM:HEAD_DIM + 1, :], o_ref, tq)


def _attn_bounded_kernel(b_ref, qt_ref, k_ref, vt_ref, w2_ref, o_ref, w2b_ref,
                         qs_ref, acc_ref, p0_ref, p1_ref, *, tq, tk):
    m_cols = GROUP * tq
    n_chunks = k_ref.shape[0] // tk
    assert n_chunks % 2 == 0 and n_chunks >= 2
    w2b_ref[...] = w2_ref[...].astype(w2b_ref.dtype)
    _stage_queries(qt_ref, qs_ref, tq, b_ref[0, 0])
    acc_ref[...] = jnp.zeros(acc_ref.shape, F32)

    def probs(c, dst_ref, l_run):
        s = jnp.dot(k_ref[pl.ds(_chunk_start(c, tk), tk), :], qs_ref[...], preferred_element_type=F32)
        p = jnp.exp2(s)
        dst_ref[...] = p.astype(BF16)
        return l_run + jnp.sum(p, axis=0, keepdims=True)

    def accumulate(c, src_ref):
        vtc = vt_ref[0:HEAD_DIM, pl.ds(_chunk_start(c, tk), tk)]
        acc_ref[...] += jnp.dot(vtc, src_ref[...], preferred_element_type=F32)

    def pair(j, l_run):
        c = 2 * j
        l_run = probs(c + 1, p1_ref, l_run)
        accumulate(c, p0_ref)
        l_run = probs(c + 2, p0_ref, l_run)
        accumulate(c + 1, p1_ref)
        return l_run

    l_run = probs(0, p0_ref, jnp.zeros((1, m_cols), F32))
    l_run = lax.fori_loop(0, n_chunks // 2 - 1, pair, l_run)
    l_run = probs(n_chunks - 1, p1_ref, l_run)
    accumulate(n_chunks - 2, p0_ref)
    accumulate(n_chunks - 1, p1_ref)
    _store_attention_out(acc_ref[...] / l_run, o_ref, tq)


def _attention(qt, k, vt, q_gain, k_gain, w_ff2, tq_online=256, tq_bounded=1024, tk=512):
    s = k.shape[0]
    gw = GROUP * HEAD_DIM
    smem = pl.BlockSpec(memory_space=pltpu.SMEM)
    attn_specs = lambda tq: [smem,
                             pl.BlockSpec((gw, tq), lambda g, i: (g, i)),
                             pl.BlockSpec((s, K_COLS), lambda g, i: (0, g)),
                             pl.BlockSpec((V_ROWS, s), lambda g, i: (g, 0))]
    out_spec = lambda tq: pl.BlockSpec((tq, gw), lambda g, i: (i, g))
    o_shape = jax.ShapeDtypeStruct((s, Q_W), BF16)

    def online(b):
        tq = tq_online
        m_cols = GROUP * tq
        o = pl.pallas_call(
            functools.partial(_attn_online_kernel, tq=tq, tk=tk),
            grid=(N_KV_HEADS, s // tq),
            in_specs=attn_specs(tq),
            out_specs=out_spec(tq),
            out_shape=o_shape,
            scratch_shapes=[pltpu.VMEM((K_COLS, m_cols), BF16),
                            pltpu.VMEM((V_ROWS, m_cols), F32),
                            pltpu.VMEM((tk, m_cols), F32),
                            pltpu.VMEM((tk, m_cols), F32)],
            compiler_params=_params(("arbitrary", "arbitrary"), 48),
            name="gqa_online")(b, qt, k, vt)
        return o, w_ff2.astype(BF16)

    def bounded(b):
        tq = tq_bounded
        m_cols = GROUP * tq
        n_i = s // tq
        w2_rows = w_ff2.shape[0] // (N_KV_HEADS * n_i)
        w2_spec = pl.BlockSpec((w2_rows, w_ff2.shape[1]), lambda g, i: (g * n_i + i, 0))
        return pl.pallas_call(
            functools.partial(_attn_bounded_kernel, tq=tq, tk=tk),
            grid=(N_KV_HEADS, n_i),
            in_specs=attn_specs(tq) + [w2_spec],
            out_specs=[out_spec(tq), w2_spec],
            out_shape=[o_shape, jax.ShapeDtypeStruct(w_ff2.shape, BF16)],
            scratch_shapes=[pltpu.VMEM((K_COLS, m_cols), BF16),
                            pltpu.VMEM((HEAD_DIM, m_cols), F32),
                            pltpu.VMEM((tk, m_cols), BF16),
                            pltpu.VMEM((tk, m_cols), BF16)],
            compiler_params=_params(("arbitrary", "arbitrary"), 52),
            name="gqa_bounded")(b, qt, k, vt, w_ff2)

    bound = (HEAD_DIM * Q_SCALE * SCORE_BOUND_SLACK
             * jnp.max(jnp.abs(q_gain)) * jnp.max(jnp.abs(k_gain))).astype(F32)
    return lax.cond(bound <= MAX_FIXED_SHIFT, bounded, online, bound.reshape(1, 1))


def _merge_kernel(o_ref, w_ref, mc_ref, ga_ref, out_ref, wb_ref):
    _cast_once(pl.program_id(1), w_ref, wb_ref)
    ya = jnp.dot(o_ref[...], wb_ref[...], preferred_element_type=F32)
    out_ref[...] = (mc_ref[...] + ga_ref[...].astype(F32) * ya).astype(out_ref.dtype)


def _merge(o, w_ap, m_c, gates, tm=1024, tn=1024):
    s, d = o.shape
    ga_off = D_MODEL // tn
    return pl.pallas_call(
        _merge_kernel,
        grid=(D_MODEL // tn, s // tm),
        in_specs=[pl.BlockSpec((tm, d), lambda j, i: (i, 0)),
                  pl.BlockSpec((d, tn), lambda j, i: (0, j)),
                  pl.BlockSpec((tm, tn), lambda j, i: (i, j)),
                  pl.BlockSpec((tm, tn), lambda j, i: (i, ga_off + j))],
        out_specs=pl.BlockSpec((tm, tn), lambda j, i: (i, j)),
        out_shape=jax.ShapeDtypeStruct((s, D_MODEL), BF16),
        scratch_shapes=[pltpu.VMEM((d, tn), BF16)],
        compiler_params=_params(("arbitrary", "arbitrary"), 56),
        name="merge_attn_proj",
    )(o, w_ap, m_c, gates)


def _out_kernel(a_ref, w_ref, x_ref, g_ref, x1_ref, h_ref, wb_ref):
    _cast_once(pl.program_id(0), w_ref, wb_ref)
    x1 = x_ref[...] + jnp.dot(a_ref[...], wb_ref[...], preferred_element_type=F32)
    x1_ref[...] = x1
    h_ref[...] = _rms_rows(x1, g_ref[...]).astype(h_ref.dtype)


def _out_proj(a, w, x, g, tm=512):
    s, d = x.shape
    return pl.pallas_call(
        _out_kernel,
        grid=(s // tm,),
        in_specs=[pl.BlockSpec((tm, d), lambda i: (i, 0)),
                  pl.BlockSpec((d, d), lambda i: (0, 0), pipeline_mode=pl.Buffered(1)),
                  pl.BlockSpec((tm, d), lambda i: (i, 0)),
                  pl.BlockSpec((1, d), lambda i: (0, 0))],
        out_specs=[pl.BlockSpec((tm, d), lambda i: (i, 0)),
                   pl.BlockSpec((tm, d), lambda i: (i, 0))],
        out_shape=[jax.ShapeDtypeStruct((s, d), F32),
                   jax.ShapeDtypeStruct((s, d), BF16)],
        scratch_shapes=[pltpu.VMEM((d, d), BF16)],
        compiler_params=_params(("arbitrary",), 56),
        name="out_proj_residual",
    )(a, w, x, g.reshape(1, d))


def _ffn2_kernel(a_ref, w_ref, x_ref, o_ref, acc_ref):
    kk = pl.program_id(2)

    @pl.when(kk == 0)
    def _():
        acc_ref[...] = x_ref[...]

    acc_ref[...] += jnp.dot(a_ref[...], w_ref[...], preferred_element_type=F32)

    @pl.when(kk == pl.num_programs(2) - 1)
    def _():
        o_ref[...] = acc_ref[...]


def _ffn2(a, w, x, tm=1024, tn=1024, tk=2048):
    s, kdim = a.shape
    d = x.shape[1]
    return pl.pallas_call(
        _ffn2_kernel,
        grid=(s // tm, d // tn, kdim // tk),
        in_specs=[pl.BlockSpec((tm, tk), lambda i, j, k: (i, k)),
                  pl.BlockSpec((tk, tn), lambda i, j, k: (k, j)),
                  pl.BlockSpec((tm, tn), lambda i, j, k: (i, j))],
        out_specs=pl.BlockSpec((tm, tn), lambda i, j, k: (i, j)),
        out_shape=jax.ShapeDtypeStruct((s, d), F32),
        scratch_shapes=[pltpu.VMEM((tm, tn), F32)],
        compiler_params=_params(("parallel", "parallel", "arbitrary"), 48),
        name="ffn_down_residual",
    )(a, w, x)


def _ple_kernel(x_ref, p_ref, gp_ref, wg_ref, wp_ref, gf_ref, o_ref, wgb_ref, wpb_ref):
    _cast_once(pl.program_id(0), wg_ref, wgb_ref)
    _cast_once(pl.program_id(0), wp_ref, wpb_ref)
    x = x_ref[...]
    h = _rms_rows(x, gp_ref[...]).astype(BF16)
    gate = _sigmoid(jnp.dot(h, wgb_ref[...], preferred_element_type=F32))
    pp = jnp.dot(p_ref[...].astype(BF16), wpb_ref[...], preferred_element_type=F32)
    x3 = x + gate * pp
    o_ref[...] = _rms_rows(x3, gf_ref[...])


def _ple_final(x, p, g_ple, w_gate, w_proj, g_final, tm=512):
    s, d = x.shape
    return pl.pallas_call(
        _ple_kernel,
        grid=(s // tm,),
        in_specs=[pl.BlockSpec((tm, d), lambda i: (i, 0)),
                  pl.BlockSpec((tm, PLE_DIM), lambda i: (i, 0)),
                  pl.BlockSpec((1, d), lambda i: (0, 0)),
                  pl.BlockSpec((d, d), lambda i: (0, 0), pipeline_mode=pl.Buffered(1)),
                  pl.BlockSpec((PLE_DIM, d), lambda i: (0, 0), pipeline_mode=pl.Buffered(1)),
                  pl.BlockSpec((1, d), lambda i: (0, 0))],
        out_specs=pl.BlockSpec((tm, d), lambda i: (i, 0)),
        out_shape=jax.ShapeDtypeStruct((s, d), F32),
        scratch_shapes=[pltpu.VMEM((d, d), BF16), pltpu.VMEM((PLE_DIM, d), BF16)],
        compiler_params=_params(("arbitrary",), 56),
        name="ple_final_norm",
    )(x, p, g_ple.reshape(1, d), w_gate, w_proj, g_final.reshape(1, d))


def _rope_tables():
    n_rows = SEQ // GRID_W
    inv_freq = ROPE_THETA ** (-jnp.arange(0, AXIS_DIM, 2, dtype=F32) / AXIS_DIM)
    ang_row = inv_freq[:, None] * jnp.arange(n_rows, dtype=jnp.int32).astype(F32)[None, :]
    ang_col = inv_freq[:, None] * jnp.arange(GRID_W, dtype=jnp.int32).astype(F32)[None, :]
    nf = inv_freq.shape[0]

    def over_t(row_tab, col_tab):
        r = jnp.broadcast_to(row_tab[:, :, None], (nf, n_rows, GRID_W)).reshape(nf, SEQ)
        c = jnp.broadcast_to(col_tab[:, None, :], (nf, n_rows, GRID_W)).reshape(nf, SEQ)
        return r, c

    cr, cc = over_t(jnp.cos(ang_row), jnp.cos(ang_col))
    sr, sc = over_t(jnp.sin(ang_row), jnp.sin(ang_col))
    cos_t = jnp.concatenate([cr, cr, cc, cc], axis=0)
    sin_t = jnp.concatenate([-sr, sr, -sc, sc], axis=0)
    return cos_t, sin_t


def kernel(x, p, norm_mix, w_in, w_dw, conv_ln_g, conv_ln_b, w_conv_proj, q_norm, k_norm,
           w_attn_proj, w_out, norm_ffn, w_ff1, w_ff2, norm_ple, w_ple_gate, w_ple_proj, norm_final):
    depth = w_in.shape[0]
    assert depth == 1, "the final norm is fused into the last layer's kernel"
    tabs_t = _rope_tables()
    xs = x[0]
    for li in range(depth):
        w_i = w_in[li]
        h, h_t = _rmsnorm_cast(xs, norm_mix[li])
        u = _glu_proj(h, w_i)
        qt = _qk_proj(w_i, h_t, q_norm[li], tabs_t, OFF_Q, Q_W, Q_SCALE, False, "qt_proj")
        k = _qk_proj(w_i, h_t, k_norm[li], tabs_t, OFF_K, KV_W, 1.0, True, "k_proj")
        vt = _vt_proj(w_i, h_t)
        gates = _act_proj(h, w_i, OFF_G, 2 * D_MODEL, "sigmoid", "gate_proj")
        m_c = _conv_branch(u, w_dw[li], conv_ln_g[li], conv_ln_b[li], w_conv_proj[li], gates)
        o, w_ff2_b = _attention(qt, k, vt, q_norm[li], k_norm[li], w_ff2[li])
        merged = _merge(o, w_attn_proj[li], m_c, gates)
        x1, h2 = _out_proj(merged, w_out[li], xs, norm_ffn[li])
        a = _act_proj(h2, w_ff1[li], 0, D_FF, "relu2", "ffn_up")
        x2 = _ffn2(a, w_ff2_b, x1)
        xs = _ple_final(x2, p[li, 0], norm_ple[li], w_ple_gate[li], w_ple_proj[li], norm_final)
    return xs[None]
```

```python
import functools
import math

import jax
import jax.numpy as jnp
from jax import lax
from jax.experimental import pallas as pl
from jax.experimental.pallas import tpu as pltpu

D_MODEL = 2048
SEQ = 8192
N_HEADS = 16
N_KV_HEADS = 4
HEAD_DIM = 128
GROUP = N_HEADS // N_KV_HEADS
ROPE_THETA = 10000.0
AXIS_DIM = HEAD_DIM // 2
GRID_W = 64
CONV_WIDTH = D_MODEL // 2
CONV_KERNEL = 31
CONV_HALO = 16
CONV_ROWS = 128
D_FF = 4 * D_MODEL
PLE_DIM = 256
EPS = 1e-6
Q_W = N_HEADS * HEAD_DIM
KV_W = N_KV_HEADS * HEAD_DIM
LANES = 128
SUBLANES = 8
BF16_SUBLANES = 16
V_ROWS = HEAD_DIM + BF16_SUBLANES
K_COLS = 2 * HEAD_DIM

OFF_CA = 0
OFF_CB = CONV_WIDTH
OFF_Q = 2 * CONV_WIDTH
OFF_K = OFF_Q + Q_W
OFF_V = OFF_K + KV_W
OFF_G = OFF_V + KV_W

Q_SCALE = (HEAD_DIM ** -0.5) * math.log2(math.e)
SCORE_BOUND_SLACK = 1.02
MAX_FIXED_SHIFT = 60.0

BF16 = jnp.bfloat16
F32 = jnp.float32
MIB = 1024 * 1024


def _params(sem, vmem_mib):
    return pltpu.CompilerParams(dimension_semantics=sem, vmem_limit_bytes=vmem_mib * MIB)


def _sigmoid(v):
    return 1.0 / (1.0 + jnp.exp(-v))


def _rms_rows(v, g):
    ms = jnp.mean(v * v, axis=-1, keepdims=True)
    return v * lax.rsqrt(ms + EPS) * g


def _cast_once(step, w_ref, wb_ref, transpose=False):
    @pl.when(step == 0)
    def _():
        w = w_ref[...]
        wb_ref[...] = (w.T if transpose else w).astype(wb_ref.dtype)


def _rmsnorm_kernel(x_ref, g_ref, o_ref, ot_ref):
    y = _rms_rows(x_ref[...], g_ref[...])
    o_ref[...] = y.astype(o_ref.dtype)
    for cb in range(0, y.shape[1], LANES):
        ot_ref[cb:cb + LANES, :] = y[:, cb:cb + LANES].T.astype(ot_ref.dtype)


def _rmsnorm_cast(x, g, tm=512):
    s, d = x.shape
    return pl.pallas_call(
        _rmsnorm_kernel,
        grid=(s // tm,),
        in_specs=[pl.BlockSpec((tm, d), lambda i: (i, 0)),
                  pl.BlockSpec((1, d), lambda i: (0, 0))],
        out_specs=[pl.BlockSpec((tm, d), lambda i: (i, 0)),
                   pl.BlockSpec((d, tm), lambda i: (0, i))],
        out_shape=[jax.ShapeDtypeStruct((s, d), BF16),
                   jax.ShapeDtypeStruct((d, s), BF16)],
        compiler_params=_params(("arbitrary",), 32),
        name="rmsnorm_cast",
    )(x, g.reshape(1, d))


def _glu_kernel(h_ref, wa_ref, wb_ref, o_ref, wab_ref, wbb_ref):
    _cast_once(pl.program_id(1), wa_ref, wab_ref)
    _cast_once(pl.program_id(1), wb_ref, wbb_ref)
    h = h_ref[...]
    a = jnp.dot(h, wab_ref[...], preferred_element_type=F32)
    b = jnp.dot(h, wbb_ref[...], preferred_element_type=F32)
    o_ref[...] = a * _sigmoid(b)


def _glu_proj(h, w, tm=2048, tn=512):
    s, d = h.shape
    nb = CONV_WIDTH // tn
    return pl.pallas_call(
        _glu_kernel,
        grid=(nb, s // tm),
        in_specs=[pl.BlockSpec((tm, d), lambda j, i: (i, 0)),
                  pl.BlockSpec((d, tn), lambda j, i: (0, OFF_CA // tn + j)),
                  pl.BlockSpec((d, tn), lambda j, i: (0, OFF_CB // tn + j))],
        out_specs=pl.BlockSpec((tm, tn), lambda j, i: (i, j)),
        out_shape=jax.ShapeDtypeStruct((s, CONV_WIDTH), F32),
        scratch_shapes=[pltpu.VMEM((d, tn), BF16), pltpu.VMEM((d, tn), BF16)],
        compiler_params=_params(("arbitrary", "arbitrary"), 58),
        name="glu_proj",
    )(h, w, w)


def _swap_axis_halves(y):
    q = AXIS_DIM // 2
    return jnp.concatenate([y[q:2 * q], y[0:q], y[3 * q:4 * q], y[2 * q:3 * q]], axis=0)


def _qk_kernel(w_ref, ht_ref, g_ref, cos_ref, sin_ref, o_ref, wt_ref, *, scale, row_major_out):
    _cast_once(pl.program_id(1), w_ref, wt_ref, transpose=True)
    zt = jnp.dot(wt_ref[...], ht_ref[...], preferred_element_type=F32)
    g = jnp.broadcast_to(g_ref[...], (HEAD_DIM, zt.shape[1]))
    cos = cos_ref[...]
    sin = sin_ref[...]
    for hh in range(zt.shape[0] // HEAD_DIM):
        zh = zt[hh * HEAD_DIM:(hh + 1) * HEAD_DIM, :]
        ms = jnp.mean(zh * zh, axis=0, keepdims=True)
        y = zh * lax.rsqrt(ms + EPS) * g
        r = y * cos + _swap_axis_halves(y) * sin
        if scale != 1.0:
            r = r * scale
        if row_major_out:
            o_ref[:, hh * K_COLS:hh * K_COLS + HEAD_DIM] = r.T.astype(o_ref.dtype)
            lane = lax.broadcasted_iota(jnp.int32, (zt.shape[1], K_COLS - HEAD_DIM), 1)
            o_ref[:, hh * K_COLS + HEAD_DIM:(hh + 1) * K_COLS] = (
                jnp.where(lane == 0, 1.0, 0.0).astype(o_ref.dtype))
        else:
            o_ref[hh * HEAD_DIM:(hh + 1) * HEAD_DIM, :] = r.astype(o_ref.dtype)


def _qk_proj(w, h_t, gain, tabs_t, col_off, width, scale, row_major_out, name, tm=1024):
    d, s = h_t.shape
    tn = min(width, 1024)
    cos_t, sin_t = tabs_t
    tab_spec = pl.BlockSpec((HEAD_DIM, tm), lambda j, i: (0, i))
    if row_major_out:
        heads_per_tile = tn // HEAD_DIM
        out_spec = pl.BlockSpec((tm, heads_per_tile * K_COLS), lambda j, i: (i, j))
        out_shape = jax.ShapeDtypeStruct((s, width // HEAD_DIM * K_COLS), BF16)
    else:
        out_spec = pl.BlockSpec((tn, tm), lambda j, i: (j, i))
        out_shape = jax.ShapeDtypeStruct((width, s), BF16)
    return pl.pallas_call(
        functools.partial(_qk_kernel, scale=scale, row_major_out=row_major_out),
        grid=(width // tn, s // tm),
        in_specs=[pl.BlockSpec((d, tn), lambda j, i: (0, col_off // tn + j)),
                  pl.BlockSpec((d, tm), lambda j, i: (0, i)),
                  pl.BlockSpec((HEAD_DIM, 1), lambda j, i: (0, 0)),
                  tab_spec, tab_spec],
        out_specs=out_spec,
        out_shape=out_shape,
        scratch_shapes=[pltpu.VMEM((tn, d), BF16)],
        compiler_params=_params(("arbitrary", "arbitrary"), 58),
        name=name,
    )(w, h_t, gain.reshape(HEAD_DIM, 1), cos_t, sin_t)


def _vt_kernel(w_ref, ht_ref, o_ref, wt_ref):
    _cast_once(pl.program_id(0), w_ref, wt_ref, transpose=True)
    zt = jnp.dot(wt_ref[...], ht_ref[...], preferred_element_type=F32)
    pad_rows = V_ROWS - HEAD_DIM
    row = lax.broadcasted_iota(jnp.int32, (pad_rows, zt.shape[1]), 0)
    ones_then_zeros = jnp.where(row == 0, 1.0, 0.0).astype(o_ref.dtype)
    for hh in range(N_KV_HEADS):
        o_ref[hh * V_ROWS:hh * V_ROWS + HEAD_DIM, :] = (
            zt[hh * HEAD_DIM:(hh + 1) * HEAD_DIM, :].astype(o_ref.dtype))
        o_ref[hh * V_ROWS + HEAD_DIM:(hh + 1) * V_ROWS, :] = ones_then_zeros


def _vt_proj(w, h_t, tm=1024):
    d, s = h_t.shape
    return pl.pallas_call(
        _vt_kernel,
        grid=(s // tm,),
        in_specs=[pl.BlockSpec((d, KV_W), lambda i: (0, OFF_V // KV_W)),
                  pl.BlockSpec((d, tm), lambda i: (0, i))],
        out_specs=pl.BlockSpec((N_KV_HEADS * V_ROWS, tm), lambda i: (0, i)),
        out_shape=jax.ShapeDtypeStruct((N_KV_HEADS * V_ROWS, s), BF16),
        scratch_shapes=[pltpu.VMEM((KV_W, d), BF16)],
        compiler_params=_params(("arbitrary",), 40),
        name="vt_proj",
    )(w, h_t)


def _act_kernel(h_ref, w_ref, o_ref, wb_ref, *, act):
    _cast_once(pl.program_id(1), w_ref, wb_ref)
    z = jnp.dot(h_ref[...], wb_ref[...], preferred_element_type=F32)
    if act == "sigmoid":
        z = _sigmoid(z)
    elif act == "relu2":
        z = jnp.square(jnp.maximum(z, 0.0))
    o_ref[...] = z.astype(o_ref.dtype)


def _act_proj(h, w, col_off, width, act, name, tm=2048, tn=1024):
    s, d = h.shape
    return pl.pallas_call(
        functools.partial(_act_kernel, act=act),
        grid=(width // tn, s // tm),
        in_specs=[pl.BlockSpec((tm, d), lambda j, i: (i, 0)),
                  pl.BlockSpec((d, tn), lambda j, i: (0, col_off // tn + j))],
        out_specs=pl.BlockSpec((tm, tn), lambda j, i: (i, j)),
        out_shape=jax.ShapeDtypeStruct((s, width), BF16),
        scratch_shapes=[pltpu.VMEM((d, tn), BF16)],
        compiler_params=_params(("arbitrary", "arbitrary"), 58),
        name=name,
    )(h, w)


def _conv_kernel(u_ref, up_ref, un_ref, wdw_ref, lng_ref, lnb_ref, wp_ref, gate_ref,
                 o_ref, buf_ref, cv_ref, wpb_ref):
    i = pl.program_id(0)
    tm = u_ref.shape[0]
    n_slabs = CONV_WIDTH // LANES
    half = CONV_ROWS // 2
    base = CONV_HALO - CONV_KERNEL // 2
    _cast_once(i, wp_ref, wpb_ref)
    prev_ok = (i > 0).astype(F32)
    next_ok = (i < pl.num_programs(0) - 1).astype(F32)
    for sl in range(n_slabs):
        lanes = slice(sl * LANES, (sl + 1) * LANES)
        buf_ref[sl, 0:CONV_HALO, :] = up_ref[:, lanes] * prev_ok
        buf_ref[sl, CONV_HALO:CONV_HALO + tm, :] = u_ref[:, lanes]
        buf_ref[sl, CONV_HALO + tm:, :] = un_ref[:, lanes] * next_ok

    for sl in range(n_slabs):
        def body(r, carry, sl=sl):
            r0 = r * CONV_ROWS
            for par in range(2):
                acc = jnp.zeros((half, LANES), F32)
                for k in range(CONV_KERNEL):
                    tap = buf_ref[sl, pl.ds(r0 + (par + base + k), half, stride=2), :]
                    acc = acc + tap * wdw_ref[k:k + 1, sl * LANES:(sl + 1) * LANES]
                cv_ref[sl, pl.ds(r0 + par, half, stride=2), :] = acc
            return carry
        lax.fori_loop(0, tm // CONV_ROWS, body, 0)

    cv = jnp.concatenate([cv_ref[sl] for sl in range(n_slabs)], axis=1)
    mu = jnp.mean(cv, axis=-1, keepdims=True)
    xc = cv - mu
    var = jnp.mean(xc * xc, axis=-1, keepdims=True)
    y = xc * lax.rsqrt(var + EPS) * lng_ref[...] + lnb_ref[...]
    y = y * _sigmoid(y)
    yc = jnp.dot(y.astype(BF16), wpb_ref[...], preferred_element_type=F32)
    o_ref[...] = gate_ref[...].astype(F32) * yc


def _conv_branch(u, w_dw, ln_g, ln_b, w_proj, gates, tm=512):
    s = u.shape[0]
    hb = tm // CONV_HALO
    n_hblk = s // CONV_HALO
    return pl.pallas_call(
        _conv_kernel,
        grid=(s // tm,),
        in_specs=[pl.BlockSpec((tm, CONV_WIDTH), lambda i: (i, 0)),
                  pl.BlockSpec((CONV_HALO, CONV_WIDTH), lambda i: (jnp.maximum(i * hb - 1, 0), 0)),
                  pl.BlockSpec((CONV_HALO, CONV_WIDTH), lambda i: (jnp.minimum((i + 1) * hb, n_hblk - 1), 0)),
                  pl.BlockSpec((CONV_KERNEL, CONV_WIDTH), lambda i: (0, 0)),
                  pl.BlockSpec((1, CONV_WIDTH), lambda i: (0, 0)),
                  pl.BlockSpec((1, CONV_WIDTH), lambda i: (0, 0)),
                  pl.BlockSpec((CONV_WIDTH, D_MODEL), lambda i: (0, 0), pipeline_mode=pl.Buffered(1)),
                  pl.BlockSpec((tm, D_MODEL), lambda i: (i, 0))],
        out_specs=pl.BlockSpec((tm, D_MODEL), lambda i: (i, 0)),
        out_shape=jax.ShapeDtypeStruct((s, D_MODEL), F32),
        scratch_shapes=[pltpu.VMEM((CONV_WIDTH // LANES, tm + 2 * CONV_HALO, LANES), F32),
                        pltpu.VMEM((CONV_WIDTH // LANES, tm, LANES), F32),
                        pltpu.VMEM((CONV_WIDTH, D_MODEL), BF16)],
        compiler_params=_params(("arbitrary",), 48),
        name="conv_branch",
    )(u, u, u, w_dw, ln_g.reshape(1, -1), ln_b.reshape(1, -1), w_proj, gates)


def _stage_queries(qt_ref, qs_ref, tq, shift):
    for hh in range(GROUP):
        qs_ref[0:HEAD_DIM, hh * tq:(hh + 1) * tq] = qt_ref[hh * HEAD_DIM:(hh + 1) * HEAD_DIM, :]
    row = lax.broadcasted_iota(jnp.int32, (K_COLS - HEAD_DIM, qs_ref.shape[1]), 0)
    qs_ref[HEAD_DIM:K_COLS, :] = jnp.where(row == 0, -shift, 0.0).astype(qs_ref.dtype)


def _store_attention_out(o_t, o_ref, tq):
    for hh in range(GROUP):
        o_ref[:, hh * HEAD_DIM:(hh + 1) * HEAD_DIM] = o_t[:, hh * tq:(hh + 1) * tq].T.astype(o_ref.dtype)


def _chunk_start(c, tk):
    return c * tk if isinstance(c, int) else pl.multiple_of(c * tk, tk)


def _attn_online_kernel(b_ref, qt_ref, k_ref, vt_ref, o_ref, qs_ref, acc_ref, s0_ref, s1_ref, *, tq, tk):
    m_cols = GROUP * tq
    n_chunks = k_ref.shape[0] // tk
    assert n_chunks % 2 == 0 and n_chunks >= 2
    _stage_queries(qt_ref, qs_ref, tq, b_ref[0, 0])
    acc_ref[...] = jnp.zeros(acc_ref.shape, F32)

    def scores(c, dst_ref):
        s = jnp.dot(k_ref[pl.ds(_chunk_start(c, tk), tk), :], qs_ref[...], preferred_element_type=F32)
        dst_ref[...] = s
        return jnp.max(s, axis=0, keepdims=True)

    def update(c, src_ref, col_max, m_prev):
        m_new = jnp.maximum(m_prev, col_max)
        alpha = jnp.exp2(m_prev - m_new)
        p = jnp.exp2(src_ref[...] - m_new).astype(BF16)
        vtc = vt_ref[:, pl.ds(_chunk_start(c, tk), tk)]
        acc_ref[...] = alpha * acc_ref[...] + jnp.dot(vtc, p, preferred_element_type=F32)
        return m_new

    def pair(j, carry):
        m_run, cm0 = carry
        c = 2 * j
        cm1 = scores(c + 1, s1_ref)
        m_run = update(c, s0_ref, cm0, m_run)
        cm2 = scores(c + 2, s0_ref)
        m_run = update(c + 1, s1_ref, cm1, m_run)
        return m_run, cm2

    cm0 = scores(0, s0_ref)
    carry = (jnp.full((1, m_cols), -1e30, F32), cm0)
    m_run, cm0 = lax.fori_loop(0, n_chunks // 2 - 1, pair, carry)
    cm1 = scores(n_chunks - 1, s1_ref)
    m_run = update(n_chunks - 2, s0_ref, cm0, m_run)
    update(n_chunks - 1, s1_ref, cm1, m_run)
    _store_attention_out(acc_ref[0:HEAD_DIM, :] / acc_ref[HEAD_DIM:HEAD_DIM + 1, :], o_ref, tq)


def _attn_bounded_kernel(b_ref, qt_ref, k_ref, vt_ref, w2_ref, o_ref, w2b_ref,
                         qs_ref, acc_ref, p0_ref, p1_ref, *, tq, tk):
    m_cols = GROUP * tq
    n_chunks = k_ref.shape[0] // tk
    assert n_chunks % 2 == 0 and n_chunks >= 2
    w2b_ref[...] = w2_ref[...].astype(w2b_ref.dtype)
    _stage_queries(qt_ref, qs_ref, tq, b_ref[0, 0])
    acc_ref[...] = jnp.zeros(acc_ref.shape, F32)

    def probs(c, dst_ref, l_run):
        s = jnp.dot(k_ref[pl.ds(_chunk_start(c, tk), tk), :], qs_ref[...], preferred_element_type=F32)
        p = jnp.exp2(s)
        dst_ref[...] = p.astype(BF16)
        return l_run + jnp.sum(p, axis=0, keepdims=True)

    def accumulate(c, src_ref):
        vtc = vt_ref[0:HEAD_DIM, pl.ds(_chunk_start(c, tk), tk)]
        acc_ref[...] += jnp.dot(vtc, src_ref[...], preferred_element_type=F32)

    def pair(j, l_run):
        c = 2 * j
        l_run = probs(c + 1, p1_ref, l_run)
        accumulate(c, p0_ref)
        l_run = probs(c + 2, p0_ref, l_run)
        accumulate(c + 1, p1_ref)
        return l_run

    l_run = probs(0, p0_ref, jnp.zeros((1, m_cols), F32))
    l_run = lax.fori_loop(0, n_chunks // 2 - 1, pair, l_run)
    l_run = probs(n_chunks - 1, p1_ref, l_run)
    accumulate(n_chunks - 2, p0_ref)
    accumulate(n_chunks - 1, p1_ref)
    _store_attention_out(acc_ref[...] / l_run, o_ref, tq)


def _attention(qt, k, vt, q_gain, k_gain, w_ff2, tq_online=256, tq_bounded=1024, tk=512):
    s = k.shape[0]
    gw = GROUP * HEAD_DIM
    smem = pl.BlockSpec(memory_space=pltpu.SMEM)
    attn_specs = lambda tq: [smem,
                             pl.BlockSpec((gw, tq), lambda g, i: (g, i)),
                             pl.BlockSpec((s, K_COLS), lambda g, i: (0, g)),
                             pl.BlockSpec((V_ROWS, s), lambda g, i: (g, 0))]
    out_spec = lambda tq: pl.BlockSpec((tq, gw), lambda g, i: (i, g))
    o_shape = jax.ShapeDtypeStruct((s, Q_W), BF16)

    def online(b):
        tq = tq_online
        m_cols = GROUP * tq
        o = pl.pallas_call(
            functools.partial(_attn_online_kernel, tq=tq, tk=tk),
            grid=(N_KV_HEADS, s // tq),
            in_specs=attn_specs(tq),
            out_specs=out_spec(tq),
            out_shape=o_shape,
            scratch_shapes=[pltpu.VMEM((K_COLS, m_cols), BF16),
                            pltpu.VMEM((V_ROWS, m_cols), F32),
                            pltpu.VMEM((tk, m_cols), F32),
                            pltpu.VMEM((tk, m_cols), F32)],
            compiler_params=_params(("arbitrary", "arbitrary"), 48),
            name="gqa_online")(b, qt, k, vt)
        return o, w_ff2.astype(BF16)

    def bounded(b):
        tq = tq_bounded
        m_cols = GROUP * tq
        n_i = s // tq
        w2_rows = w_ff2.shape[0] // (N_KV_HEADS * n_i)
        w2_spec = pl.BlockSpec((w2_rows, w_ff2.shape[1]), lambda g, i: (g * n_i + i, 0))
        return pl.pallas_call(
            functools.partial(_attn_bounded_kernel, tq=tq, tk=tk),
            grid=(N_KV_HEADS, n_i),
            in_specs=attn_specs(tq) + [w2_spec],
            out_specs=[out_spec(tq), w2_spec],
            out_shape=[o_shape, jax.ShapeDtypeStruct(w_ff2.shape, BF16)],
            scratch_shapes=[pltpu.VMEM((K_COLS, m_cols), BF16),
                            pltpu.VMEM((HEAD_DIM, m_cols), F32),
                            pltpu.VMEM((tk, m_cols), BF16),
                            pltpu.VMEM((tk, m_cols), BF16)],
            compiler_params=_params(("arbitrary", "arbitrary"), 52),
            name="gqa_bounded")(b, qt, k, vt, w_ff2)

    bound = (HEAD_DIM * Q_SCALE * SCORE_BOUND_SLACK
             * jnp.max(jnp.abs(q_gain)) * jnp.max(jnp.abs(k_gain))).astype(F32)
    return lax.cond(bound <= MAX_FIXED_SHIFT, bounded, online, bound.reshape(1, 1))


def _merge_kernel(o_ref, w_ref, mc_ref, ga_ref, out_ref, wb_ref):
    _cast_once(pl.program_id(1), w_ref, wb_ref)
    ya = jnp.dot(o_ref[...], wb_ref[...], preferred_element_type=F32)
    out_ref[...] = (mc_ref[...] + ga_ref[...].astype(F32) * ya).astype(out_ref.dtype)


def _merge(o, w_ap, m_c, gates, tm=1024, tn=1024):
    s, d = o.shape
    ga_off = D_MODEL // tn
    return pl.pallas_call(
        _merge_kernel,
        grid=(D_MODEL // tn, s // tm),
        in_specs=[pl.BlockSpec((tm, d), lambda j, i: (i, 0)),
                  pl.BlockSpec((d, tn), lambda j, i: (0, j)),
                  pl.BlockSpec((tm, tn), lambda j, i: (i, j)),
                  pl.BlockSpec((tm, tn), lambda j, i: (i, ga_off + j))],
        out_specs=pl.BlockSpec((tm, tn), lambda j, i: (i, j)),
        out_shape=jax.ShapeDtypeStruct((s, D_MODEL), BF16),
        scratch_shapes=[pltpu.VMEM((d, tn), BF16)],
        compiler_params=_params(("arbitrary", "arbitrary"), 56),
        name="merge_attn_proj",
    )(o, w_ap, m_c, gates)


def _out_kernel(a_ref, w_ref, x_ref, g_ref, x1_ref, h_ref, wb_ref):
    _cast_once(pl.program_id(0), w_ref, wb_ref)
    x1 = x_ref[...] + jnp.dot(a_ref[...], wb_ref[...], preferred_element_type=F32)
    x1_ref[...] = x1
    h_ref[...] = _rms_rows(x1, g_ref[...]).astype(h_ref.dtype)


def _out_proj(a, w, x, g, tm=512):
    s, d = x.shape
    return pl.pallas_call(
        _out_kernel,
        grid=(s // tm,),
        in_specs=[pl.BlockSpec((tm, d), lambda i: (i, 0)),
                  pl.BlockSpec((d, d), lambda i: (0, 0), pipeline_mode=pl.Buffered(1)),
                  pl.BlockSpec((tm, d), lambda i: (i, 0)),
                  pl.BlockSpec((1, d), lambda i: (0, 0))],
        out_specs=[pl.BlockSpec((tm, d), lambda i: (i, 0)),
                   pl.BlockSpec((tm, d), lambda i: (i, 0))],
        out_shape=[jax.ShapeDtypeStruct((s, d), F32),
                   jax.ShapeDtypeStruct((s, d), BF16)],
        scratch_shapes=[pltpu.VMEM((d, d), BF16)],
        compiler_params=_params(("arbitrary",), 56),
        name="out_proj_residual",
    )(a, w, x, g.reshape(1, d))


def _ffn2_kernel(a_ref, w_ref, x_ref, o_ref):
    @pl.when(pl.program_id(2) == 0)
    def _():
        o_ref[...] = x_ref[...]

    o_ref[...] += jnp.dot(a_ref[...], w_ref[...], preferred_element_type=F32)


def _ffn2(a, w, x, tm=1024, tn=1024, tk=4096):
    s, kdim = a.shape
    d = x.shape[1]
    return pl.pallas_call(
        _ffn2_kernel,
        grid=(s // tm, d // tn, kdim // tk),
        in_specs=[pl.BlockSpec((tm, tk), lambda i, j, k: (i, k)),
                  pl.BlockSpec((tk, tn), lambda i, j, k: (k, j)),
                  pl.BlockSpec((tm, tn), lambda i, j, k: (i, j))],
        out_specs=pl.BlockSpec((tm, tn), lambda i, j, k: (i, j)),
        out_shape=jax.ShapeDtypeStruct((s, d), F32),
        compiler_params=_params(("arbitrary", "arbitrary", "arbitrary"), 58),
        name="ffn_down_residual",
    )(a, w, x)


def _ple_kernel(x_ref, p_ref, gp_ref, wg_ref, wp_ref, gf_ref, o_ref, wgb_ref, wpb_ref):
    _cast_once(pl.program_id(0), wg_ref, wgb_ref)
    _cast_once(pl.program_id(0), wp_ref, wpb_ref)
    x = x_ref[...]
    h = _rms_rows(x, gp_ref[...]).astype(BF16)
    gate = _sigmoid(jnp.dot(h, wgb_ref[...], preferred_element_type=F32))
    pp = jnp.dot(p_ref[...].astype(BF16), wpb_ref[...], preferred_element_type=F32)
    x3 = x + gate * pp
    o_ref[...] = _rms_rows(x3, gf_ref[...])


def _ple_final(x, p, g_ple, w_gate, w_proj, g_final, tm=512):
    s, d = x.shape
    return pl.pallas_call(
        _ple_kernel,
        grid=(s // tm,),
        in_specs=[pl.BlockSpec((tm, d), lambda i: (i, 0)),
                  pl.BlockSpec((tm, PLE_DIM), lambda i: (i, 0)),
                  pl.BlockSpec((1, d), lambda i: (0, 0)),
                  pl.BlockSpec((d, d), lambda i: (0, 0), pipeline_mode=pl.Buffered(1)),
                  pl.BlockSpec((PLE_DIM, d), lambda i: (0, 0), pipeline_mode=pl.Buffered(1)),
                  pl.BlockSpec((1, d), lambda i: (0, 0))],
        out_specs=pl.BlockSpec((tm, d), lambda i: (i, 0)),
        out_shape=jax.ShapeDtypeStruct((s, d), F32),
        scratch_shapes=[pltpu.VMEM((d, d), BF16), pltpu.VMEM((PLE_DIM, d), BF16)],
        compiler_params=_params(("arbitrary",), 56),
        name="ple_final_norm",
    )(x, p, g_ple.reshape(1, d), w_gate, w_proj, g_final.reshape(1, d))


def _rope_tables():
    n_rows = SEQ // GRID_W
    inv_freq = ROPE_THETA ** (-jnp.arange(0, AXIS_DIM, 2, dtype=F32) / AXIS_DIM)
    ang_row = inv_freq[:, None] * jnp.arange(n_rows, dtype=jnp.int32).astype(F32)[None, :]
    ang_col = inv_freq[:, None] * jnp.arange(GRID_W, dtype=jnp.int32).astype(F32)[None, :]
    nf = inv_freq.shape[0]

    def over_t(row_tab, col_tab):
        r = jnp.broadcast_to(row_tab[:, :, None], (nf, n_rows, GRID_W)).reshape(nf, SEQ)
        c = jnp.broadcast_to(col_tab[:, None, :], (nf, n_rows, GRID_W)).reshape(nf, SEQ)
        return r, c

    cr, cc = over_t(jnp.cos(ang_row), jnp.cos(ang_col))
    sr, sc = over_t(jnp.sin(ang_row), jnp.sin(ang_col))
    cos_t = jnp.concatenate([cr, cr, cc, cc], axis=0)
    sin_t = jnp.concatenate([-sr, sr, -sc, sc], axis=0)
    return cos_t, sin_t


def kernel(x, p, norm_mix, w_in, w_dw, conv_ln_g, conv_ln_b, w_conv_proj, q_norm, k_norm,
           w_attn_proj, w_out, norm_ffn, w_ff1, w_ff2, norm_ple, w_ple_gate, w_ple_proj, norm_final):
    depth = w_in.shape[0]
    assert depth == 1, "the final norm is fused into the last layer's kernel"
    tabs_t = _rope_tables()
    xs = x[0]
    for li in range(depth):
        w_i = w_in[li]
        h, h_t = _rmsnorm_cast(xs, norm_mix[li])
        u = _glu_proj(h, w_i)
        qt = _qk_proj(w_i, h_t, q_norm[li], tabs_t, OFF_Q, Q_W, Q_SCALE, False, "qt_proj", tm=2048)
        k = _qk_proj(w_i, h_t, k_norm[li], tabs_t, OFF_K, KV_W, 1.0, True, "k_proj")
        vt = _vt_proj(w_i, h_t)
        gates = _act_proj(h, w_i, OFF_G, 2 * D_MODEL, "sigmoid", "gate_proj")
        m_c = _conv_branch(u, w_dw[li], conv_ln_g[li], conv_ln_b[li], w_conv_proj[li], gates)
        o, w_ff2_b = _attention(qt, k, vt, q_norm[li], k_norm[li], w_ff2[li])
        merged = _merge(o, w_attn_proj[li], m_c, gates)
        x1, h2 = _out_proj(merged, w_out[li], xs, norm_ffn[li])
        a = _act_proj(h2, w_ff1[li], 0, D_FF, "relu2", "ffn_up")
        x2 = _ffn2(a, w_ff2_b, x1)
        xs = _ple_final(x2, p[li, 0], norm_ple[li], w_ple_gate[li], w_ple_proj[li], norm_final)
    return xs[None]
```

```python
import functools
import math

import jax
import jax.numpy as jnp
from jax import lax
from jax.experimental import pallas as pl
from jax.experimental.pallas import tpu as pltpu

D_MODEL = 2048
SEQ = 8192
N_HEADS = 16
N_KV_HEADS = 4
HEAD_DIM = 128
GROUP = N_HEADS // N_KV_HEADS
ROPE_THETA = 10000.0
AXIS_DIM = HEAD_DIM // 2
GRID_W = 64
CONV_WIDTH = D_MODEL // 2
CONV_KERNEL = 31
CONV_HALO = 16
CONV_ROWS = 128
D_FF = 4 * D_MODEL
PLE_DIM = 256
EPS = 1e-6
Q_W = N_HEADS * HEAD_DIM
KV_W = N_KV_HEADS * HEAD_DIM
LANES = 128
SUBLANES = 8
BF16_SUBLANES = 16
V_ROWS = HEAD_DIM + BF16_SUBLANES
K_COLS = 2 * HEAD_DIM

OFF_CA = 0
OFF_CB = CONV_WIDTH
OFF_Q = 2 * CONV_WIDTH
OFF_K = OFF_Q + Q_W
OFF_V = OFF_K + KV_W
OFF_G = OFF_V + KV_W

Q_SCALE = (HEAD_DIM ** -0.5) * math.log2(math.e)
SCORE_BOUND_SLACK = 1.02
MAX_FIXED_SHIFT = 60.0
BOUNDED_UNROLL = 4

BF16 = jnp.bfloat16
F32 = jnp.float32
MIB = 1024 * 1024


def _params(sem, vmem_mib):
    return pltpu.CompilerParams(dimension_semantics=sem, vmem_limit_bytes=vmem_mib * MIB)


def _sigmoid(v):
    return 1.0 / (1.0 + jnp.exp(-v))


def _rms_rows(v, g):
    ms = jnp.mean(v * v, axis=-1, keepdims=True)
    return v * lax.rsqrt(ms + EPS) * g


def _cast_once(step, w_ref, wb_ref, transpose=False):
    @pl.when(step == 0)
    def _():
        w = w_ref[...]
        wb_ref[...] = (w.T if transpose else w).astype(wb_ref.dtype)


def _rmsnorm_kernel(x_ref, g_ref, o_ref, ot_ref):
    y = _rms_rows(x_ref[...], g_ref[...])
    o_ref[...] = y.astype(o_ref.dtype)
    for cb in range(0, y.shape[1], LANES):
        ot_ref[cb:cb + LANES, :] = y[:, cb:cb + LANES].T.astype(ot_ref.dtype)


def _rmsnorm_cast(x, g, tm=512):
    s, d = x.shape
    return pl.pallas_call(
        _rmsnorm_kernel,
        grid=(s // tm,),
        in_specs=[pl.BlockSpec((tm, d), lambda i: (i, 0)),
                  pl.BlockSpec((1, d), lambda i: (0, 0))],
        out_specs=[pl.BlockSpec((tm, d), lambda i: (i, 0)),
                   pl.BlockSpec((d, tm), lambda i: (0, i))],
        out_shape=[jax.ShapeDtypeStruct((s, d), BF16),
                   jax.ShapeDtypeStruct((d, s), BF16)],
        compiler_params=_params(("arbitrary",), 32),
        name="rmsnorm_cast",
    )(x, g.reshape(1, d))


def _glu_kernel(h_ref, wa_ref, wb_ref, o_ref, wab_ref, wbb_ref):
    _cast_once(pl.program_id(1), wa_ref, wab_ref)
    _cast_once(pl.program_id(1), wb_ref, wbb_ref)
    h = h_ref[...]
    a = jnp.dot(h, wab_ref[...], preferred_element_type=F32)
    b = jnp.dot(h, wbb_ref[...], preferred_element_type=F32)
    o_ref[...] = a * _sigmoid(b)


def _glu_proj(h, w, tm=1024, tn=512):
    s, d = h.shape
    nb = CONV_WIDTH // tn
    return pl.pallas_call(
        _glu_kernel,
        grid=(nb, s // tm),
        in_specs=[pl.BlockSpec((tm, d), lambda j, i: (i, 0)),
                  pl.BlockSpec((d, tn), lambda j, i: (0, OFF_CA // tn + j)),
                  pl.BlockSpec((d, tn), lambda j, i: (0, OFF_CB // tn + j))],
        out_specs=pl.BlockSpec((tm, tn), lambda j, i: (i, j)),
        out_shape=jax.ShapeDtypeStruct((s, CONV_WIDTH), F32),
        scratch_shapes=[pltpu.VMEM((d, tn), BF16), pltpu.VMEM((d, tn), BF16)],
        compiler_params=_params(("arbitrary", "arbitrary"), 58),
        name="glu_proj",
    )(h, w, w)


def _swap_axis_halves(y):
    q = AXIS_DIM // 2
    return jnp.concatenate([y[q:2 * q], y[0:q], y[3 * q:4 * q], y[2 * q:3 * q]], axis=0)


def _qk_kernel(w_ref, ht_ref, g_ref, cos_ref, sin_ref, o_ref, wt_ref, *, scale, row_major_out):
    _cast_once(pl.program_id(1), w_ref, wt_ref, transpose=True)
    zt = jnp.dot(wt_ref[...], ht_ref[...], preferred_element_type=F32)
    g = jnp.broadcast_to(g_ref[...], (HEAD_DIM, zt.shape[1]))
    cos = cos_ref[...]
    sin = sin_ref[...]
    for hh in range(zt.shape[0] // HEAD_DIM):
        zh = zt[hh * HEAD_DIM:(hh + 1) * HEAD_DIM, :]
        ms = jnp.mean(zh * zh, axis=0, keepdims=True)
        y = zh * lax.rsqrt(ms + EPS) * g
        r = y * cos + _swap_axis_halves(y) * sin
        if scale != 1.0:
            r = r * scale
        if row_major_out:
            o_ref[:, hh * K_COLS:hh * K_COLS + HEAD_DIM] = r.T.astype(o_ref.dtype)
            lane = lax.broadcasted_iota(jnp.int32, (zt.shape[1], K_COLS - HEAD_DIM), 1)
            o_ref[:, hh * K_COLS + HEAD_DIM:(hh + 1) * K_COLS] = (
                jnp.where(lane == 0, 1.0, 0.0).astype(o_ref.dtype))
        else:
            o_ref[hh * HEAD_DIM:(hh + 1) * HEAD_DIM, :] = r.astype(o_ref.dtype)


def _qk_proj(w, h_t, gain, tabs_t, col_off, width, scale, row_major_out, name, tm=1024):
    d, s = h_t.shape
    tn = min(width, 1024)
    cos_t, sin_t = tabs_t
    tab_spec = pl.BlockSpec((HEAD_DIM, tm), lambda j, i: (0, i))
    if row_major_out:
        heads_per_tile = tn // HEAD_DIM
        out_spec = pl.BlockSpec((tm, heads_per_tile * K_COLS), lambda j, i: (i, j))
        out_shape = jax.ShapeDtypeStruct((s, width // HEAD_DIM * K_COLS), BF16)
    else:
        out_spec = pl.BlockSpec((tn, tm), lambda j, i: (j, i))
        out_shape = jax.ShapeDtypeStruct((width, s), BF16)
    return pl.pallas_call(
        functools.partial(_qk_kernel, scale=scale, row_major_out=row_major_out),
        grid=(width // tn, s // tm),
        in_specs=[pl.BlockSpec((d, tn), lambda j, i: (0, col_off // tn + j)),
                  pl.BlockSpec((d, tm), lambda j, i: (0, i)),
                  pl.BlockSpec((HEAD_DIM, 1), lambda j, i: (0, 0)),
                  tab_spec, tab_spec],
        out_specs=out_spec,
        out_shape=out_shape,
        scratch_shapes=[pltpu.VMEM((tn, d), BF16)],
        compiler_params=_params(("arbitrary", "arbitrary"), 58),
        name=name,
    )(w, h_t, gain.reshape(HEAD_DIM, 1), cos_t, sin_t)


def _vt_kernel(w_ref, ht_ref, o_ref, wt_ref):
    _cast_once(pl.program_id(0), w_ref, wt_ref, transpose=True)
    zt = jnp.dot(wt_ref[...], ht_ref[...], preferred_element_type=F32)
    pad_rows = V_ROWS - HEAD_DIM
    row = lax.broadcasted_iota(jnp.int32, (pad_rows, zt.shape[1]), 0)
    ones_then_zeros = jnp.where(row == 0, 1.0, 0.0).astype(o_ref.dtype)
    for hh in range(N_KV_HEADS):
        o_ref[hh * V_ROWS:hh * V_ROWS + HEAD_DIM, :] = (
            zt[hh * HEAD_DIM:(hh + 1) * HEAD_DIM, :].astype(o_ref.dtype))
        o_ref[hh * V_ROWS + HEAD_DIM:(hh + 1) * V_ROWS, :] = ones_then_zeros


def _vt_proj(w, h_t, tm=1024):
    d, s = h_t.shape
    return pl.pallas_call(
        _vt_kernel,
        grid=(s // tm,),
        in_specs=[pl.BlockSpec((d, KV_W), lambda i: (0, OFF_V // KV_W)),
                  pl.BlockSpec((d, tm), lambda i: (0, i))],
        out_specs=pl.BlockSpec((N_KV_HEADS * V_ROWS, tm), lambda i: (0, i)),
        out_shape=jax.ShapeDtypeStruct((N_KV_HEADS * V_ROWS, s), BF16),
        scratch_shapes=[pltpu.VMEM((KV_W, d), BF16)],
        compiler_params=_params(("arbitrary",), 40),
        name="vt_proj",
    )(w, h_t)


def _act_kernel(h_ref, w_ref, o_ref, wb_ref, *, act):
    _cast_once(pl.program_id(1), w_ref, wb_ref)
    z = jnp.dot(h_ref[...], wb_ref[...], preferred_element_type=F32)
    if act == "sigmoid":
        z = _sigmoid(z)
    elif act == "relu2":
        z = jnp.square(jnp.maximum(z, 0.0))
    o_ref[...] = z.astype(o_ref.dtype)


def _act_proj(h, w, col_off, width, act, name, tm=2048, tn=1024):
    s, d = h.shape
    return pl.pallas_call(
        functools.partial(_act_kernel, act=act),
        grid=(width // tn, s // tm),
        in_specs=[pl.BlockSpec((tm, d), lambda j, i: (i, 0)),
                  pl.BlockSpec((d, tn), lambda j, i: (0, col_off // tn + j))],
        out_specs=pl.BlockSpec((tm, tn), lambda j, i: (i, j)),
        out_shape=jax.ShapeDtypeStruct((s, width), BF16),
        scratch_shapes=[pltpu.VMEM((d, tn), BF16)],
        compiler_params=_params(("arbitrary", "arbitrary"), 58),
        name=name,
    )(h, w)


def _conv_kernel(u_ref, up_ref, un_ref, wdw_ref, lng_ref, lnb_ref, wp_ref, gate_ref,
                 o_ref, buf_ref, cv_ref, wpb_ref):
    i = pl.program_id(0)
    tm = u_ref.shape[0]
    n_slabs = CONV_WIDTH // LANES
    half = CONV_ROWS // 2
    base = CONV_HALO - CONV_KERNEL // 2
    _cast_once(i, wp_ref, wpb_ref)
    prev_ok = (i > 0).astype(F32)
    next_ok = (i < pl.num_programs(0) - 1).astype(F32)
    for sl in range(n_slabs):
        lanes = slice(sl * LANES, (sl + 1) * LANES)
        buf_ref[sl, 0:CONV_HALO, :] = up_ref[:, lanes] * prev_ok
        buf_ref[sl, CONV_HALO:CONV_HALO + tm, :] = u_ref[:, lanes]
        buf_ref[sl, CONV_HALO + tm:, :] = un_ref[:, lanes] * next_ok

    for sl in range(n_slabs):
        def body(r, carry, sl=sl):
            r0 = r * CONV_ROWS
            for par in range(2):
                acc = jnp.zeros((half, LANES), F32)
                for k in range(CONV_KERNEL):
                    tap = buf_ref[sl, pl.ds(r0 + (par + base + k), half, stride=2), :]
                    acc = acc + tap * wdw_ref[k:k + 1, sl * LANES:(sl + 1) * LANES]
                cv_ref[sl, pl.ds(r0 + par, half, stride=2), :] = acc
            return carry
        lax.fori_loop(0, tm // CONV_ROWS, body, 0)

    cv = jnp.concatenate([cv_ref[sl] for sl in range(n_slabs)], axis=1)
    mu = jnp.mean(cv, axis=-1, keepdims=True)
    xc = cv - mu
    var = jnp.mean(xc * xc, axis=-1, keepdims=True)
    y = xc * lax.rsqrt(var + EPS) * lng_ref[...] + lnb_ref[...]
    y = y * _sigmoid(y)
    yc = jnp.dot(y.astype(BF16), wpb_ref[...], preferred_element_type=F32)
    o_ref[...] = gate_ref[...].astype(F32) * yc


def _conv_branch(u, w_dw, ln_g, ln_b, w_proj, gates, tm=512):
    s = u.shape[0]
    hb = tm // CONV_HALO
    n_hblk = s // CONV_HALO
    return pl.pallas_call(
        _conv_kernel,
        grid=(s // tm,),
        in_specs=[pl.BlockSpec((tm, CONV_WIDTH), lambda i: (i, 0)),
                  pl.BlockSpec((CONV_HALO, CONV_WIDTH), lambda i: (jnp.maximum(i * hb - 1, 0), 0)),
                  pl.BlockSpec((CONV_HALO, CONV_WIDTH), lambda i: (jnp.minimum((i + 1) * hb, n_hblk - 1), 0)),
                  pl.BlockSpec((CONV_KERNEL, CONV_WIDTH), lambda i: (0, 0)),
                  pl.BlockSpec((1, CONV_WIDTH), lambda i: (0, 0)),
                  pl.BlockSpec((1, CONV_WIDTH), lambda i: (0, 0)),
                  pl.BlockSpec((CONV_WIDTH, D_MODEL), lambda i: (0, 0), pipeline_mode=pl.Buffered(1)),
                  pl.BlockSpec((tm, D_MODEL), lambda i: (i, 0))],
        out_specs=pl.BlockSpec((tm, D_MODEL), lambda i: (i, 0)),
        out_shape=jax.ShapeDtypeStruct((s, D_MODEL), F32),
        scratch_shapes=[pltpu.VMEM((CONV_WIDTH // LANES, tm + 2 * CONV_HALO, LANES), F32),
                        pltpu.VMEM((CONV_WIDTH // LANES, tm, LANES), F32),
                        pltpu.VMEM((CONV_WIDTH, D_MODEL), BF16)],
        compiler_params=_params(("arbitrary",), 48),
        name="conv_branch",
    )(u, u, u, w_dw, ln_g.reshape(1, -1), ln_b.reshape(1, -1), w_proj, gates)


def _stage_queries(qt_ref, qs_ref, tq, shift):
    for hh in range(GROUP):
        qs_ref[0:HEAD_DIM, hh * tq:(hh + 1) * tq] = qt_ref[hh * HEAD_DIM:(hh + 1) * HEAD_DIM, :]
    row = lax.broadcasted_iota(jnp.int32, (K_COLS - HEAD_DIM, qs_ref.shape[1]), 0)
    qs_ref[HEAD_DIM:K_COLS, :] = jnp.where(row == 0, -shift, 0.0).astype(qs_ref.dtype)


def _store_attention_out(o_t, o_ref, tq):
    for hh in range(GROUP):
        o_ref[:, hh * HEAD_DIM:(hh + 1) * HEAD_DIM] = o_t[:, hh * tq:(hh + 1) * tq].T.astype(o_ref.dtype)


def _chunk_start(c, tk):
    return c * tk if isinstance(c, int) else pl.multiple_of(c * tk, tk)


def _attn_online_kernel(b_ref, qt_ref, k_ref, vt_ref, o_ref, qs_ref, acc_ref, s0_ref, s1_ref, *, tq, tk):
    m_cols = GROUP * tq
    n_chunks = k_ref.shape[0] // tk
    assert n_chunks % 2 == 0 and n_chunks >= 2
    _stage_queries(qt_ref, qs_ref, tq, b_ref[0, 0])
    acc_ref[...] = jnp.zeros(acc_ref.shape, F32)

    def scores(c, dst_ref):
        s = jnp.dot(k_ref[pl.ds(_chunk_start(c, tk), tk), :], qs_ref[...], preferred_element_type=F32)
        dst_ref[...] = s
        return jnp.max(s, axis=0, keepdims=True)

    def update(c, src_ref, col_max, m_prev):
        m_new = jnp.maximum(m_prev, col_max)
        alpha = jnp.exp2(m_prev - m_new)
        p = jnp.exp2(src_ref[...] - m_new).astype(BF16)
        vtc = vt_ref[:, pl.ds(_chunk_start(c, tk), tk)]
        acc_ref[...] = alpha * acc_ref[...] + jnp.dot(vtc, p, preferred_element_type=F32)
        return m_new

    def pair(j, carry):
        m_run, cm0 = carry
        c = 2 * j
        cm1 = scores(c + 1, s1_ref)
        m_run = update(c, s0_ref, cm0, m_run)
        cm2 = scores(c + 2, s0_ref)
        m_run = update(c + 1, s1_ref, cm1, m_run)
        return m_run, cm2

    cm0 = scores(0, s0_ref)
    carry = (jnp.full((1, m_cols), -1e30, F32), cm0)
    m_run, cm0 = lax.fori_loop(0, n_chunks // 2 - 1, pair, carry)
    cm1 = scores(n_chunks - 1, s1_ref)
    m_run = update(n_chunks - 2, s0_ref, cm0, m_run)
    update(n_chunks - 1, s1_ref, cm1, m_run)
    _store_attention_out(acc_ref[0:HEAD_DIM, :] / acc_ref[HEAD_DIM:HEAD_DIM + 1, :], o_ref, tq)


def _attn_bounded_kernel(b_ref, qt_ref, k_ref, vt_ref, w2_ref, o_ref, w2b_ref,
                         qs_ref, acc_ref, p0_ref, p1_ref, *, tq, tk):
    m_cols = GROUP * tq
    n_chunks = k_ref.shape[0] // tk
    assert n_chunks % 2 == 0 and n_chunks >= 2
    w2b_ref[...] = w2_ref[...].astype(w2b_ref.dtype)
    _stage_queries(qt_ref, qs_ref, tq, b_ref[0, 0])
    acc_ref[...] = jnp.zeros(acc_ref.shape, F32)

    def probs(c, dst_ref, l_run):
        s = jnp.dot(k_ref[pl.ds(_chunk_start(c, tk), tk), :], qs_ref[...], preferred_element_type=F32)
        p = jnp.exp2(s)
        dst_ref[...] = p.astype(BF16)
        return l_run + jnp.sum(p, axis=0, keepdims=True)

    def accumulate(c, src_ref):
        vtc = vt_ref[0:HEAD_DIM, pl.ds(_chunk_start(c, tk), tk)]
        acc_ref[...] += jnp.dot(vtc, src_ref[...], preferred_element_type=F32)

    bufs = (p0_ref, p1_ref)
    unroll = BOUNDED_UNROLL
    assert n_chunks % unroll == 0

    def body(j, l_run):
        c = unroll * j
        for e in range(unroll):
            l_run = probs(c + e + 1, bufs[(e + 1) % 2], l_run)
            accumulate(c + e, bufs[e % 2])
        return l_run

    l_run = probs(0, p0_ref, jnp.zeros((1, m_cols), F32))
    l_run = lax.fori_loop(0, n_chunks // unroll - 1, body, l_run)
    for c in range(n_chunks - unroll, n_chunks):
        if c + 1 < n_chunks:
            l_run = probs(c + 1, bufs[(c + 1) % 2], l_run)
        accumulate(c, bufs[c % 2])
    _store_attention_out(acc_ref[...] / l_run, o_ref, tq)


def _attention(qt, k, vt, q_gain, k_gain, w_ff2, tq_online=256, tq_bounded=1024, tk=512):
    s = k.shape[0]
    gw = GROUP * HEAD_DIM
    smem = pl.BlockSpec(memory_space=pltpu.SMEM)
    attn_specs = lambda tq: [smem,
                             pl.BlockSpec((gw, tq), lambda g, i: (g, i)),
                             pl.BlockSpec((s, K_COLS), lambda g, i: (0, g)),
                             pl.BlockSpec((V_ROWS, s), lambda g, i: (g, 0))]
    out_spec = lambda tq: pl.BlockSpec((tq, gw), lambda g, i: (i, g))
    o_shape = jax.ShapeDtypeStruct((s, Q_W), BF16)

    def online(b):
        tq = tq_online
        m_cols = GROUP * tq
        o = pl.pallas_call(
            functools.partial(_attn_online_kernel, tq=tq, tk=tk),
            grid=(N_KV_HEADS, s // tq),
            in_specs=attn_specs(tq),
            out_specs=out_spec(tq),
            out_shape=o_shape,
            scratch_shapes=[pltpu.VMEM((K_COLS, m_cols), BF16),
                            pltpu.VMEM((V_ROWS, m_cols), F32),
                            pltpu.VMEM((tk, m_cols), F32),
                            pltpu.VMEM((tk, m_cols), F32)],
            compiler_params=_params(("arbitrary", "arbitrary"), 48),
            name="gqa_online")(b, qt, k, vt)
        return o, w_ff2.astype(BF16)

    def bounded(b):
        tq = tq_bounded
        m_cols = GROUP * tq
        n_i = s // tq
        w2_rows = w_ff2.shape[0] // (N_KV_HEADS * n_i)
        w2_spec = pl.BlockSpec((w2_rows, w_ff2.shape[1]), lambda g, i: (g * n_i + i, 0))
        return pl.pallas_call(
            functools.partial(_attn_bounded_kernel, tq=tq, tk=tk),
            grid=(N_KV_HEADS, n_i),
            in_specs=attn_specs(tq) + [w2_spec],
            out_specs=[out_spec(tq), w2_spec],
            out_shape=[o_shape, jax.ShapeDtypeStruct(w_ff2.shape, BF16)],
            scratch_shapes=[pltpu.VMEM((K_COLS, m_cols), BF16),
                            pltpu.VMEM((HEAD_DIM, m_cols), F32),
                            pltpu.VMEM((tk, m_cols), BF16),
                            pltpu.VMEM((tk, m_cols), BF16)],
            compiler_params=_params(("arbitrary", "arbitrary"), 52),
            name="gqa_bounded")(b, qt, k, vt, w_ff2)

    bound = (HEAD_DIM * Q_SCALE * SCORE_BOUND_SLACK
             * jnp.max(jnp.abs(q_gain)) * jnp.max(jnp.abs(k_gain))).astype(F32)
    return lax.cond(bound <= MAX_FIXED_SHIFT, bounded, online, bound.reshape(1, 1))


def _merge_kernel(o_ref, w_ref, mc_ref, ga_ref, out_ref, wb_ref):
    _cast_once(pl.program_id(1), w_ref, wb_ref)
    ya = jnp.dot(o_ref[...], wb_ref[...], preferred_element_type=F32)
    out_ref[...] = (mc_ref[...] + ga_ref[...].astype(F32) * ya).astype(out_ref.dtype)


def _merge(o, w_ap, m_c, gates, tm=1024, tn=1024):
    s, d = o.shape
    ga_off = D_MODEL // tn
    return pl.pallas_call(
        _merge_kernel,
        grid=(D_MODEL // tn, s // tm),
        in_specs=[pl.BlockSpec((tm, d), lambda j, i: (i, 0)),
                  pl.BlockSpec((d, tn), lambda j, i: (0, j)),
                  pl.BlockSpec((tm, tn), lambda j, i: (i, j)),
                  pl.BlockSpec((tm, tn), lambda j, i: (i, ga_off + j))],
        out_specs=pl.BlockSpec((tm, tn), lambda j, i: (i, j)),
        out_shape=jax.ShapeDtypeStruct((s, D_MODEL), BF16),
        scratch_shapes=[pltpu.VMEM((d, tn), BF16)],
        compiler_params=_params(("arbitrary", "arbitrary"), 56),
        name="merge_attn_proj",
    )(o, w_ap, m_c, gates)


def _out_kernel(a_ref, w_ref, x_ref, g_ref, x1_ref, h_ref, wb_ref):
    _cast_once(pl.program_id(0), w_ref, wb_ref)
    x1 = x_ref[...] + jnp.dot(a_ref[...], wb_ref[...], preferred_element_type=F32)
    x1_ref[...] = x1
    h_ref[...] = _rms_rows(x1, g_ref[...]).astype(h_ref.dtype)


def _out_proj(a, w, x, g, tm=512):
    s, d = x.shape
    return pl.pallas_call(
        _out_kernel,
        grid=(s // tm,),
        in_specs=[pl.BlockSpec((tm, d), lambda i: (i, 0)),
                  pl.BlockSpec((d, d), lambda i: (0, 0), pipeline_mode=pl.Buffered(1)),
                  pl.BlockSpec((tm, d), lambda i: (i, 0)),
                  pl.BlockSpec((1, d), lambda i: (0, 0))],
        out_specs=[pl.BlockSpec((tm, d), lambda i: (i, 0)),
                   pl.BlockSpec((tm, d), lambda i: (i, 0))],
        out_shape=[jax.ShapeDtypeStruct((s, d), F32),
                   jax.ShapeDtypeStruct((s, d), BF16)],
        scratch_shapes=[pltpu.VMEM((d, d), BF16)],
        compiler_params=_params(("arbitrary",), 56),
        name="out_proj_residual",
    )(a, w, x, g.reshape(1, d))


def _ffn2_kernel(a_ref, w_ref, x_ref, o_ref):
    @pl.when(pl.program_id(2) == 0)
    def _():
        o_ref[...] = x_ref[...]

    o_ref[...] += jnp.dot(a_ref[...], w_ref[...], preferred_element_type=F32)


def _ffn2(a, w, x, tm=1024, tn=1024, tk=4096):
    s, kdim = a.shape
    d = x.shape[1]
    return pl.pallas_call(
        _ffn2_kernel,
        grid=(s // tm, d // tn, kdim // tk),
        in_specs=[pl.BlockSpec((tm, tk), lambda i, j, k: (i, k)),
                  pl.BlockSpec((tk, tn), lambda i, j, k: (k, j)),
                  pl.BlockSpec((tm, tn), lambda i, j, k: (i, j))],
        out_specs=pl.BlockSpec((tm, tn), lambda i, j, k: (i, j)),
        out_shape=jax.ShapeDtypeStruct((s, d), F32),
        compiler_params=_params(("arbitrary", "arbitrary", "arbitrary"), 58),
        name="ffn_down_residual",
    )(a, w, x)


def _ple_kernel(x_ref, p_ref, gp_ref, wg_ref, wp_ref, gf_ref, o_ref, wgb_ref, wpb_ref):
    _cast_once(pl.program_id(0), wg_ref, wgb_ref)
    _cast_once(pl.program_id(0), wp_ref, wpb_ref)
    x = x_ref[...]
    h = _rms_rows(x, gp_ref[...]).astype(BF16)
    gate = _sigmoid(jnp.dot(h, wgb_ref[...], preferred_element_type=F32))
    pp = jnp.dot(p_ref[...].astype(BF16), wpb_ref[...], preferred_element_type=F32)
    x3 = x + gate * pp
    o_ref[...] = _rms_rows(x3, gf_ref[...])


def _ple_final(x, p, g_ple, w_gate, w_proj, g_final, tm=512):
    s, d = x.shape
    return pl.pallas_call(
        _ple_kernel,
        grid=(s // tm,),
        in_specs=[pl.BlockSpec((tm, d), lambda i: (i, 0)),
                  pl.BlockSpec((tm, PLE_DIM), lambda i: (i, 0)),
                  pl.BlockSpec((1, d), lambda i: (0, 0)),
                  pl.BlockSpec((d, d), lambda i: (0, 0), pipeline_mode=pl.Buffered(1)),
                  pl.BlockSpec((PLE_DIM, d), lambda i: (0, 0), pipeline_mode=pl.Buffered(1)),
                  pl.BlockSpec((1, d), lambda i: (0, 0))],
        out_specs=pl.BlockSpec((tm, d), lambda i: (i, 0)),
        out_shape=jax.ShapeDtypeStruct((s, d), F32),
        scratch_shapes=[pltpu.VMEM((d, d), BF16), pltpu.VMEM((PLE_DIM, d), BF16)],
        compiler_params=_params(("arbitrary",), 56),
        name="ple_final_norm",
    )(x, p, g_ple.reshape(1, d), w_gate, w_proj, g_final.reshape(1, d))


def _rope_tables():
    n_rows = SEQ // GRID_W
    inv_freq = ROPE_THETA ** (-jnp.arange(0, AXIS_DIM, 2, dtype=F32) / AXIS_DIM)
    ang_row = inv_freq[:, None] * jnp.arange(n_rows, dtype=jnp.int32).astype(F32)[None, :]
    ang_col = inv_freq[:, None] * jnp.arange(GRID_W, dtype=jnp.int32).astype(F32)[None, :]
    nf = inv_freq.shape[0]

    def over_t(row_tab, col_tab):
        r = jnp.broadcast_to(row_tab[:, :, None], (nf, n_rows, GRID_W)).reshape(nf, SEQ)
        c = jnp.broadcast_to(col_tab[:, None, :], (nf, n_rows, GRID_W)).reshape(nf, SEQ)
        return r, c

    cr, cc = over_t(jnp.cos(ang_row), jnp.cos(ang_col))
    sr, sc = over_t(jnp.sin(ang_row), jnp.sin(ang_col))
    cos_t = jnp.concatenate([cr, cr, cc, cc], axis=0)
    sin_t = jnp.concatenate([-sr, sr, -sc, sc], axis=0)
    return cos_t, sin_t


def kernel(x, p, norm_mix, w_in, w_dw, conv_ln_g, conv_ln_b, w_conv_proj, q_norm, k_norm,
           w_attn_proj, w_out, norm_ffn, w_ff1, w_ff2, norm_ple, w_ple_gate, w_ple_proj, norm_final):
    depth = w_in.shape[0]
    assert depth == 1, "the final norm is fused into the last layer's kernel"
    tabs_t = _rope_tables()
    xs = x[0]
    for li in range(depth):
        w_i = w_in[li]
        h, h_t = _rmsnorm_cast(xs, norm_mix[li])
        u = _glu_proj(h, w_i)
        qt = _qk_proj(w_i, h_t, q_norm[li], tabs_t, OFF_Q, Q_W, Q_SCALE, False, "qt_proj")
        k = _qk_proj(w_i, h_t, k_norm[li], tabs_t, OFF_K, KV_W, 1.0, True, "k_proj")
        vt = _vt_proj(w_i, h_t)
        gates = _act_proj(h, w_i, OFF_G, 2 * D_MODEL, "sigmoid", "gate_proj")
        m_c = _conv_branch(u, w_dw[li], conv_ln_g[li], conv_ln_b[li], w_conv_proj[li], gates)
        o, w_ff2_b = _attention(qt, k, vt, q_norm[li], k_norm[li], w_ff2[li])
        merged = _merge(o, w_attn_proj[li], m_c, gates)
        x1, h2 = _out_proj(merged, w_out[li], xs, norm_ffn[li])
        a = _act_proj(h2, w_ff1[li], 0, D_FF, "relu2", "ffn_up")
        x2 = _ffn2(a, w_ff2_b, x1)
        xs = _ple_final(x2, p[li, 0], norm_ple[li], w_ple_gate[li], w_ple_proj[li], norm_final)
    return xs[None]
```

```python
import functools
import math

import jax
import jax.numpy as jnp
from jax import lax
from jax.experimental import pallas as pl
from jax.experimental.pallas import tpu as pltpu

D_MODEL = 2048
SEQ = 8192
N_HEADS = 16
N_KV_HEADS = 4
HEAD_DIM = 128
GROUP = N_HEADS // N_KV_HEADS
ROPE_THETA = 10000.0
AXIS_DIM = HEAD_DIM // 2
GRID_W = 64
CONV_WIDTH = D_MODEL // 2
CONV_KERNEL = 31
CONV_HALO = 16
CONV_ROWS = 256
D_FF = 4 * D_MODEL
PLE_DIM = 256
EPS = 1e-6
Q_W = N_HEADS * HEAD_DIM
KV_W = N_KV_HEADS * HEAD_DIM
LANES = 128
SUBLANES = 8
BF16_SUBLANES = 16
V_ROWS = HEAD_DIM + BF16_SUBLANES
K_COLS = 2 * HEAD_DIM

OFF_CA = 0
OFF_CB = CONV_WIDTH
OFF_Q = 2 * CONV_WIDTH
OFF_K = OFF_Q + Q_W
OFF_V = OFF_K + KV_W
OFF_G = OFF_V + KV_W

Q_SCALE = (HEAD_DIM ** -0.5) * math.log2(math.e)
SCORE_BOUND_SLACK = 1.02
MAX_FIXED_SHIFT = 60.0
BOUNDED_UNROLL = 4

BF16 = jnp.bfloat16
F32 = jnp.float32
MIB = 1024 * 1024


def _params(sem, vmem_mib):
    return pltpu.CompilerParams(dimension_semantics=sem, vmem_limit_bytes=vmem_mib * MIB)


def _sigmoid(v):
    return 1.0 / (1.0 + jnp.exp(-v))


def _rms_rows(v, g):
    ms = jnp.mean(v * v, axis=-1, keepdims=True)
    return v * lax.rsqrt(ms + EPS) * g


def _cast_once(step, w_ref, wb_ref, transpose=False):
    @pl.when(step == 0)
    def _():
        w = w_ref[...]
        wb_ref[...] = (w.T if transpose else w).astype(wb_ref.dtype)


def _rmsnorm_kernel(x_ref, g_ref, o_ref, ot_ref):
    y = _rms_rows(x_ref[...], g_ref[...])
    o_ref[...] = y.astype(o_ref.dtype)
    for cb in range(0, y.shape[1], LANES):
        ot_ref[cb:cb + LANES, :] = y[:, cb:cb + LANES].T.astype(ot_ref.dtype)


def _rmsnorm_cast(x, g, tm=512):
    s, d = x.shape
    return pl.pallas_call(
        _rmsnorm_kernel,
        grid=(s // tm,),
        in_specs=[pl.BlockSpec((tm, d), lambda i: (i, 0)),
                  pl.BlockSpec((1, d), lambda i: (0, 0))],
        out_specs=[pl.BlockSpec((tm, d), lambda i: (i, 0)),
                   pl.BlockSpec((d, tm), lambda i: (0, i))],
        out_shape=[jax.ShapeDtypeStruct((s, d), BF16),
                   jax.ShapeDtypeStruct((d, s), BF16)],
        compiler_params=_params(("arbitrary",), 32),
        name="rmsnorm_cast",
    )(x, g.reshape(1, d))


def _glu_kernel(h_ref, wa_ref, wb_ref, o_ref, wab_ref, wbb_ref):
    _cast_once(pl.program_id(1), wa_ref, wab_ref)
    _cast_once(pl.program_id(1), wb_ref, wbb_ref)
    h = h_ref[...]
    a = jnp.dot(h, wab_ref[...], preferred_element_type=F32)
    b = jnp.dot(h, wbb_ref[...], preferred_element_type=F32)
    o_ref[...] = a * _sigmoid(b)


def _glu_proj(h, w, tm=1024, tn=512):
    s, d = h.shape
    nb = CONV_WIDTH // tn
    return pl.pallas_call(
        _glu_kernel,
        grid=(nb, s // tm),
        in_specs=[pl.BlockSpec((tm, d), lambda j, i: (i, 0)),
                  pl.BlockSpec((d, tn), lambda j, i: (0, OFF_CA // tn + j)),
                  pl.BlockSpec((d, tn), lambda j, i: (0, OFF_CB // tn + j))],
        out_specs=pl.BlockSpec((tm, tn), lambda j, i: (i, j)),
        out_shape=jax.ShapeDtypeStruct((s, CONV_WIDTH), F32),
        scratch_shapes=[pltpu.VMEM((d, tn), BF16), pltpu.VMEM((d, tn), BF16)],
        compiler_params=_params(("arbitrary", "arbitrary"), 58),
        name="glu_proj",
    )(h, w, w)


def _swap_axis_halves(y):
    q = AXIS_DIM // 2
    return jnp.concatenate([y[q:2 * q], y[0:q], y[3 * q:4 * q], y[2 * q:3 * q]], axis=0)


def _norm_rope_head(zh, g, cos, sin):
    ms = jnp.mean(zh * zh, axis=0, keepdims=True)
    y = zh * lax.rsqrt(ms + EPS) * g
    return y * cos + _swap_axis_halves(y) * sin


def _qt_kernel(w_ref, ht_ref, g_ref, cos_ref, sin_ref, o_ref, wt_ref):
    _cast_once(pl.program_id(1), w_ref, wt_ref, transpose=True)
    zt = jnp.dot(wt_ref[...], ht_ref[...], preferred_element_type=F32)
    g = jnp.broadcast_to(g_ref[...], (HEAD_DIM, zt.shape[1]))
    cos = cos_ref[...]
    sin = sin_ref[...]
    for hh in range(zt.shape[0] // HEAD_DIM):
        rows = slice(hh * HEAD_DIM, (hh + 1) * HEAD_DIM)
        o_ref[rows, :] = (_norm_rope_head(zt[rows, :], g, cos, sin) * Q_SCALE).astype(o_ref.dtype)


def _qt_proj(w, h_t, gain, tabs_t, tm=1024, tn=1024):
    d, s = h_t.shape
    cos_t, sin_t = tabs_t
    tab_spec = pl.BlockSpec((HEAD_DIM, tm), lambda j, i: (0, i))
    return pl.pallas_call(
        _qt_kernel,
        grid=(Q_W // tn, s // tm),
        in_specs=[pl.BlockSpec((d, tn), lambda j, i: (0, OFF_Q // tn + j)),
                  pl.BlockSpec((d, tm), lambda j, i: (0, i)),
                  pl.BlockSpec((HEAD_DIM, 1), lambda j, i: (0, 0)),
                  tab_spec, tab_spec],
        out_specs=pl.BlockSpec((tn, tm), lambda j, i: (j, i)),
        out_shape=jax.ShapeDtypeStruct((Q_W, s), BF16),
        scratch_shapes=[pltpu.VMEM((tn, d), BF16)],
        compiler_params=_params(("arbitrary", "arbitrary"), 48),
        name="qt_proj",
    )(w, h_t, gain.reshape(HEAD_DIM, 1), cos_t, sin_t)


def _kv_kernel(w_ref, ht_ref, g_ref, cos_ref, sin_ref, k_ref, vt_ref, wt_ref):
    _cast_once(pl.program_id(0), w_ref, wt_ref, transpose=True)
    zt = jnp.dot(wt_ref[...], ht_ref[...], preferred_element_type=F32)
    tm = zt.shape[1]
    g = jnp.broadcast_to(g_ref[...], (HEAD_DIM, tm))
    cos = cos_ref[...]
    sin = sin_ref[...]
    lane = lax.broadcasted_iota(jnp.int32, (tm, K_COLS - HEAD_DIM), 1)
    ones_col = jnp.where(lane == 0, 1.0, 0.0).astype(k_ref.dtype)
    row = lax.broadcasted_iota(jnp.int32, (V_ROWS - HEAD_DIM, tm), 0)
    ones_row = jnp.where(row == 0, 1.0, 0.0).astype(vt_ref.dtype)
    for hh in range(N_KV_HEADS):
        r = _norm_rope_head(zt[hh * HEAD_DIM:(hh + 1) * HEAD_DIM, :], g, cos, sin)
        k_ref[:, hh * K_COLS:hh * K_COLS + HEAD_DIM] = r.T.astype(k_ref.dtype)
        k_ref[:, hh * K_COLS + HEAD_DIM:(hh + 1) * K_COLS] = ones_col
        v_rows = slice(KV_W + hh * HEAD_DIM, KV_W + (hh + 1) * HEAD_DIM)
        vt_ref[hh * V_ROWS:hh * V_ROWS + HEAD_DIM, :] = zt[v_rows, :].astype(vt_ref.dtype)
        vt_ref[hh * V_ROWS + HEAD_DIM:(hh + 1) * V_ROWS, :] = ones_row


def _kv_proj(w, h_t, gain, tabs_t, tm=1024):
    d, s = h_t.shape
    assert OFF_V == OFF_K + KV_W and OFF_K % (2 * KV_W) == 0
    cos_t, sin_t = tabs_t
    tab_spec = pl.BlockSpec((HEAD_DIM, tm), lambda i: (0, i))
    return pl.pallas_call(
        _kv_kernel,
        grid=(s // tm,),
        in_specs=[pl.BlockSpec((d, 2 * KV_W), lambda i: (0, OFF_K // (2 * KV_W))),
                  pl.BlockSpec((d, tm), lambda i: (0, i)),
                  pl.BlockSpec((HEAD_DIM, 1), lambda i: (0, 0)),
                  tab_spec, tab_spec],
        out_specs=[pl.BlockSpec((tm, N_KV_HEADS * K_COLS), lambda i: (i, 0)),
                   pl.BlockSpec((N_KV_HEADS * V_ROWS, tm), lambda i: (0, i))],
        out_shape=[jax.ShapeDtypeStruct((s, N_KV_HEADS * K_COLS), BF16),
                   jax.ShapeDtypeStruct((N_KV_HEADS * V_ROWS, s), BF16)],
        scratch_shapes=[pltpu.VMEM((2 * KV_W, d), BF16)],
        compiler_params=_params(("arbitrary",), 48),
        name="kv_proj",
    )(w, h_t, gain.reshape(HEAD_DIM, 1), cos_t, sin_t)


def _act_kernel(h_ref, w_ref, o_ref, wb_ref, *, act):
    _cast_once(pl.program_id(1), w_ref, wb_ref)
    z = jnp.dot(h_ref[...], wb_ref[...], preferred_element_type=F32)
    if act == "sigmoid":
        z = _sigmoid(z)
    elif act == "relu2":
        z = jnp.square(jnp.maximum(z, 0.0))
    o_ref[...] = z.astype(o_ref.dtype)


def _act_proj(h, w, col_off, width, act, name, tm=2048, tn=1024):
    s, d = h.shape
    return pl.pallas_call(
        functools.partial(_act_kernel, act=act),
        grid=(width // tn, s // tm),
        in_specs=[pl.BlockSpec((tm, d), lambda j, i: (i, 0)),
                  pl.BlockSpec((d, tn), lambda j, i: (0, col_off // tn + j))],
        out_specs=pl.BlockSpec((tm, tn), lambda j, i: (i, j)),
        out_shape=jax.ShapeDtypeStruct((s, width), BF16),
        scratch_shapes=[pltpu.VMEM((d, tn), BF16)],
        compiler_params=_params(("arbitrary", "arbitrary"), 58),
        name=name,
    )(h, w)


def _conv_kernel(u_ref, up_ref, un_ref, wdw_ref, lng_ref, lnb_ref, wp_ref, gate_ref,
                 o_ref, buf_ref, cv_ref, wpb_ref):
    i = pl.program_id(0)
    tm = u_ref.shape[0]
    n_slabs = CONV_WIDTH // LANES
    half = CONV_ROWS // 2
    base = CONV_HALO - CONV_KERNEL // 2
    _cast_once(i, wp_ref, wpb_ref)
    prev_ok = (i > 0).astype(F32)
    next_ok = (i < pl.num_programs(0) - 1).astype(F32)
    for sl in range(n_slabs):
        lanes = slice(sl * LANES, (sl + 1) * LANES)
        buf_ref[sl, 0:CONV_HALO, :] = up_ref[:, lanes] * prev_ok
        buf_ref[sl, CONV_HALO:CONV_HALO + tm, :] = u_ref[:, lanes]
        buf_ref[sl, CONV_HALO + tm:, :] = un_ref[:, lanes] * next_ok

    for sl in range(n_slabs):
        def body(r, carry, sl=sl):
            r0 = r * CONV_ROWS
            for par in range(2):
                acc = jnp.zeros((half, LANES), F32)
                for k in range(CONV_KERNEL):
                    tap = buf_ref[sl, pl.ds(r0 + (par + base + k), half, stride=2), :]
                    acc = acc + tap * wdw_ref[k:k + 1, sl * LANES:(sl + 1) * LANES]
                cv_ref[sl, pl.ds(r0 + par, half, stride=2), :] = acc
            return carry
        lax.fori_loop(0, tm // CONV_ROWS, body, 0)

    cv = jnp.concatenate([cv_ref[sl] for sl in range(n_slabs)], axis=1)
    mu = jnp.mean(cv, axis=-1, keepdims=True)
    xc = cv - mu
    var = jnp.mean(xc * xc, axis=-1, keepdims=True)
    y = xc * lax.rsqrt(var + EPS) * lng_ref[...] + lnb_ref[...]
    y = y * _sigmoid(y)
    yc = jnp.dot(y.astype(BF16), wpb_ref[...], preferred_element_type=F32)
    o_ref[...] = gate_ref[...].astype(F32) * yc


def _conv_branch(u, w_dw, ln_g, ln_b, w_proj, gates, tm=512):
    s = u.shape[0]
    hb = tm // CONV_HALO
    n_hblk = s // CONV_HALO
    return pl.pallas_call(
        _conv_kernel,
        grid=(s // tm,),
        in_specs=[pl.BlockSpec((tm, CONV_WIDTH), lambda i: (i, 0)),
                  pl.BlockSpec((CONV_HALO, CONV_WIDTH), lambda i: (jnp.maximum(i * hb - 1, 0), 0)),
                  pl.BlockSpec((CONV_HALO, CONV_WIDTH), lambda i: (jnp.minimum((i + 1) * hb, n_hblk - 1), 0)),
                  pl.BlockSpec((CONV_KERNEL, CONV_WIDTH), lambda i: (0, 0)),
                  pl.BlockSpec((1, CONV_WIDTH), lambda i: (0, 0)),
                  pl.BlockSpec((1, CONV_WIDTH), lambda i: (0, 0)),
                  pl.BlockSpec((CONV_WIDTH, D_MODEL), lambda i: (0, 0), pipeline_mode=pl.Buffered(1)),
                  pl.BlockSpec((tm, D_MODEL), lambda i: (i, 0))],
        out_specs=pl.BlockSpec((tm, D_MODEL), lambda i: (i, 0)),
        out_shape=jax.ShapeDtypeStruct((s, D_MODEL), F32),
        scratch_shapes=[pltpu.VMEM((CONV_WIDTH // LANES, tm + 2 * CONV_HALO, LANES), F32),
                        pltpu.VMEM((CONV_WIDTH // LANES, tm, LANES), F32),
                        pltpu.VMEM((CONV_WIDTH, D_MODEL), BF16)],
        compiler_params=_params(("arbitrary",), 48),
        name="conv_branch",
    )(u, u, u, w_dw, ln_g.reshape(1, -1), ln_b.reshape(1, -1), w_proj, gates)


def _stage_queries(qt_ref, qs_ref, tq, shift):
    for hh in range(GROUP):
        qs_ref[0:HEAD_DIM, hh * tq:(hh + 1) * tq] = qt_ref[hh * HEAD_DIM:(hh + 1) * HEAD_DIM, :]
    row = lax.broadcasted_iota(jnp.int32, (K_COLS - HEAD_DIM, qs_ref.shape[1]), 0)
    qs_ref[HEAD_DIM:K_COLS, :] = jnp.where(row == 0, -shift, 0.0).astype(qs_ref.dtype)


def _store_attention_out(o_t, o_ref, tq):
    for hh in range(GROUP):
        o_ref[:, hh * HEAD_DIM:(hh + 1) * HEAD_DIM] = o_t[:, hh * tq:(hh + 1) * tq].T.astype(o_ref.dtype)


def _chunk_start(c, tk):
    return c * tk if isinstance(c, int) else pl.multiple_of(c * tk, tk)


def _attn_online_kernel(b_ref, qt_ref, k_ref, vt_ref, o_ref, qs_ref, acc_ref, s0_ref, s1_ref, *, tq, tk):
    m_cols = GROUP * tq
    n_chunks = k_ref.shape[0] // tk
    assert n_chunks % 2 == 0 and n_chunks >= 2
    _stage_queries(qt_ref, qs_ref, tq, b_ref[0, 0])
    acc_ref[...] = jnp.zeros(acc_ref.shape, F32)

    def scores(c, dst_ref):
        s = jnp.dot(k_ref[pl.ds(_chunk_start(c, tk), tk), :], qs_ref[...], preferred_element_type=F32)
        dst_ref[...] = s
        return jnp.max(s, axis=0, keepdims=True)

    def update(c, src_ref, col_max, m_prev):
        m_new = jnp.maximum(m_prev, col_max)
        alpha = jnp.exp2(m_prev - m_new)
        p = jnp.exp2(src_ref[...] - m_new).astype(BF16)
        vtc = vt_ref[:, pl.ds(_chunk_start(c, tk), tk)]
        acc_ref[...] = alpha * acc_ref[...] + jnp.dot(vtc, p, preferred_element_type=F32)
        return m_new

    def pair(j, carry):
        m_run, cm0 = carry
        c = 2 * j
        cm1 = scores(c + 1, s1_ref)
        m_run = update(c, s0_ref, cm0, m_run)
        cm2 = scores(c + 2, s0_ref)
        m_run = update(c + 1, s1_ref, cm1, m_run)
        return m_run, cm2

    cm0 = scores(0, s0_ref)
    carry = (jnp.full((1, m_cols), -1e30, F32), cm0)
    m_run, cm0 = lax.fori_loop(0, n_chunks // 2 - 1, pair, carry)
    cm1 = scores(n_chunks - 1, s1_ref)
    m_run = update(n_chunks - 2, s0_ref, cm0, m_run)
    update(n_chunks - 1, s1_ref, cm1, m_run)
    _store_attention_out(acc_ref[0:HEAD_DIM, :] / acc_ref[HEAD_DIM:HEAD_DIM + 1, :], o_ref, tq)


def _attn_bounded_kernel(b_ref, qt_ref, k_ref, vt_ref, w2_ref, o_ref, w2b_ref,
                         qs_ref, acc_ref, p0_ref, p1_ref, *, tq, tk):
    m_cols = GROUP * tq
    n_chunks = k_ref.shape[0] // tk
    assert n_chunks % 2 == 0 and n_chunks >= 2
    w2b_ref[...] = w2_ref[...].astype(w2b_ref.dtype)
    _stage_queries(qt_ref, qs_ref, tq, b_ref[0, 0])
    acc_ref[...] = jnp.zeros(acc_ref.shape, F32)

    def probs(c, dst_ref, l_run):
        s = jnp.dot(k_ref[pl.ds(_chunk_start(c, tk), tk), :], qs_ref[...], preferred_element_type=F32)
        p = jnp.exp2(s)
        dst_ref[...] = p.astype(BF16)
        return l_run + jnp.sum(p, axis=0, keepdims=True)

    def accumulate(c, src_ref):
        vtc = vt_ref[0:HEAD_DIM, pl.ds(_chunk_start(c, tk), tk)]
        acc_ref[...] += jnp.dot(vtc, src_ref[...], preferred_element_type=F32)

    bufs = (p0_ref, p1_ref)
    unroll = BOUNDED_UNROLL
    assert n_chunks % unroll == 0

    def body(j, l_run):
        c = unroll * j
        for e in range(unroll):
            l_run = probs(c + e + 1, bufs[(e + 1) % 2], l_run)
            accumulate(c + e, bufs[e % 2])
        return l_run

    l_run = probs(0, p0_ref, jnp.zeros((1, m_cols), F32))
    l_run = lax.fori_loop(0, n_chunks // unroll - 1, body, l_run)
    for c in range(n_chunks - unroll, n_chunks):
        if c + 1 < n_chunks:
            l_run = probs(c + 1, bufs[(c + 1) % 2], l_run)
        accumulate(c, bufs[c % 2])
    _store_attention_out(acc_ref[...] / l_run, o_ref, tq)


def _attention(qt, k, vt, q_gain, k_gain, w_ff2, tq_online=256, tq_bounded=1024, tk=512):
    s = k.shape[0]
    gw = GROUP * HEAD_DIM
    smem = pl.BlockSpec(memory_space=pltpu.SMEM)
    attn_specs = lambda tq: [smem,
                             pl.BlockSpec((gw, tq), lambda g, i: (g, i)),
                             pl.BlockSpec((s, K_COLS), lambda g, i: (0, g)),
                             pl.BlockSpec((V_ROWS, s), lambda g, i: (g, 0))]
    out_spec = lambda tq: pl.BlockSpec((tq, gw), lambda g, i: (i, g))
    o_shape = jax.ShapeDtypeStruct((s, Q_W), BF16)

    def online(b):
        tq = tq_online
        m_cols = GROUP * tq
        o = pl.pallas_call(
            functools.partial(_attn_online_kernel, tq=tq, tk=tk),
            grid=(N_KV_HEADS, s // tq),
            in_specs=attn_specs(tq),
            out_specs=out_spec(tq),
            out_shape=o_shape,
            scratch_shapes=[pltpu.VMEM((K_COLS, m_cols), BF16),
                            pltpu.VMEM((V_ROWS, m_cols), F32),
                            pltpu.VMEM((tk, m_cols), F32),
                            pltpu.VMEM((tk, m_cols), F32)],
            compiler_params=_params(("arbitrary", "arbitrary"), 48),
            name="gqa_online")(b, qt, k, vt)
        return o, w_ff2.astype(BF16)

    def bounded(b):
        tq = tq_bounded
        m_cols = GROUP * tq
        n_i = s // tq
        w2_rows = w_ff2.shape[0] // (N_KV_HEADS * n_i)
        w2_spec = pl.BlockSpec((w2_rows, w_ff2.shape[1]), lambda g, i: (g * n_i + i, 0))
        return pl.pallas_call(
            functools.partial(_attn_bounded_kernel, tq=tq, tk=tk),
            grid=(N_KV_HEADS, n_i),
            in_specs=attn_specs(tq) + [w2_spec],
            out_specs=[out_spec(tq), w2_spec],
            out_shape=[o_shape, jax.ShapeDtypeStruct(w_ff2.shape, BF16)],
            scratch_shapes=[pltpu.VMEM((K_COLS, m_cols), BF16),
                            pltpu.VMEM((HEAD_DIM, m_cols), F32),
                            pltpu.VMEM((tk, m_cols), BF16),
                            pltpu.VMEM((tk, m_cols), BF16)],
            compiler_params=_params(("arbitrary", "arbitrary"), 52),
            name="gqa_bounded")(b, qt, k, vt, w_ff2)

    bound = (HEAD_DIM * Q_SCALE * SCORE_BOUND_SLACK
             * jnp.max(jnp.abs(q_gain)) * jnp.max(jnp.abs(k_gain))).astype(F32)
    return lax.cond(bound <= MAX_FIXED_SHIFT, bounded, online, bound.reshape(1, 1))


def _merge_kernel(o_ref, w_ref, mc_ref, ga_ref, out_ref, wb_ref):
    _cast_once(pl.program_id(1), w_ref, wb_ref)
    ya = jnp.dot(o_ref[...], wb_ref[...], preferred_element_type=F32)
    out_ref[...] = (mc_ref[...] + ga_ref[...].astype(F32) * ya).astype(out_ref.dtype)


def _merge(o, w_ap, m_c, gates, tm=1024, tn=1024):
    s, d = o.shape
    ga_off = D_MODEL // tn
    return pl.pallas_call(
        _merge_kernel,
        grid=(D_MODEL // tn, s // tm),
        in_specs=[pl.BlockSpec((tm, d), lambda j, i: (i, 0)),
                  pl.BlockSpec((d, tn), lambda j, i: (0, j)),
                  pl.BlockSpec((tm, tn), lambda j, i: (i, j)),
                  pl.BlockSpec((tm, tn), lambda j, i: (i, ga_off + j))],
        out_specs=pl.BlockSpec((tm, tn), lambda j, i: (i, j)),
        out_shape=jax.ShapeDtypeStruct((s, D_MODEL), BF16),
        scratch_shapes=[pltpu.VMEM((d, tn), BF16)],
        compiler_params=_params(("arbitrary", "arbitrary"), 56),
        name="merge_attn_proj",
    )(o, w_ap, m_c, gates)


def _out_kernel(a_ref, w_ref, x_ref, g_ref, x1_ref, h_ref, wb_ref):
    _cast_once(pl.program_id(0), w_ref, wb_ref)
    x1 = x_ref[...] + jnp.dot(a_ref[...], wb_ref[...], preferred_element_type=F32)
    x1_ref[...] = x1
    h_ref[...] = _rms_rows(x1, g_ref[...]).astype(h_ref.dtype)


def _out_proj(a, w, x, g, tm=512):
    s, d = x.shape
    return pl.pallas_call(
        _out_kernel,
        grid=(s // tm,),
        in_specs=[pl.BlockSpec((tm, d), lambda i: (i, 0)),
                  pl.BlockSpec((d, d), lambda i: (0, 0), pipeline_mode=pl.Buffered(1)),
                  pl.BlockSpec((tm, d), lambda i: (i, 0)),
                  pl.BlockSpec((1, d), lambda i: (0, 0))],
        out_specs=[pl.BlockSpec((tm, d), lambda i: (i, 0)),
                   pl.BlockSpec((tm, d), lambda i: (i, 0))],
        out_shape=[jax.ShapeDtypeStruct((s, d), F32),
                   jax.ShapeDtypeStruct((s, d), BF16)],
        scratch_shapes=[pltpu.VMEM((d, d), BF16)],
        compiler_params=_params(("arbitrary",), 56),
        name="out_proj_residual",
    )(a, w, x, g.reshape(1, d))


def _ffn2_kernel(a_ref, w_ref, x_ref, o_ref):
    @pl.when(pl.program_id(2) == 0)
    def _():
        o_ref[...] = x_ref[...]

    o_ref[...] += jnp.dot(a_ref[...], w_ref[...], preferred_element_type=F32)


def _ffn2(a, w, x, tm=1024, tn=1024, tk=4096):
    s, kdim = a.shape
    d = x.shape[1]
    return pl.pallas_call(
        _ffn2_kernel,
        grid=(s // tm, d // tn, kdim // tk),
        in_specs=[pl.BlockSpec((tm, tk), lambda i, j, k: (i, k)),
                  pl.BlockSpec((tk, tn), lambda i, j, k: (k, j)),
                  pl.BlockSpec((tm, tn), lambda i, j, k: (i, j))],
        out_specs=pl.BlockSpec((tm, tn), lambda i, j, k: (i, j)),
        out_shape=jax.ShapeDtypeStruct((s, d), F32),
        compiler_params=_params(("arbitrary", "arbitrary", "arbitrary"), 58),
        name="ffn_down_residual",
    )(a, w, x)


def _ple_kernel(x_ref, p_ref, gp_ref, wg_ref, wp_ref, gf_ref, o_ref, wgb_ref, wpb_ref):
    _cast_once(pl.program_id(0), wg_ref, wgb_ref)
    _cast_once(pl.program_id(0), wp_ref, wpb_ref)
    x = x_ref[...]
    h = _rms_rows(x, gp_ref[...]).astype(BF16)
    gate = _sigmoid(jnp.dot(h, wgb_ref[...], preferred_element_type=F32))
    pp = jnp.dot(p_ref[...].astype(BF16), wpb_ref[...], preferred_element_type=F32)
    x3 = x + gate * pp
    o_ref[...] = _rms_rows(x3, gf_ref[...])


def _ple_final(x, p, g_ple, w_gate, w_proj, g_final, tm=512):
    s, d = x.shape
    return pl.pallas_call(
        _ple_kernel,
        grid=(s // tm,),
        in_specs=[pl.BlockSpec((tm, d), lambda i: (i, 0)),
                  pl.BlockSpec((tm, PLE_DIM), lambda i: (i, 0)),
                  pl.BlockSpec((1, d), lambda i: (0, 0)),
                  pl.BlockSpec((d, d), lambda i: (0, 0), pipeline_mode=pl.Buffered(1)),
                  pl.BlockSpec((PLE_DIM, d), lambda i: (0, 0), pipeline_mode=pl.Buffered(1)),
                  pl.BlockSpec((1, d), lambda i: (0, 0))],
        out_specs=pl.BlockSpec((tm, d), lambda i: (i, 0)),
        out_shape=jax.ShapeDtypeStruct((s, d), F32),
        scratch_shapes=[pltpu.VMEM((d, d), BF16), pltpu.VMEM((PLE_DIM, d), BF16)],
        compiler_params=_params(("arbitrary",), 56),
        name="ple_final_norm",
    )(x, p, g_ple.reshape(1, d), w_gate, w_proj, g_final.reshape(1, d))


def _rope_tables():
    n_rows = SEQ // GRID_W
    inv_freq = ROPE_THETA ** (-jnp.arange(0, AXIS_DIM, 2, dtype=F32) / AXIS_DIM)
    ang_row = inv_freq[:, None] * jnp.arange(n_rows, dtype=jnp.int32).astype(F32)[None, :]
    ang_col = inv_freq[:, None] * jnp.arange(GRID_W, dtype=jnp.int32).astype(F32)[None, :]
    nf = inv_freq.shape[0]

    def over_t(row_tab, col_tab):
        r = jnp.broadcast_to(row_tab[:, :, None], (nf, n_rows, GRID_W)).reshape(nf, SEQ)
        c = jnp.broadcast_to(col_tab[:, None, :], (nf, n_rows, GRID_W)).reshape(nf, SEQ)
        return r, c

    cr, cc = over_t(jnp.cos(ang_row), jnp.cos(ang_col))
    sr, sc = over_t(jnp.sin(ang_row), jnp.sin(ang_col))
    cos_t = jnp.concatenate([cr, cr, cc, cc], axis=0)
    sin_t = jnp.concatenate([-sr, sr, -sc, sc], axis=0)
    return cos_t, sin_t


def kernel(x, p, norm_mix, w_in, w_dw, conv_ln_g, conv_ln_b, w_conv_proj, q_norm, k_norm,
           w_attn_proj, w_out, norm_ffn, w_ff1, w_ff2, norm_ple, w_ple_gate, w_ple_proj, norm_final):
    depth = w_in.shape[0]
    assert depth == 1, "the final norm is fused into the last layer's kernel"
    tabs_t = _rope_tables()
    xs = x[0]
    for li in range(depth):
        w_i = w_in[li]
        h, h_t = _rmsnorm_cast(xs, norm_mix[li])
        u = _glu_proj(h, w_i)
        qt = _qt_proj(w_i, h_t, q_norm[li], tabs_t)
        k, vt = _kv_proj(w_i, h_t, k_norm[li], tabs_t)
        gates = _act_proj(h, w_i, OFF_G, 2 * D_MODEL, "sigmoid", "gate_proj")
        m_c = _conv_branch(u, w_dw[li], conv_ln_g[li], conv_ln_b[li], w_conv_proj[li], gates)
        o, w_ff2_b = _attention(qt, k, vt, q_norm[li], k_norm[li], w_ff2[li])
        merged = _merge(o, w_attn_proj[li], m_c, gates)
        x1, h2 = _out_proj(merged, w_out[li], xs, norm_ffn[li])
        a = _act_proj(h2, w_ff1[li], 0, D_FF, "relu2", "ffn_up")
        x2 = _ffn2(a, w_ff2_b, x1)
        xs = _ple_final(x2, p[li, 0], norm_ple[li], w_ple_gate[li], w_ple_proj[li], norm_final)
    return xs[None]
```

```python
import functools
import math

import jax
import jax.numpy as jnp
from jax import lax
from jax.experimental import pallas as pl
from jax.experimental.pallas import tpu as pltpu

D_MODEL = 2048
SEQ = 8192
N_HEADS = 16
N_KV_HEADS = 4
HEAD_DIM = 128
GROUP = N_HEADS // N_KV_HEADS
ROPE_THETA = 10000.0
AXIS_DIM = HEAD_DIM // 2
GRID_W = 64
CONV_WIDTH = D_MODEL // 2
CONV_KERNEL = 31
CONV_HALO = 16
CONV_ROWS = 256
D_FF = 4 * D_MODEL
PLE_DIM = 256
EPS = 1e-6
Q_W = N_HEADS * HEAD_DIM
KV_W = N_KV_HEADS * HEAD_DIM
LANES = 128
SUBLANES = 8
BF16_SUBLANES = 16
V_ROWS = HEAD_DIM + BF16_SUBLANES
K_COLS = 2 * HEAD_DIM

OFF_CA = 0
OFF_CB = CONV_WIDTH
OFF_Q = 2 * CONV_WIDTH
OFF_K = OFF_Q + Q_W
OFF_V = OFF_K + KV_W
OFF_G = OFF_V + KV_W

Q_SCALE = (HEAD_DIM ** -0.5) * math.log2(math.e)
SCORE_BOUND_SLACK = 1.02
MAX_FIXED_SHIFT = 60.0
BOUNDED_UNROLL = 4

BF16 = jnp.bfloat16
F32 = jnp.float32
MIB = 1024 * 1024


_VMEM_LIMIT_MIB = {
    "rmsnorm_cast": 48, "glu_proj": 58, "qt_proj": 48, "kv_proj": 48, "gate_proj": 58,
    "ffn_up": 58, "conv_branch": 48, "gqa_online": 48, "gqa_bounded": 52,
    "merge_attn_proj": 56, "out_proj_residual": 56, "ffn_down_residual": 58,
    "ple_final_norm": 56,
}


def _params(name, grid_rank):
    return pltpu.CompilerParams(dimension_semantics=("arbitrary",) * grid_rank,
                                vmem_limit_bytes=_VMEM_LIMIT_MIB[name] * MIB)


def _sigmoid(v):
    return 1.0 / (1.0 + jnp.exp(-v))


def _rms_rows(v, g):
    ms = jnp.mean(v * v, axis=-1, keepdims=True)
    return v * lax.rsqrt(ms + EPS) * g


def _cast_once(step, w_ref, wb_ref, transpose=False):
    @pl.when(step == 0)
    def _():
        w = w_ref[...]
        wb_ref[...] = (w.T if transpose else w).astype(wb_ref.dtype)


def _rmsnorm_kernel(x_ref, g_ref, o_ref, ot_ref):
    y = _rms_rows(x_ref[...], g_ref[...])
    o_ref[...] = y.astype(o_ref.dtype)
    for cb in range(0, y.shape[1], LANES):
        ot_ref[cb:cb + LANES, :] = y[:, cb:cb + LANES].T.astype(ot_ref.dtype)


def _rmsnorm_cast(x, g, tm=1024):
    s, d = x.shape
    return pl.pallas_call(
        _rmsnorm_kernel,
        grid=(s // tm,),
        in_specs=[pl.BlockSpec((tm, d), lambda i: (i, 0)),
                  pl.BlockSpec((1, d), lambda i: (0, 0))],
        out_specs=[pl.BlockSpec((tm, d), lambda i: (i, 0)),
                   pl.BlockSpec((d, tm), lambda i: (0, i))],
        out_shape=[jax.ShapeDtypeStruct((s, d), BF16),
                   jax.ShapeDtypeStruct((d, s), BF16)],
        compiler_params=_params("rmsnorm_cast", 1),
        name="rmsnorm_cast",
    )(x, g.reshape(1, d))


def _glu_kernel(h_ref, wa_ref, wb_ref, o_ref, wab_ref, wbb_ref):
    _cast_once(pl.program_id(1), wa_ref, wab_ref)
    _cast_once(pl.program_id(1), wb_ref, wbb_ref)
    h = h_ref[...]
    a = jnp.dot(h, wab_ref[...], preferred_element_type=F32)
    b = jnp.dot(h, wbb_ref[...], preferred_element_type=F32)
    o_ref[...] = a * _sigmoid(b)


def _glu_proj(h, w, tm=1024, tn=512):
    s, d = h.shape
    nb = CONV_WIDTH // tn
    return pl.pallas_call(
        _glu_kernel,
        grid=(nb, s // tm),
        in_specs=[pl.BlockSpec((tm, d), lambda j, i: (i, 0)),
                  pl.BlockSpec((d, tn), lambda j, i: (0, OFF_CA // tn + j)),
                  pl.BlockSpec((d, tn), lambda j, i: (0, OFF_CB // tn + j))],
        out_specs=pl.BlockSpec((tm, tn), lambda j, i: (i, j)),
        out_shape=jax.ShapeDtypeStruct((s, CONV_WIDTH), F32),
        scratch_shapes=[pltpu.VMEM((d, tn), BF16), pltpu.VMEM((d, tn), BF16)],
        compiler_params=_params("glu_proj", 2),
        name="glu_proj",
    )(h, w, w)


def _swap_axis_halves(y):
    q = AXIS_DIM // 2
    return jnp.concatenate([y[q:2 * q], y[0:q], y[3 * q:4 * q], y[2 * q:3 * q]], axis=0)


def _norm_rope_head(zh, g, cos, sin):
    ms = jnp.mean(zh * zh, axis=0, keepdims=True)
    y = zh * lax.rsqrt(ms + EPS) * g
    return y * cos + _swap_axis_halves(y) * sin


def _qt_kernel(w_ref, ht_ref, g_ref, cos_ref, sin_ref, o_ref, wt_ref):
    _cast_once(pl.program_id(1), w_ref, wt_ref, transpose=True)
    zt = jnp.dot(wt_ref[...], ht_ref[...], preferred_element_type=F32)
    g = jnp.broadcast_to(g_ref[...], (HEAD_DIM, zt.shape[1]))
    cos = cos_ref[...]
    sin = sin_ref[...]
    for hh in range(zt.shape[0] // HEAD_DIM):
        rows = slice(hh * HEAD_DIM, (hh + 1) * HEAD_DIM)
        o_ref[rows, :] = (_norm_rope_head(zt[rows, :], g, cos, sin) * Q_SCALE).astype(o_ref.dtype)


def _qt_proj(w, h_t, gain, tabs_t, tm=1024, tn=1024):
    d, s = h_t.shape
    cos_t, sin_t = tabs_t
    tab_spec = pl.BlockSpec((HEAD_DIM, tm), lambda j, i: (0, i))
    return pl.pallas_call(
        _qt_kernel,
        grid=(Q_W // tn, s // tm),
        in_specs=[pl.BlockSpec((d, tn), lambda j, i: (0, OFF_Q // tn + j)),
                  pl.BlockSpec((d, tm), lambda j, i: (0, i)),
                  pl.BlockSpec((HEAD_DIM, 1), lambda j, i: (0, 0)),
                  tab_spec, tab_spec],
        out_specs=pl.BlockSpec((tn, tm), lambda j, i: (j, i)),
        out_shape=jax.ShapeDtypeStruct((Q_W, s), BF16),
        scratch_shapes=[pltpu.VMEM((tn, d), BF16)],
        compiler_params=_params("qt_proj", 2),
        name="qt_proj",
    )(w, h_t, gain.reshape(HEAD_DIM, 1), cos_t, sin_t)


def _kv_kernel(w_ref, ht_ref, g_ref, cos_ref, sin_ref, k_ref, vt_ref, wt_ref):
    _cast_once(pl.program_id(0), w_ref, wt_ref, transpose=True)
    zt = jnp.dot(wt_ref[...], ht_ref[...], preferred_element_type=F32)
    tm = zt.shape[1]
    g = jnp.broadcast_to(g_ref[...], (HEAD_DIM, tm))
    cos = cos_ref[...]
    sin = sin_ref[...]
    lane = lax.broadcasted_iota(jnp.int32, (tm, K_COLS - HEAD_DIM), 1)
    ones_col = jnp.where(lane == 0, 1.0, 0.0).astype(k_ref.dtype)
    row = lax.broadcasted_iota(jnp.int32, (V_ROWS - HEAD_DIM, tm), 0)
    ones_row = jnp.where(row == 0, 1.0, 0.0).astype(vt_ref.dtype)
    for hh in range(N_KV_HEADS):
        r = _norm_rope_head(zt[hh * HEAD_DIM:(hh + 1) * HEAD_DIM, :], g, cos, sin)
        k_ref[:, hh * K_COLS:hh * K_COLS + HEAD_DIM] = r.T.astype(k_ref.dtype)
        k_ref[:, hh * K_COLS + HEAD_DIM:(hh + 1) * K_COLS] = ones_col
        v_rows = slice(KV_W + hh * HEAD_DIM, KV_W + (hh + 1) * HEAD_DIM)
        vt_ref[hh * V_ROWS:hh * V_ROWS + HEAD_DIM, :] = zt[v_rows, :].astype(vt_ref.dtype)
        vt_ref[hh * V_ROWS + HEAD_DIM:(hh + 1) * V_ROWS, :] = ones_row


def _kv_proj(w, h_t, gain, tabs_t, tm=1024):
    d, s = h_t.shape
    assert OFF_V == OFF_K + KV_W and OFF_K % (2 * KV_W) == 0
    cos_t, sin_t = tabs_t
    tab_spec = pl.BlockSpec((HEAD_DIM, tm), lambda i: (0, i))
    return pl.pallas_call(
        _kv_kernel,
        grid=(s // tm,),
        in_specs=[pl.BlockSpec((d, 2 * KV_W), lambda i: (0, OFF_K // (2 * KV_W))),
                  pl.BlockSpec((d, tm), lambda i: (0, i)),
                  pl.BlockSpec((HEAD_DIM, 1), lambda i: (0, 0)),
                  tab_spec, tab_spec],
        out_specs=[pl.BlockSpec((tm, N_KV_HEADS * K_COLS), lambda i: (i, 0)),
                   pl.BlockSpec((N_KV_HEADS * V_ROWS, tm), lambda i: (0, i))],
        out_shape=[jax.ShapeDtypeStruct((s, N_KV_HEADS * K_COLS), BF16),
                   jax.ShapeDtypeStruct((N_KV_HEADS * V_ROWS, s), BF16)],
        scratch_shapes=[pltpu.VMEM((2 * KV_W, d), BF16)],
        compiler_params=_params("kv_proj", 1),
        name="kv_proj",
    )(w, h_t, gain.reshape(HEAD_DIM, 1), cos_t, sin_t)


def _act_kernel(h_ref, w_ref, o_ref, wb_ref, *, act):
    _cast_once(pl.program_id(1), w_ref, wb_ref)
    z = jnp.dot(h_ref[...], wb_ref[...], preferred_element_type=F32)
    if act == "sigmoid":
        z = _sigmoid(z)
    elif act == "relu2":
        z = jnp.square(jnp.maximum(z, 0.0))
    o_ref[...] = z.astype(o_ref.dtype)


def _act_proj(h, w, col_off, width, act, name, tm=2048, tn=1024):
    s, d = h.shape
    return pl.pallas_call(
        functools.partial(_act_kernel, act=act),
        grid=(width // tn, s // tm),
        in_specs=[pl.BlockSpec((tm, d), lambda j, i: (i, 0)),
                  pl.BlockSpec((d, tn), lambda j, i: (0, col_off // tn + j))],
        out_specs=pl.BlockSpec((tm, tn), lambda j, i: (i, j)),
        out_shape=jax.ShapeDtypeStruct((s, width), BF16),
        scratch_shapes=[pltpu.VMEM((d, tn), BF16)],
        compiler_params=_params(name, 2),
        name=name,
    )(h, w)


def _conv_kernel(u_ref, up_ref, un_ref, wdw_ref, lng_ref, lnb_ref, wp_ref, gate_ref,
                 o_ref, buf_ref, cv_ref, wpb_ref):
    i = pl.program_id(0)
    tm = u_ref.shape[0]
    n_slabs = CONV_WIDTH // LANES
    half = CONV_ROWS // 2
    base = CONV_HALO - CONV_KERNEL // 2
    _cast_once(i, wp_ref, wpb_ref)
    prev_ok = (i > 0).astype(F32)
    next_ok = (i < pl.num_programs(0) - 1).astype(F32)
    for sl in range(n_slabs):
        lanes = slice(sl * LANES, (sl + 1) * LANES)
        buf_ref[sl, 0:CONV_HALO, :] = up_ref[:, lanes] * prev_ok
        buf_ref[sl, CONV_HALO:CONV_HALO + tm, :] = u_ref[:, lanes]
        buf_ref[sl, CONV_HALO + tm:, :] = un_ref[:, lanes] * next_ok

    for sl in range(n_slabs):
        def body(r, carry, sl=sl):
            r0 = r * CONV_ROWS
            for par in range(2):
                acc = jnp.zeros((half, LANES), F32)
                for k in range(CONV_KERNEL):
                    tap = buf_ref[sl, pl.ds(r0 + (par + base + k), half, stride=2), :]
                    acc = acc + tap * wdw_ref[k:k + 1, sl * LANES:(sl + 1) * LANES]
                cv_ref[sl, pl.ds(r0 + par, half, stride=2), :] = acc
            return carry
        lax.fori_loop(0, tm // CONV_ROWS, body, 0)

    cv = jnp.concatenate([cv_ref[sl] for sl in range(n_slabs)], axis=1)
    mu = jnp.mean(cv, axis=-1, keepdims=True)
    xc = cv - mu
    var = jnp.mean(xc * xc, axis=-1, keepdims=True)
    y = xc * lax.rsqrt(var + EPS) * lng_ref[...] + lnb_ref[...]
    y = y * _sigmoid(y)
    yc = jnp.dot(y.astype(BF16), wpb_ref[...], preferred_element_type=F32)
    o_ref[...] = gate_ref[...].astype(F32) * yc


def _conv_branch(u, w_dw, ln_g, ln_b, w_proj, gates, tm=512):
    s = u.shape[0]
    hb = tm // CONV_HALO
    n_hblk = s // CONV_HALO
    return pl.pallas_call(
        _conv_kernel,
        grid=(s // tm,),
        in_specs=[pl.BlockSpec((tm, CONV_WIDTH), lambda i: (i, 0)),
                  pl.BlockSpec((CONV_HALO, CONV_WIDTH), lambda i: (jnp.maximum(i * hb - 1, 0), 0)),
                  pl.BlockSpec((CONV_HALO, CONV_WIDTH), lambda i: (jnp.minimum((i + 1) * hb, n_hblk - 1), 0)),
                  pl.BlockSpec((CONV_KERNEL, CONV_WIDTH), lambda i: (0, 0)),
                  pl.BlockSpec((1, CONV_WIDTH), lambda i: (0, 0)),
                  pl.BlockSpec((1, CONV_WIDTH), lambda i: (0, 0)),
                  pl.BlockSpec((CONV_WIDTH, D_MODEL), lambda i: (0, 0), pipeline_mode=pl.Buffered(1)),
                  pl.BlockSpec((tm, D_MODEL), lambda i: (i, 0))],
        out_specs=pl.BlockSpec((tm, D_MODEL), lambda i: (i, 0)),
        out_shape=jax.ShapeDtypeStruct((s, D_MODEL), F32),
        scratch_shapes=[pltpu.VMEM((CONV_WIDTH // LANES, tm + 2 * CONV_HALO, LANES), F32),
                        pltpu.VMEM((CONV_WIDTH // LANES, tm, LANES), F32),
                        pltpu.VMEM((CONV_WIDTH, D_MODEL), BF16)],
        compiler_params=_params("conv_branch", 1),
        name="conv_branch",
    )(u, u, u, w_dw, ln_g.reshape(1, -1), ln_b.reshape(1, -1), w_proj, gates)


def _stage_queries(qt_ref, qs_ref, tq, shift):
    for hh in range(GROUP):
        qs_ref[0:HEAD_DIM, hh * tq:(hh + 1) * tq] = qt_ref[hh * HEAD_DIM:(hh + 1) * HEAD_DIM, :]
    row = lax.broadcasted_iota(jnp.int32, (K_COLS - HEAD_DIM, qs_ref.shape[1]), 0)
    qs_ref[HEAD_DIM:K_COLS, :] = jnp.where(row == 0, -shift, 0.0).astype(qs_ref.dtype)


def _store_attention_out(o_t, o_ref, tq):
    for hh in range(GROUP):
        o_ref[:, hh * HEAD_DIM:(hh + 1) * HEAD_DIM] = o_t[:, hh * tq:(hh + 1) * tq].T.astype(o_ref.dtype)


def _chunk_start(c, tk):
    return c * tk if isinstance(c, int) else pl.multiple_of(c * tk, tk)


def _attn_online_kernel(b_ref, qt_ref, k_ref, vt_ref, o_ref, qs_ref, acc_ref, s0_ref, s1_ref, *, tq, tk):
    m_cols = GROUP * tq
    n_chunks = k_ref.shape[0] // tk
    assert n_chunks % 2 == 0 and n_chunks >= 2
    _stage_queries(qt_ref, qs_ref, tq, b_ref[0, 0])
    acc_ref[...] = jnp.zeros(acc_ref.shape, F32)

    def scores(c, dst_ref):
        s = jnp.dot(k_ref[pl.ds(_chunk_start(c, tk), tk), :], qs_ref[...], preferred_element_type=F32)
        dst_ref[...] = s
        return jnp.max(s, axis=0, keepdims=True)

    def update(c, src_ref, col_max, m_prev):
        m_new = jnp.maximum(m_prev, col_max)
        alpha = jnp.exp2(m_prev - m_new)
        p = jnp.exp2(src_ref[...] - m_new).astype(BF16)
        vtc = vt_ref[:, pl.ds(_chunk_start(c, tk), tk)]
        acc_ref[...] = alpha * acc_ref[...] + jnp.dot(vtc, p, preferred_element_type=F32)
        return m_new

    def pair(j, carry):
        m_run, cm0 = carry
        c = 2 * j
        cm1 = scores(c + 1, s1_ref)
        m_run = update(c, s0_ref, cm0, m_run)
        cm2 = scores(c + 2, s0_ref)
        m_run = update(c + 1, s1_ref, cm1, m_run)
        return m_run, cm2

    cm0 = scores(0, s0_ref)
    carry = (jnp.full((1, m_cols), -1e30, F32), cm0)
    m_run, cm0 = lax.fori_loop(0, n_chunks // 2 - 1, pair, carry)
    cm1 = scores(n_chunks - 1, s1_ref)
    m_run = update(n_chunks - 2, s0_ref, cm0, m_run)
    update(n_chunks - 1, s1_ref, cm1, m_run)
    _store_attention_out(acc_ref[0:HEAD_DIM, :] / acc_ref[HEAD_DIM:HEAD_DIM + 1, :], o_ref, tq)


def _attn_bounded_kernel(b_ref, qt_ref, k_ref, vt_ref, w2_ref, o_ref, w2b_ref,
                         qs_ref, acc_ref, p0_ref, p1_ref, *, tq, tk):
    m_cols = GROUP * tq
    n_chunks = k_ref.shape[0] // tk
    assert n_chunks % 2 == 0 and n_chunks >= 2
    w2b_ref[...] = w2_ref[...].astype(w2b_ref.dtype)
    _stage_queries(qt_ref, qs_ref, tq, b_ref[0, 0])
    acc_ref[...] = jnp.zeros(acc_ref.shape, F32)

    def probs(c, dst_ref, l_run):
        s = jnp.dot(k_ref[pl.ds(_chunk_start(c, tk), tk), :], qs_ref[...], preferred_element_type=F32)
        p = jnp.exp2(s)
        dst_ref[...] = p.astype(BF16)
        return l_run + jnp.sum(p, axis=0, keepdims=True)

    def accumulate(c, src_ref):
        vtc = vt_ref[0:HEAD_DIM, pl.ds(_chunk_start(c, tk), tk)]
        acc_ref[...] += jnp.dot(vtc, src_ref[...], preferred_element_type=F32)

    bufs = (p0_ref, p1_ref)
    unroll = BOUNDED_UNROLL
    assert n_chunks % unroll == 0

    def body(j, l_run):
        c = unroll * j
        for e in range(unroll):
            l_run = probs(c + e + 1, bufs[(e + 1) % 2], l_run)
            accumulate(c + e, bufs[e % 2])
        return l_run

    l_run = probs(0, p0_ref, jnp.zeros((1, m_cols), F32))
    l_run = lax.fori_loop(0, n_chunks // unroll - 1, body, l_run)
    for c in range(n_chunks - unroll, n_chunks):
        if c + 1 < n_chunks:
            l_run = probs(c + 1, bufs[(c + 1) % 2], l_run)
        accumulate(c, bufs[c % 2])
    _store_attention_out(acc_ref[...] / l_run, o_ref, tq)


def _attention(qt, k, vt, q_gain, k_gain, w_ff2, tq_online=256, tq_bounded=1024, tk=512):
    s = k.shape[0]
    gw = GROUP * HEAD_DIM
    smem = pl.BlockSpec(memory_space=pltpu.SMEM)
    attn_specs = lambda tq: [smem,
                             pl.BlockSpec((gw, tq), lambda g, i: (g, i)),
                             pl.BlockSpec((s, K_COLS), lambda g, i: (0, g)),
                             pl.BlockSpec((V_ROWS, s), lambda g, i: (g, 0))]
    out_spec = lambda tq: pl.BlockSpec((tq, gw), lambda g, i: (i, g))
    o_shape = jax.ShapeDtypeStruct((s, Q_W), BF16)

    def online(b):
        tq = tq_online
        m_cols = GROUP * tq
        o = pl.pallas_call(
            functools.partial(_attn_online_kernel, tq=tq, tk=tk),
            grid=(N_KV_HEADS, s // tq),
            in_specs=attn_specs(tq),
            out_specs=out_spec(tq),
            out_shape=o_shape,
            scratch_shapes=[pltpu.VMEM((K_COLS, m_cols), BF16),
                            pltpu.VMEM((V_ROWS, m_cols), F32),
                            pltpu.VMEM((tk, m_cols), F32),
                            pltpu.VMEM((tk, m_cols), F32)],
            compiler_params=_params("gqa_online", 2),
            name="gqa_online")(b, qt, k, vt)
        return o, w_ff2.astype(BF16)

    def bounded(b):
        tq = tq_bounded
        m_cols = GROUP * tq
        n_i = s // tq
        w2_rows = w_ff2.shape[0] // (N_KV_HEADS * n_i)
        w2_spec = pl.BlockSpec((w2_rows, w_ff2.shape[1]), lambda g, i: (g * n_i + i, 0))
        return pl.pallas_call(
            functools.partial(_attn_bounded_kernel, tq=tq, tk=tk),
            grid=(N_KV_HEADS, n_i),
            in_specs=attn_specs(tq) + [w2_spec],
            out_specs=[out_spec(tq), w2_spec],
            out_shape=[o_shape, jax.ShapeDtypeStruct(w_ff2.shape, BF16)],
            scratch_shapes=[pltpu.VMEM((K_COLS, m_cols), BF16),
                            pltpu.VMEM((HEAD_DIM, m_cols), F32),
                            pltpu.VMEM((tk, m_cols), BF16),
                            pltpu.VMEM((tk, m_cols), BF16)],
            compiler_params=_params("gqa_bounded", 2),
            name="gqa_bounded")(b, qt, k, vt, w_ff2)

    bound = (HEAD_DIM * Q_SCALE * SCORE_BOUND_SLACK
             * jnp.max(jnp.abs(q_gain)) * jnp.max(jnp.abs(k_gain))).astype(F32)
    return lax.cond(bound <= MAX_FIXED_SHIFT, bounded, online, bound.reshape(1, 1))


def _merge_kernel(o_ref, w_ref, mc_ref, ga_ref, out_ref, wb_ref):
    _cast_once(pl.program_id(1), w_ref, wb_ref)
    ya = jnp.dot(o_ref[...], wb_ref[...], preferred_element_type=F32)
    out_ref[...] = (mc_ref[...] + ga_ref[...].astype(F32) * ya).astype(out_ref.dtype)


def _merge(o, w_ap, m_c, gates, tm=1024, tn=1024):
    s, d = o.shape
    ga_off = D_MODEL // tn
    return pl.pallas_call(
        _merge_kernel,
        grid=(D_MODEL // tn, s // tm),
        in_specs=[pl.BlockSpec((tm, d), lambda j, i: (i, 0)),
                  pl.BlockSpec((d, tn), lambda j, i: (0, j)),
                  pl.BlockSpec((tm, tn), lambda j, i: (i, j)),
                  pl.BlockSpec((tm, tn), lambda j, i: (i, ga_off + j))],
        out_specs=pl.BlockSpec((tm, tn), lambda j, i: (i, j)),
        out_shape=jax.ShapeDtypeStruct((s, D_MODEL), BF16),
        scratch_shapes=[pltpu.VMEM((d, tn), BF16)],
        compiler_params=_params("merge_attn_proj", 2),
        name="merge_attn_proj",
    )(o, w_ap, m_c, gates)


def _out_kernel(a_ref, w_ref, x_ref, g_ref, x1_ref, h_ref, wb_ref):
    _cast_once(pl.program_id(0), w_ref, wb_ref)
    x1 = x_ref[...] + jnp.dot(a_ref[...], wb_ref[...], preferred_element_type=F32)
    x1_ref[...] = x1
    h_ref[...] = _rms_rows(x1, g_ref[...]).astype(h_ref.dtype)


def _out_proj(a, w, x, g, tm=512):
    s, d = x.shape
    return pl.pallas_call(
        _out_kernel,
        grid=(s // tm,),
        in_specs=[pl.BlockSpec((tm, d), lambda i: (i, 0)),
                  pl.BlockSpec((d, d), lambda i: (0, 0), pipeline_mode=pl.Buffered(1)),
                  pl.BlockSpec((tm, d), lambda i: (i, 0)),
                  pl.BlockSpec((1, d), lambda i: (0, 0))],
        out_specs=[pl.BlockSpec((tm, d), lambda i: (i, 0)),
                   pl.BlockSpec((tm, d), lambda i: (i, 0))],
        out_shape=[jax.ShapeDtypeStruct((s, d), F32),
                   jax.ShapeDtypeStruct((s, d), BF16)],
        scratch_shapes=[pltpu.VMEM((d, d), BF16)],
        compiler_params=_params("out_proj_residual", 1),
        name="out_proj_residual",
    )(a, w, x, g.reshape(1, d))


def _ffn2_kernel(a_ref, w_ref, x_ref, o_ref):
    @pl.when(pl.program_id(2) == 0)
    def _():
        o_ref[...] = x_ref[...]

    o_ref[...] += jnp.dot(a_ref[...], w_ref[...], preferred_element_type=F32)


def _ffn2(a, w, x, tm=1024, tn=1024, tk=4096):
    s, kdim = a.shape
    d = x.shape[1]
    return pl.pallas_call(
        _ffn2_kernel,
        grid=(s // tm, d // tn, kdim // tk),
        in_specs=[pl.BlockSpec((tm, tk), lambda i, j, k: (i, k)),
                  pl.BlockSpec((tk, tn), lambda i, j, k: (k, j)),
                  pl.BlockSpec((tm, tn), lambda i, j, k: (i, j))],
        out_specs=pl.BlockSpec((tm, tn), lambda i, j, k: (i, j)),
        out_shape=jax.ShapeDtypeStruct((s, d), F32),
        compiler_params=_params("ffn_down_residual", 3),
        name="ffn_down_residual",
    )(a, w, x)


def _ple_kernel(x_ref, p_ref, gp_ref, wg_ref, wp_ref, gf_ref, o_ref, wgb_ref, wpb_ref):
    _cast_once(pl.program_id(0), wg_ref, wgb_ref)
    _cast_once(pl.program_id(0), wp_ref, wpb_ref)
    x = x_ref[...]
    h = _rms_rows(x, gp_ref[...]).astype(BF16)
    gate = _sigmoid(jnp.dot(h, wgb_ref[...], preferred_element_type=F32))
    pp = jnp.dot(p_ref[...].astype(BF16), wpb_ref[...], preferred_element_type=F32)
    x3 = x + gate * pp
    o_ref[...] = _rms_rows(x3, gf_ref[...])


def _ple_final(x, p, g_ple, w_gate, w_proj, g_final, tm=512):
    s, d = x.shape
    return pl.pallas_call(
        _ple_kernel,
        grid=(s // tm,),
        in_specs=[pl.BlockSpec((tm, d), lambda i: (i, 0)),
                  pl.BlockSpec((tm, PLE_DIM), lambda i: (i, 0)),
                  pl.BlockSpec((1, d), lambda i: (0, 0)),
                  pl.BlockSpec((d, d), lambda i: (0, 0), pipeline_mode=pl.Buffered(1)),
                  pl.BlockSpec((PLE_DIM, d), lambda i: (0, 0), pipeline_mode=pl.Buffered(1)),
                  pl.BlockSpec((1, d), lambda i: (0, 0))],
        out_specs=pl.BlockSpec((tm, d), lambda i: (i, 0)),
        out_shape=jax.ShapeDtypeStruct((s, d), F32),
        scratch_shapes=[pltpu.VMEM((d, d), BF16), pltpu.VMEM((PLE_DIM, d), BF16)],
        compiler_params=_params("ple_final_norm", 1),
        name="ple_final_norm",
    )(x, p, g_ple.reshape(1, d), w_gate, w_proj, g_final.reshape(1, d))


def _rope_tables():
    n_rows = SEQ // GRID_W
    inv_freq = ROPE_THETA ** (-jnp.arange(0, AXIS_DIM, 2, dtype=F32) / AXIS_DIM)
    ang_row = inv_freq[:, None] * jnp.arange(n_rows, dtype=jnp.int32).astype(F32)[None, :]
    ang_col = inv_freq[:, None] * jnp.arange(GRID_W, dtype=jnp.int32).astype(F32)[None, :]
    nf = inv_freq.shape[0]

    def over_t(row_tab, col_tab):
        r = jnp.broadcast_to(row_tab[:, :, None], (nf, n_rows, GRID_W)).reshape(nf, SEQ)
        c = jnp.broadcast_to(col_tab[:, None, :], (nf, n_rows, GRID_W)).reshape(nf, SEQ)
        return r, c

    cr, cc = over_t(jnp.cos(ang_row), jnp.cos(ang_col))
    sr, sc = over_t(jnp.sin(ang_row), jnp.sin(ang_col))
    cos_t = jnp.concatenate([cr, cr, cc, cc], axis=0)
    sin_t = jnp.concatenate([-sr, sr, -sc, sc], axis=0)
    return cos_t, sin_t


def kernel(x, p, norm_mix, w_in, w_dw, conv_ln_g, conv_ln_b, w_conv_proj, q_norm, k_norm,
           w_attn_proj, w_out, norm_ffn, w_ff1, w_ff2, norm_ple, w_ple_gate, w_ple_proj, norm_final):
    depth = w_in.shape[0]
    assert depth == 1, "the final norm is fused into the last layer's kernel"
    tabs_t = _rope_tables()
    xs = x[0]
    for li in range(depth):
        w_i = w_in[li]
        h, h_t = _rmsnorm_cast(xs, norm_mix[li])
        u = _glu_proj(h, w_i)
        qt = _qt_proj(w_i, h_t, q_norm[li], tabs_t)
        k, vt = _kv_proj(w_i, h_t, k_norm[li], tabs_t)
        gates = _act_proj(h, w_i, OFF_G, 2 * D_MODEL, "sigmoid", "gate_proj")
        m_c = _conv_branch(u, w_dw[li], conv_ln_g[li], conv_ln_b[li], w_conv_proj[li], gates)
        o, w_ff2_b = _attention(qt, k, vt, q_norm[li], k_norm[li], w_ff2[li])
        merged = _merge(o, w_attn_proj[li], m_c, gates)
        x1, h2 = _out_proj(merged, w_out[li], xs, norm_ffn[li])
        a = _act_proj(h2, w_ff1[li], 0, D_FF, "relu2", "ffn_up")
        x2 = _ffn2(a, w_ff2_b, x1)
        xs = _ple_final(x2, p[li, 0], norm_ple[li], w_ple_gate[li], w_ple_proj[li], norm_final)
    return xs[None]
```

```python
import functools
import math

import jax
import jax.numpy as jnp
from jax import lax
from jax.experimental import pallas as pl
from jax.experimental.pallas import tpu as pltpu

D_MODEL = 2048
SEQ = 8192
N_HEADS = 16
N_KV_HEADS = 4
HEAD_DIM = 128
GROUP = N_HEADS // N_KV_HEADS
ROPE_THETA = 10000.0
AXIS_DIM = HEAD_DIM // 2
GRID_W = 64
CONV_WIDTH = D_MODEL // 2
CONV_KERNEL = 31
CONV_HALO = 16
CONV_ROWS = 256
D_FF = 4 * D_MODEL
PLE_DIM = 256
EPS = 1e-6
Q_W = N_HEADS * HEAD_DIM
KV_W = N_KV_HEADS * HEAD_DIM
LANES = 128
SUBLANES = 8
BF16_SUBLANES = 16
V_ROWS = HEAD_DIM + BF16_SUBLANES
K_COLS = 2 * HEAD_DIM

OFF_CA = 0
OFF_CB = CONV_WIDTH
OFF_Q = 2 * CONV_WIDTH
OFF_K = OFF_Q + Q_W
OFF_V = OFF_K + KV_W
OFF_G = OFF_V + KV_W

Q_SCALE = (HEAD_DIM ** -0.5) * math.log2(math.e)
SCORE_BOUND_SLACK = 1.02
MAX_FIXED_SHIFT = 60.0
BOUNDED_UNROLL = 4

BF16 = jnp.bfloat16
F32 = jnp.float32
MIB = 1024 * 1024


_VMEM_LIMIT_MIB = {
    "norm_glu": 56, "qt_proj": 48, "kv_proj": 48, "gate_proj": 58,
    "ffn_up": 58, "conv_branch": 48, "gqa_online": 48, "gqa_bounded": 52,
    "merge_attn_proj": 56, "out_proj_residual": 56, "ffn_down_residual": 58,
    "ple_final_norm": 56,
}


def _params(name, grid_rank):
    return pltpu.CompilerParams(dimension_semantics=("arbitrary",) * grid_rank,
                                vmem_limit_bytes=_VMEM_LIMIT_MIB[name] * MIB)


def _sigmoid(v):
    return 1.0 / (1.0 + jnp.exp(-v))


def _rms_rows(v, g):
    ms = jnp.mean(v * v, axis=-1, keepdims=True)
    return v * lax.rsqrt(ms + EPS) * g


def _cast_once(step, w_ref, wb_ref, transpose=False):
    @pl.when(step == 0)
    def _():
        w = w_ref[...]
        wb_ref[...] = (w.T if transpose else w).astype(wb_ref.dtype)


def _norm_glu_kernel(x_ref, g_ref, w_ref, u_ref, h_ref, ht_ref, wb_ref):
    _cast_once(pl.program_id(0), w_ref, wb_ref)
    y = _rms_rows(x_ref[...], g_ref[...])
    h = y.astype(BF16)
    h_ref[...] = h
    for cb in range(0, y.shape[1], LANES):
        ht_ref[cb:cb + LANES, :] = y[:, cb:cb + LANES].T.astype(ht_ref.dtype)
    a = jnp.dot(h, wb_ref[:, 0:CONV_WIDTH], preferred_element_type=F32)
    b = jnp.dot(h, wb_ref[:, CONV_WIDTH:2 * CONV_WIDTH], preferred_element_type=F32)
    u_ref[...] = a * _sigmoid(b)


def _norm_glu(x, g, w, tm=512):
    s, d = x.shape
    assert OFF_CA == 0 and OFF_CB == CONV_WIDTH
    return pl.pallas_call(
        _norm_glu_kernel,
        grid=(s // tm,),
        in_specs=[pl.BlockSpec((tm, d), lambda i: (i, 0)),
                  pl.BlockSpec((1, d), lambda i: (0, 0)),
                  pl.BlockSpec((d, 2 * CONV_WIDTH), lambda i: (0, 0), pipeline_mode=pl.Buffered(1))],
        out_specs=[pl.BlockSpec((tm, CONV_WIDTH), lambda i: (i, 0)),
                   pl.BlockSpec((tm, d), lambda i: (i, 0)),
                   pl.BlockSpec((d, tm), lambda i: (0, i))],
        out_shape=[jax.ShapeDtypeStruct((s, CONV_WIDTH), F32),
                   jax.ShapeDtypeStruct((s, d), BF16),
                   jax.ShapeDtypeStruct((d, s), BF16)],
        scratch_shapes=[pltpu.VMEM((d, 2 * CONV_WIDTH), BF16)],
        compiler_params=_params("norm_glu", 1),
        name="norm_glu",
    )(x, g.reshape(1, d), w)


def _swap_axis_halves(y):
    q = AXIS_DIM // 2
    return jnp.concatenate([y[q:2 * q], y[0:q], y[3 * q:4 * q], y[2 * q:3 * q]], axis=0)


def _norm_rope_head(zh, g, cos, sin):
    ms = jnp.mean(zh * zh, axis=0, keepdims=True)
    y = zh * lax.rsqrt(ms + EPS) * g
    return y * cos + _swap_axis_halves(y) * sin


def _qt_kernel(w_ref, ht_ref, g_ref, cos_ref, sin_ref, o_ref, wt_ref):
    _cast_once(pl.program_id(1), w_ref, wt_ref, transpose=True)
    zt = jnp.dot(wt_ref[...], ht_ref[...], preferred_element_type=F32)
    g = jnp.broadcast_to(g_ref[...], (HEAD_DIM, zt.shape[1]))
    cos = cos_ref[...]
    sin = sin_ref[...]
    for hh in range(zt.shape[0] // HEAD_DIM):
        rows = slice(hh * HEAD_DIM, (hh + 1) * HEAD_DIM)
        o_ref[rows, :] = (_norm_rope_head(zt[rows, :], g, cos, sin) * Q_SCALE).astype(o_ref.dtype)


def _qt_proj(w, h_t, gain, tabs_t, tm=1024, tn=1024):
    d, s = h_t.shape
    cos_t, sin_t = tabs_t
    tab_spec = pl.BlockSpec((HEAD_DIM, tm), lambda j, i: (0, i))
    return pl.pallas_call(
        _qt_kernel,
        grid=(Q_W // tn, s // tm),
        in_specs=[pl.BlockSpec((d, tn), lambda j, i: (0, OFF_Q // tn + j)),
                  pl.BlockSpec((d, tm), lambda j, i: (0, i)),
                  pl.BlockSpec((HEAD_DIM, 1), lambda j, i: (0, 0)),
                  tab_spec, tab_spec],
        out_specs=pl.BlockSpec((tn, tm), lambda j, i: (j, i)),
        out_shape=jax.ShapeDtypeStruct((Q_W, s), BF16),
        scratch_shapes=[pltpu.VMEM((tn, d), BF16)],
        compiler_params=_params("qt_proj", 2),
        name="qt_proj",
    )(w, h_t, gain.reshape(HEAD_DIM, 1), cos_t, sin_t)


def _kv_kernel(w_ref, ht_ref, g_ref, cos_ref, sin_ref, k_ref, vt_ref, wt_ref):
    _cast_once(pl.program_id(0), w_ref, wt_ref, transpose=True)
    zt = jnp.dot(wt_ref[...], ht_ref[...], preferred_element_type=F32)
    tm = zt.shape[1]
    g = jnp.broadcast_to(g_ref[...], (HEAD_DIM, tm))
    cos = cos_ref[...]
    sin = sin_ref[...]
    lane = lax.broadcasted_iota(jnp.int32, (tm, K_COLS - HEAD_DIM), 1)
    ones_col = jnp.where(lane == 0, 1.0, 0.0).astype(k_ref.dtype)
    row = lax.broadcasted_iota(jnp.int32, (V_ROWS - HEAD_DIM, tm), 0)
    ones_row = jnp.where(row == 0, 1.0, 0.0).astype(vt_ref.dtype)
    for hh in range(N_KV_HEADS):
        r = _norm_rope_head(zt[hh * HEAD_DIM:(hh + 1) * HEAD_DIM, :], g, cos, sin)
        k_ref[:, hh * K_COLS:hh * K_COLS + HEAD_DIM] = r.T.astype(k_ref.dtype)
        k_ref[:, hh * K_COLS + HEAD_DIM:(hh + 1) * K_COLS] = ones_col
        v_rows = slice(KV_W + hh * HEAD_DIM, KV_W + (hh + 1) * HEAD_DIM)
        vt_ref[hh * V_ROWS:hh * V_ROWS + HEAD_DIM, :] = zt[v_rows, :].astype(vt_ref.dtype)
        vt_ref[hh * V_ROWS + HEAD_DIM:(hh + 1) * V_ROWS, :] = ones_row


def _kv_proj(w, h_t, gain, tabs_t, tm=1024):
    d, s = h_t.shape
    assert OFF_V == OFF_K + KV_W and OFF_K % (2 * KV_W) == 0
    cos_t, sin_t = tabs_t
    tab_spec = pl.BlockSpec((HEAD_DIM, tm), lambda i: (0, i))
    return pl.pallas_call(
        _kv_kernel,
        grid=(s // tm,),
        in_specs=[pl.BlockSpec((d, 2 * KV_W), lambda i: (0, OFF_K // (2 * KV_W))),
                  pl.BlockSpec((d, tm), lambda i: (0, i)),
                  pl.BlockSpec((HEAD_DIM, 1), lambda i: (0, 0)),
                  tab_spec, tab_spec],
        out_specs=[pl.BlockSpec((tm, N_KV_HEADS * K_COLS), lambda i: (i, 0)),
                   pl.BlockSpec((N_KV_HEADS * V_ROWS, tm), lambda i: (0, i))],
        out_shape=[jax.ShapeDtypeStruct((s, N_KV_HEADS * K_COLS), BF16),
                   jax.ShapeDtypeStruct((N_KV_HEADS * V_ROWS, s), BF16)],
        scratch_shapes=[pltpu.VMEM((2 * KV_W, d), BF16)],
        compiler_params=_params("kv_proj", 1),
        name="kv_proj",
    )(w, h_t, gain.reshape(HEAD_DIM, 1), cos_t, sin_t)


def _act_kernel(h_ref, w_ref, o_ref, wb_ref, *, act):
    _cast_once(pl.program_id(1), w_ref, wb_ref)
    z = jnp.dot(h_ref[...], wb_ref[...], preferred_element_type=F32)
    if act == "sigmoid":
        z = _sigmoid(z)
    elif act == "relu2":
        z = jnp.square(jnp.maximum(z, 0.0))
    o_ref[...] = z.astype(o_ref.dtype)


def _act_proj(h, w, col_off, width, act, name, tm=2048, tn=1024):
    s, d = h.shape
    return pl.pallas_call(
        functools.partial(_act_kernel, act=act),
        grid=(width // tn, s // tm),
        in_specs=[pl.BlockSpec((tm, d), lambda j, i: (i, 0)),
                  pl.BlockSpec((d, tn), lambda j, i: (0, col_off // tn + j))],
        out_specs=pl.BlockSpec((tm, tn), lambda j, i: (i, j)),
        out_shape=jax.ShapeDtypeStruct((s, width), BF16),
        scratch_shapes=[pltpu.VMEM((d, tn), BF16)],
        compiler_params=_params(name, 2),
        name=name,
    )(h, w)


def _conv_kernel(u_ref, up_ref, un_ref, wdw_ref, lng_ref, lnb_ref, wp_ref, gate_ref,
                 o_ref, buf_ref, cv_ref, wpb_ref):
    i = pl.program_id(0)
    tm = u_ref.shape[0]
    n_slabs = CONV_WIDTH // LANES
    half = CONV_ROWS // 2
    base = CONV_HALO - CONV_KERNEL // 2
    _cast_once(i, wp_ref, wpb_ref)
    prev_ok = (i > 0).astype(F32)
    next_ok = (i < pl.num_programs(0) - 1).astype(F32)
    for sl in range(n_slabs):
        lanes = slice(sl * LANES, (sl + 1) * LANES)
        buf_ref[sl, 0:CONV_HALO, :] = up_ref[:, lanes] * prev_ok
        buf_ref[sl, CONV_HALO:CONV_HALO + tm, :] = u_ref[:, lanes]
        buf_ref[sl, CONV_HALO + tm:, :] = un_ref[:, lanes] * next_ok

    for sl in range(n_slabs):
        def body(r, carry, sl=sl):
            r0 = r * CONV_ROWS
            for par in range(2):
                acc = jnp.zeros((half, LANES), F32)
                for k in range(CONV_KERNEL):
                    tap = buf_ref[sl, pl.ds(r0 + (par + base + k), half, stride=2), :]
                    acc = acc + tap * wdw_ref[k:k + 1, sl * LANES:(sl + 1) * LANES]
                cv_ref[sl, pl.ds(r0 + par, half, stride=2), :] = acc
            return carry
        lax.fori_loop(0, tm // CONV_ROWS, body, 0)

    cv = jnp.concatenate([cv_ref[sl] for sl in range(n_slabs)], axis=1)
    mu = jnp.mean(cv, axis=-1, keepdims=True)
    xc = cv - mu
    var = jnp.mean(xc * xc, axis=-1, keepdims=True)
    y = xc * lax.rsqrt(var + EPS) * lng_ref[...] + lnb_ref[...]
    y = y * _sigmoid(y)
    yc = jnp.dot(y.astype(BF16), wpb_ref[...], preferred_element_type=F32)
    o_ref[...] = gate_ref[...].astype(F32) * yc


def _conv_branch(u, w_dw, ln_g, ln_b, w_proj, gates, tm=512):
    s = u.shape[0]
    hb = tm // CONV_HALO
    n_hblk = s // CONV_HALO
    return pl.pallas_call(
        _conv_kernel,
        grid=(s // tm,),
        in_specs=[pl.BlockSpec((tm, CONV_WIDTH), lambda i: (i, 0)),
                  pl.BlockSpec((CONV_HALO, CONV_WIDTH), lambda i: (jnp.maximum(i * hb - 1, 0), 0)),
                  pl.BlockSpec((CONV_HALO, CONV_WIDTH), lambda i: (jnp.minimum((i + 1) * hb, n_hblk - 1), 0)),
                  pl.BlockSpec((CONV_KERNEL, CONV_WIDTH), lambda i: (0, 0)),
                  pl.BlockSpec((1, CONV_WIDTH), lambda i: (0, 0)),
                  pl.BlockSpec((1, CONV_WIDTH), lambda i: (0, 0)),
                  pl.BlockSpec((CONV_WIDTH, D_MODEL), lambda i: (0, 0), pipeline_mode=pl.Buffered(1)),
                  pl.BlockSpec((tm, D_MODEL), lambda i: (i, 0))],
        out_specs=pl.BlockSpec((tm, D_MODEL), lambda i: (i, 0)),
        out_shape=jax.ShapeDtypeStruct((s, D_MODEL), F32),
        scratch_shapes=[pltpu.VMEM((CONV_WIDTH // LANES, tm + 2 * CONV_HALO, LANES), F32),
                        pltpu.VMEM((CONV_WIDTH // LANES, tm, LANES), F32),
                        pltpu.VMEM((CONV_WIDTH, D_MODEL), BF16)],
        compiler_params=_params("conv_branch", 1),
        name="conv_branch",
    )(u, u, u, w_dw, ln_g.reshape(1, -1), ln_b.reshape(1, -1), w_proj, gates)


def _stage_queries(qt_ref, qs_ref, tq, shift):
    for hh in range(GROUP):
        qs_ref[0:HEAD_DIM, hh * tq:(hh + 1) * tq] = qt_ref[hh * HEAD_DIM:(hh + 1) * HEAD_DIM, :]
    row = lax.broadcasted_iota(jnp.int32, (K_COLS - HEAD_DIM, qs_ref.shape[1]), 0)
    qs_ref[HEAD_DIM:K_COLS, :] = jnp.where(row == 0, -shift, 0.0).astype(qs_ref.dtype)


def _store_attention_out(o_t, o_ref, tq):
    for hh in range(GROUP):
        o_ref[:, hh * HEAD_DIM:(hh + 1) * HEAD_DIM] = o_t[:, hh * tq:(hh + 1) * tq].T.astype(o_ref.dtype)


def _chunk_start(c, tk):
    return c * tk if isinstance(c, int) else pl.multiple_of(c * tk, tk)


def _attn_online_kernel(b_ref, qt_ref, k_ref, vt_ref, o_ref, qs_ref, acc_ref, s0_ref, s1_ref, *, tq, tk):
    m_cols = GROUP * tq
    n_chunks = k_ref.shape[0] // tk
    assert n_chunks % 2 == 0 and n_chunks >= 2
    _stage_queries(qt_ref, qs_ref, tq, b_ref[0, 0])
    acc_ref[...] = jnp.zeros(acc_ref.shape, F32)

    def scores(c, dst_ref):
        s = jnp.dot(k_ref[pl.ds(_chunk_start(c, tk), tk), :], qs_ref[...], preferred_element_type=F32)
        dst_ref[...] = s
        return jnp.max(s, axis=0, keepdims=True)

    def update(c, src_ref, col_max, m_prev):
        m_new = jnp.maximum(m_prev, col_max)
        alpha = jnp.exp2(m_prev - m_new)
        p = jnp.exp2(src_ref[...] - m_new).astype(BF16)
        vtc = vt_ref[:, pl.ds(_chunk_start(c, tk), tk)]
        acc_ref[...] = alpha * acc_ref[...] + jnp.dot(vtc, p, preferred_element_type=F32)
        return m_new

    def pair(j, carry):
        m_run, cm0 = carry
        c = 2 * j
        cm1 = scores(c + 1, s1_ref)
        m_run = update(c, s0_ref, cm0, m_run)
        cm2 = scores(c + 2, s0_ref)
        m_run = update(c + 1, s1_ref, cm1, m_run)
        return m_run, cm2

    cm0 = scores(0, s0_ref)
    carry = (jnp.full((1, m_cols), -1e30, F32), cm0)
    m_run, cm0 = lax.fori_loop(0, n_chunks // 2 - 1, pair, carry)
    cm1 = scores(n_chunks - 1, s1_ref)
    m_run = update(n_chunks - 2, s0_ref, cm0, m_run)
    update(n_chunks - 1, s1_ref, cm1, m_run)
    _store_attention_out(acc_ref[0:HEAD_DIM, :] / acc_ref[HEAD_DIM:HEAD_DIM + 1, :], o_ref, tq)


def _attn_bounded_kernel(b_ref, qt_ref, k_ref, vt_ref, w2_ref, o_ref, w2b_ref,
                         qs_ref, acc_ref, p0_ref, p1_ref, *, tq, tk):
    m_cols = GROUP * tq
    n_chunks = k_ref.shape[0] // tk
    assert n_chunks % 2 == 0 and n_chunks >= 2
    w2b_ref[...] = w2_ref[...].astype(w2b_ref.dtype)
    _stage_queries(qt_ref, qs_ref, tq, b_ref[0, 0])
    acc_ref[...] = jnp.zeros(acc_ref.shape, F32)

    def probs(c, dst_ref, l_run):
        s = jnp.dot(k_ref[pl.ds(_chunk_start(c, tk), tk), :], qs_ref[...], preferred_element_type=F32)
        p = jnp.exp2(s)
        dst_ref[...] = p.astype(BF16)
        return l_run + jnp.sum(p, axis=0, keepdims=True)

    def accumulate(c, src_ref):
        vtc = vt_ref[0:HEAD_DIM, pl.ds(_chunk_start(c, tk), tk)]
        acc_ref[...] += jnp.dot(vtc, src_ref[...], preferred_element_type=F32)

    bufs = (p0_ref, p1_ref)
    unroll = BOUNDED_UNROLL
    assert n_chunks % unroll == 0

    def body(j, l_run):
        c = unroll * j
        for e in range(unroll):
            l_run = probs(c + e + 1, bufs[(e + 1) % 2], l_run)
            accumulate(c + e, bufs[e % 2])
        return l_run

    l_run = probs(0, p0_ref, jnp.zeros((1, m_cols), F32))
    l_run = lax.fori_loop(0, n_chunks // unroll - 1, body, l_run)
    for c in range(n_chunks - unroll, n_chunks):
        if c + 1 < n_chunks:
            l_run = probs(c + 1, bufs[(c + 1) % 2], l_run)
        accumulate(c, bufs[c % 2])
    _store_attention_out(acc_ref[...] / l_run, o_ref, tq)


def _attention(qt, k, vt, q_gain, k_gain, w_ff2, tq_online=256, tq_bounded=1024, tk=512):
    s = k.shape[0]
    gw = GROUP * HEAD_DIM
    smem = pl.BlockSpec(memory_space=pltpu.SMEM)
    attn_specs = lambda tq: [smem,
                             pl.BlockSpec((gw, tq), lambda g, i: (g, i)),
                             pl.BlockSpec((s, K_COLS), lambda g, i: (0, g)),
                             pl.BlockSpec((V_ROWS, s), lambda g, i: (g, 0))]
    out_spec = lambda tq: pl.BlockSpec((tq, gw), lambda g, i: (i, g))
    o_shape = jax.ShapeDtypeStruct((s, Q_W), BF16)

    def online(b):
        tq = tq_online
        m_cols = GROUP * tq
        o = pl.pallas_call(
            functools.partial(_attn_online_kernel, tq=tq, tk=tk),
            grid=(N_KV_HEADS, s // tq),
            in_specs=attn_specs(tq),
            out_specs=out_spec(tq),
            out_shape=o_shape,
            scratch_shapes=[pltpu.VMEM((K_COLS, m_cols), BF16),
                            pltpu.VMEM((V_ROWS, m_cols), F32),
                            pltpu.VMEM((tk, m_cols), F32),
                            pltpu.VMEM((tk, m_cols), F32)],
            compiler_params=_params("gqa_online", 2),
            name="gqa_online")(b, qt, k, vt)
        return o, w_ff2.astype(BF16)

    def bounded(b):
        tq = tq_bounded
        m_cols = GROUP * tq
        n_i = s // tq
        w2_rows = w_ff2.shape[0] // (N_KV_HEADS * n_i)
        w2_spec = pl.BlockSpec((w2_rows, w_ff2.shape[1]), lambda g, i: (g * n_i + i, 0))
        return pl.pallas_call(
            functools.partial(_attn_bounded_kernel, tq=tq, tk=tk),
            grid=(N_KV_HEADS, n_i),
            in_specs=attn_specs(tq) + [w2_spec],
            out_specs=[out_spec(tq), w2_spec],
            out_shape=[o_shape, jax.ShapeDtypeStruct(w_ff2.shape, BF16)],
            scratch_shapes=[pltpu.VMEM((K_COLS, m_cols), BF16),
                            pltpu.VMEM((HEAD_DIM, m_cols), F32),
                            pltpu.VMEM((tk, m_cols), BF16),
                            pltpu.VMEM((tk, m_cols), BF16)],
            compiler_params=_params("gqa_bounded", 2),
            name="gqa_bounded")(b, qt, k, vt, w_ff2)

    bound = (HEAD_DIM * Q_SCALE * SCORE_BOUND_SLACK
             * jnp.max(jnp.abs(q_gain)) * jnp.max(jnp.abs(k_gain))).astype(F32)
    return lax.cond(bound <= MAX_FIXED_SHIFT, bounded, online, bound.reshape(1, 1))


def _merge_kernel(o_ref, w_ref, mc_ref, ga_ref, out_ref, wb_ref):
    _cast_once(pl.program_id(1), w_ref, wb_ref)
    ya = jnp.dot(o_ref[...], wb_ref[...], preferred_element_type=F32)
    out_ref[...] = (mc_ref[...] + ga_ref[...].astype(F32) * ya).astype(out_ref.dtype)


def _merge(o, w_ap, m_c, gates, tm=1024, tn=1024):
    s, d = o.shape
    ga_off = D_MODEL // tn
    return pl.pallas_call(
        _merge_kernel,
        grid=(D_MODEL // tn, s // tm),
        in_specs=[pl.BlockSpec((tm, d), lambda j, i: (i, 0)),
                  pl.BlockSpec((d, tn), lambda j, i: (0, j)),
                  pl.BlockSpec((tm, tn), lambda j, i: (i, j)),
                  pl.BlockSpec((tm, tn), lambda j, i: (i, ga_off + j))],
        out_specs=pl.BlockSpec((tm, tn), lambda j, i: (i, j)),
        out_shape=jax.ShapeDtypeStruct((s, D_MODEL), BF16),
        scratch_shapes=[pltpu.VMEM((d, tn), BF16)],
        compiler_params=_params("merge_attn_proj", 2),
        name="merge_attn_proj",
    )(o, w_ap, m_c, gates)


def _out_kernel(a_ref, w_ref, x_ref, g_ref, x1_ref, h_ref, wb_ref):
    _cast_once(pl.program_id(0), w_ref, wb_ref)
    x1 = x_ref[...] + jnp.dot(a_ref[...], wb_ref[...], preferred_element_type=F32)
    x1_ref[...] = x1
    h_ref[...] = _rms_rows(x1, g_ref[...]).astype(h_ref.dtype)


def _out_proj(a, w, x, g, tm=512):
    s, d = x.shape
    return pl.pallas_call(
        _out_kernel,
        grid=(s // tm,),
        in_specs=[pl.BlockSpec((tm, d), lambda i: (i, 0)),
                  pl.BlockSpec((d, d), lambda i: (0, 0), pipeline_mode=pl.Buffered(1)),
                  pl.BlockSpec((tm, d), lambda i: (i, 0)),
                  pl.BlockSpec((1, d), lambda i: (0, 0))],
        out_specs=[pl.BlockSpec((tm, d), lambda i: (i, 0)),
                   pl.BlockSpec((tm, d), lambda i: (i, 0))],
        out_shape=[jax.ShapeDtypeStruct((s, d), F32),
                   jax.ShapeDtypeStruct((s, d), BF16)],
        scratch_shapes=[pltpu.VMEM((d, d), BF16)],
        compiler_params=_params("out_proj_residual", 1),
        name="out_proj_residual",
    )(a, w, x, g.reshape(1, d))


def _ffn2_kernel(a_ref, w_ref, x_ref, o_ref):
    @pl.when(pl.program_id(2) == 0)
    def _():
        o_ref[...] = x_ref[...]

    o_ref[...] += jnp.dot(a_ref[...], w_ref[...], preferred_element_type=F32)


def _ffn2(a, w, x, tm=1024, tn=1024, tk=4096):
    s, kdim = a.shape
    d = x.shape[1]
    return pl.pallas_call(
        _ffn2_kernel,
        grid=(s // tm, d // tn, kdim // tk),
        in_specs=[pl.BlockSpec((tm, tk), lambda i, j, k: (i, k)),
                  pl.BlockSpec((tk, tn), lambda i, j, k: (k, j)),
                  pl.BlockSpec((tm, tn), lambda i, j, k: (i, j))],
        out_specs=pl.BlockSpec((tm, tn), lambda i, j, k: (i, j)),
        out_shape=jax.ShapeDtypeStruct((s, d), F32),
        compiler_params=_params("ffn_down_residual", 3),
        name="ffn_down_residual",
    )(a, w, x)


def _ple_kernel(x_ref, p_ref, gp_ref, wg_ref, wp_ref, gf_ref, o_ref, wgb_ref, wpb_ref):
    _cast_once(pl.program_id(0), wg_ref, wgb_ref)
    _cast_once(pl.program_id(0), wp_ref, wpb_ref)
    x = x_ref[...]
    h = _rms_rows(x, gp_ref[...]).astype(BF16)
    gate = _sigmoid(jnp.dot(h, wgb_ref[...], preferred_element_type=F32))
    pp = jnp.dot(p_ref[...].astype(BF16), wpb_ref[...], preferred_element_type=F32)
    x3 = x + gate * pp
    o_ref[...] = _rms_rows(x3, gf_ref[...])


def _ple_final(x, p, g_ple, w_gate, w_proj, g_final, tm=512):
    s, d = x.shape
    return pl.pallas_call(
        _ple_kernel,
        grid=(s // tm,),
        in_specs=[pl.BlockSpec((tm, d), lambda i: (i, 0)),
                  pl.BlockSpec((tm, PLE_DIM), lambda i: (i, 0)),
                  pl.BlockSpec((1, d), lambda i: (0, 0)),
                  pl.BlockSpec((d, d), lambda i: (0, 0), pipeline_mode=pl.Buffered(1)),
                  pl.BlockSpec((PLE_DIM, d), lambda i: (0, 0), pipeline_mode=pl.Buffered(1)),
                  pl.BlockSpec((1, d), lambda i: (0, 0))],
        out_specs=pl.BlockSpec((tm, d), lambda i: (i, 0)),
        out_shape=jax.ShapeDtypeStruct((s, d), F32),
        scratch_shapes=[pltpu.VMEM((d, d), BF16), pltpu.VMEM((PLE_DIM, d), BF16)],
        compiler_params=_params("ple_final_norm", 1),
        name="ple_final_norm",
    )(x, p, g_ple.reshape(1, d), w_gate, w_proj, g_final.reshape(1, d))


def _rope_tables():
    n_rows = SEQ // GRID_W
    inv_freq = ROPE_THETA ** (-jnp.arange(0, AXIS_DIM, 2, dtype=F32) / AXIS_DIM)
    ang_row = inv_freq[:, None] * jnp.arange(n_rows, dtype=jnp.int32).astype(F32)[None, :]
    ang_col = inv_freq[:, None] * jnp.arange(GRID_W, dtype=jnp.int32).astype(F32)[None, :]
    nf = inv_freq.shape[0]

    def over_t(row_tab, col_tab):
        r = jnp.broadcast_to(row_tab[:, :, None], (nf, n_rows, GRID_W)).reshape(nf, SEQ)
        c = jnp.broadcast_to(col_tab[:, None, :], (nf, n_rows, GRID_W)).reshape(nf, SEQ)
        return r, c

    cr, cc = over_t(jnp.cos(ang_row), jnp.cos(ang_col))
    sr, sc = over_t(jnp.sin(ang_row), jnp.sin(ang_col))
    cos_t = jnp.concatenate([cr, cr, cc, cc], axis=0)
    sin_t = jnp.concatenate([-sr, sr, -sc, sc], axis=0)
    return cos_t, sin_t


def kernel(x, p, norm_mix, w_in, w_dw, conv_ln_g, conv_ln_b, w_conv_proj, q_norm, k_norm,
           w_attn_proj, w_out, norm_ffn, w_ff1, w_ff2, norm_ple, w_ple_gate, w_ple_proj, norm_final):
    depth = w_in.shape[0]
    assert depth == 1, "the final norm is fused into the last layer's kernel"
    tabs_t = _rope_tables()
    xs = x[0]
    for li in range(depth):
        w_i = w_in[li]
        u, h, h_t = _norm_glu(xs, norm_mix[li], w_i)
        qt = _qt_proj(w_i, h_t, q_norm[li], tabs_t)
        k, vt = _kv_proj(w_i, h_t, k_norm[li], tabs_t)
        gates = _act_proj(h, w_i, OFF_G, 2 * D_MODEL, "sigmoid", "gate_proj")
        m_c = _conv_branch(u, w_dw[li], conv_ln_g[li], conv_ln_b[li], w_conv_proj[li], gates)
        o, w_ff2_b = _attention(qt, k, vt, q_norm[li], k_norm[li], w_ff2[li])
        merged = _merge(o, w_attn_proj[li], m_c, gates)
        x1, h2 = _out_proj(merged, w_out[li], xs, norm_ffn[li])
        a = _act_proj(h2, w_ff1[li], 0, D_FF, "relu2", "ffn_up")
        x2 = _ffn2(a, w_ff2_b, x1)
        xs = _ple_final(x2, p[li, 0], norm_ple[li], w_ple_gate[li], w_ple_proj[li], norm_final)
    return xs[None]
```

```python
import functools
import math

import jax
import jax.numpy as jnp
from jax import lax
from jax.experimental import pallas as pl
from jax.experimental.pallas import tpu as pltpu

D_MODEL = 2048
SEQ = 8192
N_HEADS = 16
N_KV_HEADS = 4
HEAD_DIM = 128
GROUP = N_HEADS // N_KV_HEADS
ROPE_THETA = 10000.0
AXIS_DIM = HEAD_DIM // 2
GRID_W = 64
CONV_WIDTH = D_MODEL // 2
CONV_KERNEL = 31
CONV_HALO = 16
CONV_ROWS = 256
D_FF = 4 * D_MODEL
PLE_DIM = 256
EPS = 1e-6
Q_W = N_HEADS * HEAD_DIM
KV_W = N_KV_HEADS * HEAD_DIM
LANES = 128
SUBLANES = 8
BF16_SUBLANES = 16
V_ROWS = HEAD_DIM + BF16_SUBLANES
K_COLS = 2 * HEAD_DIM

OFF_CA = 0
OFF_CB = CONV_WIDTH
OFF_Q = 2 * CONV_WIDTH
OFF_K = OFF_Q + Q_W
OFF_V = OFF_K + KV_W
OFF_G = OFF_V + KV_W

Q_SCALE = (HEAD_DIM ** -0.5) * math.log2(math.e)
SCORE_BOUND_SLACK = 1.02
MAX_FIXED_SHIFT = 60.0
BOUNDED_UNROLL = 4

BF16 = jnp.bfloat16
F32 = jnp.float32
MIB = 1024 * 1024


_VMEM_LIMIT_MIB = {
    "norm_glu": 56, "qt_proj": 48, "kv_proj": 48, "gate_proj": 58,
    "ffn_up": 58, "conv_branch": 48, "gqa_online": 48, "gqa_bounded": 52,
    "merge_attn_proj": 56, "out_proj_residual": 56, "ffn_down_residual": 58,
    "ple_final_norm": 56,
}


def _params(name, grid_rank):
    return pltpu.CompilerParams(dimension_semantics=("arbitrary",) * grid_rank,
                                vmem_limit_bytes=_VMEM_LIMIT_MIB[name] * MIB)


def _sigmoid(v):
    return 1.0 / (1.0 + jnp.exp(-v))


def _rms_rows(v, g):
    ms = jnp.mean(v * v, axis=-1, keepdims=True)
    return v * lax.rsqrt(ms + EPS) * g


def _cast_once(step, w_ref, wb_ref, transpose=False):
    @pl.when(step == 0)
    def _():
        w = w_ref[...]
        wb_ref[...] = (w.T if transpose else w).astype(wb_ref.dtype)


def _norm_glu_kernel(x_ref, g_ref, w_ref, u_ref, h_ref, ht_ref, wb_ref):
    _cast_once(pl.program_id(0), w_ref, wb_ref)
    y = _rms_rows(x_ref[...], g_ref[...])
    h = y.astype(BF16)
    h_ref[...] = h
    for cb in range(0, y.shape[1], LANES):
        ht_ref[cb:cb + LANES, :] = y[:, cb:cb + LANES].T.astype(ht_ref.dtype)
    a = jnp.dot(h, wb_ref[:, 0:CONV_WIDTH], preferred_element_type=F32)
    b = jnp.dot(h, wb_ref[:, CONV_WIDTH:2 * CONV_WIDTH], preferred_element_type=F32)
    u_ref[...] = a * _sigmoid(b)


def _norm_glu(x, g, w, tm=512):
    s, d = x.shape
    assert OFF_CA == 0 and OFF_CB == CONV_WIDTH
    return pl.pallas_call(
        _norm_glu_kernel,
        grid=(s // tm,),
        in_specs=[pl.BlockSpec((tm, d), lambda i: (i, 0)),
                  pl.BlockSpec((1, d), lambda i: (0, 0)),
                  pl.BlockSpec((d, 2 * CONV_WIDTH), lambda i: (0, 0), pipeline_mode=pl.Buffered(1))],
        out_specs=[pl.BlockSpec((tm, CONV_WIDTH), lambda i: (i, 0)),
                   pl.BlockSpec((tm, d), lambda i: (i, 0)),
                   pl.BlockSpec((d, tm), lambda i: (0, i))],
        out_shape=[jax.ShapeDtypeStruct((s, CONV_WIDTH), F32),
                   jax.ShapeDtypeStruct((s, d), BF16),
                   jax.ShapeDtypeStruct((d, s), BF16)],
        scratch_shapes=[pltpu.VMEM((d, 2 * CONV_WIDTH), BF16)],
        compiler_params=_params("norm_glu", 1),
        name="norm_glu",
    )(x, g.reshape(1, d), w)


def _swap_axis_halves(y):
    q = AXIS_DIM // 2
    return jnp.concatenate([y[q:2 * q], y[0:q], y[3 * q:4 * q], y[2 * q:3 * q]], axis=0)


def _gained_tables(g_ref, cos_ref, sin_ref, scale):
    g = jnp.broadcast_to(g_ref[...], cos_ref.shape)
    return cos_ref[...] * (g * scale), sin_ref[...] * (_swap_axis_halves(g) * scale)


def _norm_rope_head(zh, cos_g, sin_g):
    ms = jnp.mean(zh * zh, axis=0, keepdims=True)
    y = zh * lax.rsqrt(ms + EPS)
    return y * cos_g + _swap_axis_halves(y) * sin_g


def _qt_kernel(w_ref, ht_ref, g_ref, cos_ref, sin_ref, o_ref, wt_ref):
    _cast_once(pl.program_id(1), w_ref, wt_ref, transpose=True)
    zt = jnp.dot(wt_ref[...], ht_ref[...], preferred_element_type=F32)
    cos_g, sin_g = _gained_tables(g_ref, cos_ref, sin_ref, Q_SCALE)
    for hh in range(zt.shape[0] // HEAD_DIM):
        rows = slice(hh * HEAD_DIM, (hh + 1) * HEAD_DIM)
        o_ref[rows, :] = _norm_rope_head(zt[rows, :], cos_g, sin_g).astype(o_ref.dtype)


def _qt_proj(w, h_t, gain, tabs_t, tm=1024, tn=1024):
    d, s = h_t.shape
    cos_t, sin_t = tabs_t
    tab_spec = pl.BlockSpec((HEAD_DIM, tm), lambda j, i: (0, i))
    return pl.pallas_call(
        _qt_kernel,
        grid=(Q_W // tn, s // tm),
        in_specs=[pl.BlockSpec((d, tn), lambda j, i: (0, OFF_Q // tn + j)),
                  pl.BlockSpec((d, tm), lambda j, i: (0, i)),
                  pl.BlockSpec((HEAD_DIM, 1), lambda j, i: (0, 0)),
                  tab_spec, tab_spec],
        out_specs=pl.BlockSpec((tn, tm), lambda j, i: (j, i)),
        out_shape=jax.ShapeDtypeStruct((Q_W, s), BF16),
        scratch_shapes=[pltpu.VMEM((tn, d), BF16)],
        compiler_params=_params("qt_proj", 2),
        name="qt_proj",
    )(w, h_t, gain.reshape(HEAD_DIM, 1), cos_t, sin_t)


def _kv_kernel(w_ref, ht_ref, g_ref, cos_ref, sin_ref, k_ref, vt_ref, wt_ref):
    _cast_once(pl.program_id(0), w_ref, wt_ref, transpose=True)
    zt = jnp.dot(wt_ref[...], ht_ref[...], preferred_element_type=F32)
    tm = zt.shape[1]
    cos_g, sin_g = _gained_tables(g_ref, cos_ref, sin_ref, 1.0)
    lane = lax.broadcasted_iota(jnp.int32, (tm, K_COLS - HEAD_DIM), 1)
    ones_col = jnp.where(lane == 0, 1.0, 0.0).astype(k_ref.dtype)
    row = lax.broadcasted_iota(jnp.int32, (V_ROWS - HEAD_DIM, tm), 0)
    ones_row = jnp.where(row == 0, 1.0, 0.0).astype(vt_ref.dtype)
    for hh in range(N_KV_HEADS):
        r = _norm_rope_head(zt[hh * HEAD_DIM:(hh + 1) * HEAD_DIM, :], cos_g, sin_g)
        k_ref[:, hh * K_COLS:hh * K_COLS + HEAD_DIM] = r.T.astype(k_ref.dtype)
        k_ref[:, hh * K_COLS + HEAD_DIM:(hh + 1) * K_COLS] = ones_col
        v_rows = slice(KV_W + hh * HEAD_DIM, KV_W + (hh + 1) * HEAD_DIM)
        vt_ref[hh * V_ROWS:hh * V_ROWS + HEAD_DIM, :] = zt[v_rows, :].astype(vt_ref.dtype)
        vt_ref[hh * V_ROWS + HEAD_DIM:(hh + 1) * V_ROWS, :] = ones_row


def _kv_proj(w, h_t, gain, tabs_t, tm=1024):
    d, s = h_t.shape
    assert OFF_V == OFF_K + KV_W and OFF_K % (2 * KV_W) == 0
    cos_t, sin_t = tabs_t
    tab_spec = pl.BlockSpec((HEAD_DIM, tm), lambda i: (0, i))
    return pl.pallas_call(
        _kv_kernel,
        grid=(s // tm,),
        in_specs=[pl.BlockSpec((d, 2 * KV_W), lambda i: (0, OFF_K // (2 * KV_W))),
                  pl.BlockSpec((d, tm), lambda i: (0, i)),
                  pl.BlockSpec((HEAD_DIM, 1), lambda i: (0, 0)),
                  tab_spec, tab_spec],
        out_specs=[pl.BlockSpec((tm, N_KV_HEADS * K_COLS), lambda i: (i, 0)),
                   pl.BlockSpec((N_KV_HEADS * V_ROWS, tm), lambda i: (0, i))],
        out_shape=[jax.ShapeDtypeStruct((s, N_KV_HEADS * K_COLS), BF16),
                   jax.ShapeDtypeStruct((N_KV_HEADS * V_ROWS, s), BF16)],
        scratch_shapes=[pltpu.VMEM((2 * KV_W, d), BF16)],
        compiler_params=_params("kv_proj", 1),
        name="kv_proj",
    )(w, h_t, gain.reshape(HEAD_DIM, 1), cos_t, sin_t)


def _act_kernel(h_ref, w_ref, o_ref, wb_ref, *, act):
    _cast_once(pl.program_id(1), w_ref, wb_ref)
    z = jnp.dot(h_ref[...], wb_ref[...], preferred_element_type=F32)
    if act == "sigmoid":
        z = _sigmoid(z)
    elif act == "relu2":
        z = jnp.square(jnp.maximum(z, 0.0))
    o_ref[...] = z.astype(o_ref.dtype)


def _act_proj(h, w, col_off, width, act, name, tm=2048, tn=1024):
    s, d = h.shape
    return pl.pallas_call(
        functools.partial(_act_kernel, act=act),
        grid=(width // tn, s // tm),
        in_specs=[pl.BlockSpec((tm, d), lambda j, i: (i, 0)),
                  pl.BlockSpec((d, tn), lambda j, i: (0, col_off // tn + j))],
        out_specs=pl.BlockSpec((tm, tn), lambda j, i: (i, j)),
        out_shape=jax.ShapeDtypeStruct((s, width), BF16),
        scratch_shapes=[pltpu.VMEM((d, tn), BF16)],
        compiler_params=_params(name, 2),
        name=name,
    )(h, w)


def _conv_kernel(u_ref, up_ref, un_ref, wdw_ref, lng_ref, lnb_ref, wp_ref, gate_ref,
                 o_ref, buf_ref, cv_ref, wpb_ref):
    i = pl.program_id(0)
    tm = u_ref.shape[0]
    n_slabs = CONV_WIDTH // LANES
    half = CONV_ROWS // 2
    base = CONV_HALO - CONV_KERNEL // 2
    _cast_once(i, wp_ref, wpb_ref)
    prev_ok = (i > 0).astype(F32)
    next_ok = (i < pl.num_programs(0) - 1).astype(F32)
    for sl in range(n_slabs):
        lanes = slice(sl * LANES, (sl + 1) * LANES)
        buf_ref[sl, 0:CONV_HALO, :] = up_ref[:, lanes] * prev_ok
        buf_ref[sl, CONV_HALO:CONV_HALO + tm, :] = u_ref[:, lanes]
        buf_ref[sl, CONV_HALO + tm:, :] = un_ref[:, lanes] * next_ok

    for sl in range(n_slabs):
        def body(r, carry, sl=sl):
            r0 = r * CONV_ROWS
            for par in range(2):
                acc = jnp.zeros((half, LANES), F32)
                for k in range(CONV_KERNEL):
                    tap = buf_ref[sl, pl.ds(r0 + (par + base + k), half, stride=2), :]
                    acc = acc + tap * wdw_ref[k:k + 1, sl * LANES:(sl + 1) * LANES]
                cv_ref[sl, pl.ds(r0 + par, half, stride=2), :] = acc
            return carry
        lax.fori_loop(0, tm // CONV_ROWS, body, 0)

    cv = jnp.concatenate([cv_ref[sl] for sl in range(n_slabs)], axis=1)
    mu = jnp.mean(cv, axis=-1, keepdims=True)
    xc = cv - mu
    var = jnp.mean(xc * xc, axis=-1, keepdims=True)
    y = xc * lax.rsqrt(var + EPS) * lng_ref[...] + lnb_ref[...]
    y = y * _sigmoid(y)
    yc = jnp.dot(y.astype(BF16), wpb_ref[...], preferred_element_type=F32)
    o_ref[...] = gate_ref[...].astype(F32) * yc


def _conv_branch(u, w_dw, ln_g, ln_b, w_proj, gates, tm=512):
    s = u.shape[0]
    hb = tm // CONV_HALO
    n_hblk = s // CONV_HALO
    return pl.pallas_call(
        _conv_kernel,
        grid=(s // tm,),
        in_specs=[pl.BlockSpec((tm, CONV_WIDTH), lambda i: (i, 0)),
                  pl.BlockSpec((CONV_HALO, CONV_WIDTH), lambda i: (jnp.maximum(i * hb - 1, 0), 0)),
                  pl.BlockSpec((CONV_HALO, CONV_WIDTH), lambda i: (jnp.minimum((i + 1) * hb, n_hblk - 1), 0)),
                  pl.BlockSpec((CONV_KERNEL, CONV_WIDTH), lambda i: (0, 0)),
                  pl.BlockSpec((1, CONV_WIDTH), lambda i: (0, 0)),
                  pl.BlockSpec((1, CONV_WIDTH), lambda i: (0, 0)),
                  pl.BlockSpec((CONV_WIDTH, D_MODEL), lambda i: (0, 0), pipeline_mode=pl.Buffered(1)),
                  pl.BlockSpec((tm, D_MODEL), lambda i: (i, 0))],
        out_specs=pl.BlockSpec((tm, D_MODEL), lambda i: (i, 0)),
        out_shape=jax.ShapeDtypeStruct((s, D_MODEL), F32),
        scratch_shapes=[pltpu.VMEM((CONV_WIDTH // LANES, tm + 2 * CONV_HALO, LANES), F32),
                        pltpu.VMEM((CONV_WIDTH // LANES, tm, LANES), F32),
                        pltpu.VMEM((CONV_WIDTH, D_MODEL), BF16)],
        compiler_params=_params("conv_branch", 1),
        name="conv_branch",
    )(u, u, u, w_dw, ln_g.reshape(1, -1), ln_b.reshape(1, -1), w_proj, gates)


def _stage_queries(qt_ref, qs_ref, tq, shift):
    for hh in range(GROUP):
        qs_ref[0:HEAD_DIM, hh * tq:(hh + 1) * tq] = qt_ref[hh * HEAD_DIM:(hh + 1) * HEAD_DIM, :]
    row = lax.broadcasted_iota(jnp.int32, (K_COLS - HEAD_DIM, qs_ref.shape[1]), 0)
    qs_ref[HEAD_DIM:K_COLS, :] = jnp.where(row == 0, -shift, 0.0).astype(qs_ref.dtype)


def _store_attention_out(o_t, o_ref, tq):
    for hh in range(GROUP):
        o_ref[:, hh * HEAD_DIM:(hh + 1) * HEAD_DIM] = o_t[:, hh * tq:(hh + 1) * tq].T.astype(o_ref.dtype)


def _chunk_start(c, tk):
    return c * tk if isinstance(c, int) else pl.multiple_of(c * tk, tk)


def _attn_online_kernel(b_ref, qt_ref, k_ref, vt_ref, o_ref, qs_ref, acc_ref, s0_ref, s1_ref, *, tq, tk):
    m_cols = GROUP * tq
    n_chunks = k_ref.shape[0] // tk
    assert n_chunks % 2 == 0 and n_chunks >= 2
    _stage_queries(qt_ref, qs_ref, tq, b_ref[0, 0])
    acc_ref[...] = jnp.zeros(acc_ref.shape, F32)

    def scores(c, dst_ref):
        s = jnp.dot(k_ref[pl.ds(_chunk_start(c, tk), tk), :], qs_ref[...], preferred_element_type=F32)
        dst_ref[...] = s
        return jnp.max(s, axis=0, keepdims=True)

    def update(c, src_ref, col_max, m_prev):
        m_new = jnp.maximum(m_prev, col_max)
        alpha = jnp.exp2(m_prev - m_new)
        p = jnp.exp2(src_ref[...] - m_new).astype(BF16)
        vtc = vt_ref[:, pl.ds(_chunk_start(c, tk), tk)]
        acc_ref[...] = alpha * acc_ref[...] + jnp.dot(vtc, p, preferred_element_type=F32)
        return m_new

    def pair(j, carry):
        m_run, cm0 = carry
        c = 2 * j
        cm1 = scores(c + 1, s1_ref)
        m_run = update(c, s0_ref, cm0, m_run)
        cm2 = scores(c + 2, s0_ref)
        m_run = update(c + 1, s1_ref, cm1, m_run)
        return m_run, cm2

    cm0 = scores(0, s0_ref)
    carry = (jnp.full((1, m_cols), -1e30, F32), cm0)
    m_run, cm0 = lax.fori_loop(0, n_chunks // 2 - 1, pair, carry)
    cm1 = scores(n_chunks - 1, s1_ref)
    m_run = update(n_chunks - 2, s0_ref, cm0, m_run)
    update(n_chunks - 1, s1_ref, cm1, m_run)
    _store_attention_out(acc_ref[0:HEAD_DIM, :] / acc_ref[HEAD_DIM:HEAD_DIM + 1, :], o_ref, tq)


def _attn_bounded_kernel(b_ref, qt_ref, k_ref, vt_ref, w2_ref, o_ref, w2b_ref,
                         qs_ref, acc_ref, p0_ref, p1_ref, *, tq, tk):
    m_cols = GROUP * tq
    n_chunks = k_ref.shape[0] // tk
    assert n_chunks % 2 == 0 and n_chunks >= 2
    w2b_ref[...] = w2_ref[...].astype(w2b_ref.dtype)
    _stage_queries(qt_ref, qs_ref, tq, b_ref[0, 0])
    acc_ref[...] = jnp.zeros(acc_ref.shape, F32)

    def probs(c, dst_ref, l_run):
        s = jnp.dot(k_ref[pl.ds(_chunk_start(c, tk), tk), :], qs_ref[...], preferred_element_type=F32)
        p = jnp.exp2(s)
        dst_ref[...] = p.astype(BF16)
        return l_run + jnp.sum(p, axis=0, keepdims=True)

    def accumulate(c, src_ref):
        vtc = vt_ref[0:HEAD_DIM, pl.ds(_chunk_start(c, tk), tk)]
        acc_ref[...] += jnp.dot(vtc, src_ref[...], preferred_element_type=F32)

    bufs = (p0_ref, p1_ref)
    unroll = BOUNDED_UNROLL
    assert n_chunks % unroll == 0

    def body(j, l_run):
        c = unroll * j
        for e in range(unroll):
            l_run = probs(c + e + 1, bufs[(e + 1) % 2], l_run)
            accumulate(c + e, bufs[e % 2])
        return l_run

    l_run = probs(0, p0_ref, jnp.zeros((1, m_cols), F32))
    l_run = lax.fori_loop(0, n_chunks // unroll - 1, body, l_run)
    for c in range(n_chunks - unroll, n_chunks):
        if c + 1 < n_chunks:
            l_run = probs(c + 1, bufs[(c + 1) % 2], l_run)
        accumulate(c, bufs[c % 2])
    _store_attention_out(acc_ref[...] / l_run, o_ref, tq)


def _attention(qt, k, vt, q_gain, k_gain, w_ff2, tq_online=256, tq_bounded=1024, tk=512):
    s = k.shape[0]
    gw = GROUP * HEAD_DIM
    smem = pl.BlockSpec(memory_space=pltpu.SMEM)
    attn_specs = lambda tq: [smem,
                             pl.BlockSpec((gw, tq), lambda g, i: (g, i)),
                             pl.BlockSpec((s, K_COLS), lambda g, i: (0, g)),
                             pl.BlockSpec((V_ROWS, s), lambda g, i: (g, 0))]
    out_spec = lambda tq: pl.BlockSpec((tq, gw), lambda g, i: (i, g))
    o_shape = jax.ShapeDtypeStruct((s, Q_W), BF16)

    def online(b):
        tq = tq_online
        m_cols = GROUP * tq
        o = pl.pallas_call(
            functools.partial(_attn_online_kernel, tq=tq, tk=tk),
            grid=(N_KV_HEADS, s // tq),
            in_specs=attn_specs(tq),
            out_specs=out_spec(tq),
            out_shape=o_shape,
            scratch_shapes=[pltpu.VMEM((K_COLS, m_cols), BF16),
                            pltpu.VMEM((V_ROWS, m_cols), F32),
                            pltpu.VMEM((tk, m_cols), F32),
                            pltpu.VMEM((tk, m_cols), F32)],
            compiler_params=_params("gqa_online", 2),
            name="gqa_online")(b, qt, k, vt)
        return o, w_ff2.astype(BF16)

    def bounded(b):
        tq = tq_bounded
        m_cols = GROUP * tq
        n_i = s // tq
        w2_rows = w_ff2.shape[0] // (N_KV_HEADS * n_i)
        w2_spec = pl.BlockSpec((w2_rows, w_ff2.shape[1]), lambda g, i: (g * n_i + i, 0))
        return pl.pallas_call(
            functools.partial(_attn_bounded_kernel, tq=tq, tk=tk),
            grid=(N_KV_HEADS, n_i),
            in_specs=attn_specs(tq) + [w2_spec],
            out_specs=[out_spec(tq), w2_spec],
            out_shape=[o_shape, jax.ShapeDtypeStruct(w_ff2.shape, BF16)],
            scratch_shapes=[pltpu.VMEM((K_COLS, m_cols), BF16),
                            pltpu.VMEM((HEAD_DIM, m_cols), F32),
                            pltpu.VMEM((tk, m_cols), BF16),
                            pltpu.VMEM((tk, m_cols), BF16)],
            compiler_params=_params("gqa_bounded", 2),
            name="gqa_bounded")(b, qt, k, vt, w_ff2)

    bound = (HEAD_DIM * Q_SCALE * SCORE_BOUND_SLACK
             * jnp.max(jnp.abs(q_gain)) * jnp.max(jnp.abs(k_gain))).astype(F32)
    return lax.cond(bound <= MAX_FIXED_SHIFT, bounded, online, bound.reshape(1, 1))


def _merge_kernel(o_ref, w_ref, mc_ref, ga_ref, out_ref, wb_ref):
    _cast_once(pl.program_id(1), w_ref, wb_ref)
    ya = jnp.dot(o_ref[...], wb_ref[...], preferred_element_type=F32)
    out_ref[...] = (mc_ref[...] + ga_ref[...].astype(F32) * ya).astype(out_ref.dtype)


def _merge(o, w_ap, m_c, gates, tm=1024, tn=1024):
    s, d = o.shape
    ga_off = D_MODEL // tn
    return pl.pallas_call(
        _merge_kernel,
        grid=(D_MODEL // tn, s // tm),
        in_specs=[pl.BlockSpec((tm, d), lambda j, i: (i, 0)),
                  pl.BlockSpec((d, tn), lambda j, i: (0, j)),
                  pl.BlockSpec((tm, tn), lambda j, i: (i, j)),
                  pl.BlockSpec((tm, tn), lambda j, i: (i, ga_off + j))],
        out_specs=pl.BlockSpec((tm, tn), lambda j, i: (i, j)),
        out_shape=jax.ShapeDtypeStruct((s, D_MODEL), BF16),
        scratch_shapes=[pltpu.VMEM((d, tn), BF16)],
        compiler_params=_params("merge_attn_proj", 2),
        name="merge_attn_proj",
    )(o, w_ap, m_c, gates)


def _out_kernel(a_ref, w_ref, x_ref, g_ref, x1_ref, h_ref, wb_ref):
    _cast_once(pl.program_id(0), w_ref, wb_ref)
    x1 = x_ref[...] + jnp.dot(a_ref[...], wb_ref[...], preferred_element_type=F32)
    x1_ref[...] = x1
    h_ref[...] = _rms_rows(x1, g_ref[...]).astype(h_ref.dtype)


def _out_proj(a, w, x, g, tm=512):
    s, d = x.shape
    return pl.pallas_call(
        _out_kernel,
        grid=(s // tm,),
        in_specs=[pl.BlockSpec((tm, d), lambda i: (i, 0)),
                  pl.BlockSpec((d, d), lambda i: (0, 0), pipeline_mode=pl.Buffered(1)),
                  pl.BlockSpec((tm, d), lambda i: (i, 0)),
                  pl.BlockSpec((1, d), lambda i: (0, 0))],
        out_specs=[pl.BlockSpec((tm, d), lambda i: (i, 0)),
                   pl.BlockSpec((tm, d), lambda i: (i, 0))],
        out_shape=[jax.ShapeDtypeStruct((s, d), F32),
                   jax.ShapeDtypeStruct((s, d), BF16)],
        scratch_shapes=[pltpu.VMEM((d, d), BF16)],
        compiler_params=_params("out_proj_residual", 1),
        name="out_proj_residual",
    )(a, w, x, g.reshape(1, d))


def _ffn2_kernel(a_ref, w_ref, x_ref, o_ref):
    @pl.when(pl.program_id(2) == 0)
    def _():
        o_ref[...] = x_ref[...]

    o_ref[...] += jnp.dot(a_ref[...], w_ref[...], preferred_element_type=F32)


def _ffn2(a, w, x, tm=1024, tn=1024, tk=4096):
    s, kdim = a.shape
    d = x.shape[1]
    return pl.pallas_call(
        _ffn2_kernel,
        grid=(s // tm, d // tn, kdim // tk),
        in_specs=[pl.BlockSpec((tm, tk), lambda i, j, k: (i, k)),
                  pl.BlockSpec((tk, tn), lambda i, j, k: (k, j)),
                  pl.BlockSpec((tm, tn), lambda i, j, k: (i, j))],
        out_specs=pl.BlockSpec((tm, tn), lambda i, j, k: (i, j)),
        out_shape=jax.ShapeDtypeStruct((s, d), F32),
        compiler_params=_params("ffn_down_residual", 3),
        name="ffn_down_residual",
    )(a, w, x)


def _ple_kernel(x_ref, p_ref, gp_ref, wg_ref, wp_ref, gf_ref, o_ref, wgb_ref, wpb_ref):
    _cast_once(pl.program_id(0), wg_ref, wgb_ref)
    _cast_once(pl.program_id(0), wp_ref, wpb_ref)
    x = x_ref[...]
    h = _rms_rows(x, gp_ref[...]).astype(BF16)
    gate = _sigmoid(jnp.dot(h, wgb_ref[...], preferred_element_type=F32))
    pp = jnp.dot(p_ref[...].astype(BF16), wpb_ref[...], preferred_element_type=F32)
    x3 = x + gate * pp
    o_ref[...] = _rms_rows(x3, gf_ref[...])


def _ple_final(x, p, g_ple, w_gate, w_proj, g_final, tm=512):
    s, d = x.shape
    return pl.pallas_call(
        _ple_kernel,
        grid=(s // tm,),
        in_specs=[pl.BlockSpec((tm, d), lambda i: (i, 0)),
                  pl.BlockSpec((tm, PLE_DIM), lambda i: (i, 0)),
                  pl.BlockSpec((1, d), lambda i: (0, 0)),
                  pl.BlockSpec((d, d), lambda i: (0, 0), pipeline_mode=pl.Buffered(1)),
                  pl.BlockSpec((PLE_DIM, d), lambda i: (0, 0), pipeline_mode=pl.Buffered(1)),
                  pl.BlockSpec((1, d), lambda i: (0, 0))],
        out_specs=pl.BlockSpec((tm, d), lambda i: (i, 0)),
        out_shape=jax.ShapeDtypeStruct((s, d), F32),
        scratch_shapes=[pltpu.VMEM((d, d), BF16), pltpu.VMEM((PLE_DIM, d), BF16)],
        compiler_params=_params("ple_final_norm", 1),
        name="ple_final_norm",
    )(x, p, g_ple.reshape(1, d), w_gate, w_proj, g_final.reshape(1, d))


def _rope_tables():
    n_rows = SEQ // GRID_W
    inv_freq = ROPE_THETA ** (-jnp.arange(0, AXIS_DIM, 2, dtype=F32) / AXIS_DIM)
    ang_row = inv_freq[:, None] * jnp.arange(n_rows, dtype=jnp.int32).astype(F32)[None, :]
    ang_col = inv_freq[:, None] * jnp.arange(GRID_W, dtype=jnp.int32).astype(F32)[None, :]
    nf = inv_freq.shape[0]

    def over_t(row_tab, col_tab):
        r = jnp.broadcast_to(row_tab[:, :, None], (nf, n_rows, GRID_W)).reshape(nf, SEQ)
        c = jnp.broadcast_to(col_tab[:, None, :], (nf, n_rows, GRID_W)).reshape(nf, SEQ)
        return r, c

    cr, cc = over_t(jnp.cos(ang_row), jnp.cos(ang_col))
    sr, sc = over_t(jnp.sin(ang_row), jnp.sin(ang_col))
    cos_t = jnp.concatenate([cr, cr, cc, cc], axis=0)
    sin_t = jnp.concatenate([-sr, sr, -sc, sc], axis=0)
    return cos_t, sin_t


def kernel(x, p, norm_mix, w_in, w_dw, conv_ln_g, conv_ln_b, w_conv_proj, q_norm, k_norm,
           w_attn_proj, w_out, norm_ffn, w_ff1, w_ff2, norm_ple, w_ple_gate, w_ple_proj, norm_final):
    depth = w_in.shape[0]
    assert depth == 1, "the final norm is fused into the last layer's kernel"
    tabs_t = _rope_tables()
    xs = x[0]
    for li in range(depth):
        w_i = w_in[li]
        u, h, h_t = _norm_glu(xs, norm_mix[li], w_i)
        qt = _qt_proj(w_i, h_t, q_norm[li], tabs_t)
        k, vt = _kv_proj(w_i, h_t, k_norm[li], tabs_t)
        gates = _act_proj(h, w_i, OFF_G, 2 * D_MODEL, "sigmoid", "gate_proj")
        m_c = _conv_branch(u, w_dw[li], conv_ln_g[li], conv_ln_b[li], w_conv_proj[li], gates)
        o, w_ff2_b = _attention(qt, k, vt, q_norm[li], k_norm[li], w_ff2[li])
        merged = _merge(o, w_attn_proj[li], m_c, gates)
        x1, h2 = _out_proj(merged, w_out[li], xs, norm_ffn[li])
        a = _act_proj(h2, w_ff1[li], 0, D_FF, "relu2", "ffn_up")
        x2 = _ffn2(a, w_ff2_b, x1)
        xs = _ple_final(x2, p[li, 0], norm_ple[li], w_ple_gate[li], w_ple_proj[li], norm_final)
    return xs[None]
```

```python
import functools
import math

import jax
import jax.numpy as jnp
from jax import lax
from jax.experimental import pallas as pl
from jax.experimental.pallas import tpu as pltpu

D_MODEL = 2048
SEQ = 8192
N_HEADS = 16
N_KV_HEADS = 4
HEAD_DIM = 128
GROUP = N_HEADS // N_KV_HEADS
ROPE_THETA = 10000.0
AXIS_DIM = HEAD_DIM // 2
GRID_W = 64
CONV_WIDTH = D_MODEL // 2
CONV_KERNEL = 31
CONV_HALO = 16
CONV_ROWS = 256
D_FF = 4 * D_MODEL
PLE_DIM = 256
EPS = 1e-6
Q_W = N_HEADS * HEAD_DIM
KV_W = N_KV_HEADS * HEAD_DIM
LANES = 128
SUBLANES = 8
BF16_SUBLANES = 16
V_ROWS = HEAD_DIM + BF16_SUBLANES
K_COLS = 2 * HEAD_DIM

OFF_CA = 0
OFF_CB = CONV_WIDTH
OFF_Q = 2 * CONV_WIDTH
OFF_K = OFF_Q + Q_W
OFF_V = OFF_K + KV_W
OFF_G = OFF_V + KV_W

Q_SCALE = (HEAD_DIM ** -0.5) * math.log2(math.e)
SCORE_BOUND_SLACK = 1.02
MAX_FIXED_SHIFT = 60.0
BOUNDED_UNROLL = 4
HT_TILE = 512

BF16 = jnp.bfloat16
F32 = jnp.float32
MIB = 1024 * 1024


_VMEM_LIMIT_MIB = {
    "norm_glu": 56, "qt_proj": 48, "kv_proj": 48, "gate_proj": 58,
    "ffn_up": 58, "conv_branch": 48, "gqa_online": 48, "gqa_bounded": 52,
    "merge_attn_proj": 56, "out_proj_residual": 56, "ffn_down_residual": 58,
    "ple_final_norm": 56,
}


def _params(name, grid_rank):
    return pltpu.CompilerParams(dimension_semantics=("arbitrary",) * grid_rank,
                                vmem_limit_bytes=_VMEM_LIMIT_MIB[name] * MIB)


def _sigmoid(v):
    return 1.0 / (1.0 + jnp.exp(-v))


def _rms_rows(v, g):
    ms = jnp.mean(v * v, axis=-1, keepdims=True)
    return v * lax.rsqrt(ms + EPS) * g


def _cast_once(step, w_ref, wb_ref, transpose=False):
    @pl.when(step == 0)
    def _():
        w = w_ref[...]
        wb_ref[...] = (w.T if transpose else w).astype(wb_ref.dtype)


def _norm_glu_kernel(x_ref, g_ref, w_ref, u_ref, h_ref, ht_ref, wb_ref):
    _cast_once(pl.program_id(0), w_ref, wb_ref)
    y = _rms_rows(x_ref[...], g_ref[...])
    h = y.astype(BF16)
    h_ref[...] = h
    for cb in range(0, y.shape[1], LANES):
        ht_ref[0, cb:cb + LANES, :] = y[:, cb:cb + LANES].T.astype(ht_ref.dtype)
    a = jnp.dot(h, wb_ref[:, 0:CONV_WIDTH], preferred_element_type=F32)
    b = jnp.dot(h, wb_ref[:, CONV_WIDTH:2 * CONV_WIDTH], preferred_element_type=F32)
    u_ref[...] = a * _sigmoid(b)


def _norm_glu(x, g, w, tm=HT_TILE):
    s, d = x.shape
    assert OFF_CA == 0 and OFF_CB == CONV_WIDTH
    return pl.pallas_call(
        _norm_glu_kernel,
        grid=(s // tm,),
        in_specs=[pl.BlockSpec((tm, d), lambda i: (i, 0)),
                  pl.BlockSpec((1, d), lambda i: (0, 0)),
                  pl.BlockSpec((d, 2 * CONV_WIDTH), lambda i: (0, 0), pipeline_mode=pl.Buffered(1))],
        out_specs=[pl.BlockSpec((tm, CONV_WIDTH), lambda i: (i, 0)),
                   pl.BlockSpec((tm, d), lambda i: (i, 0)),
                   pl.BlockSpec((1, d, tm), lambda i: (i, 0, 0))],
        out_shape=[jax.ShapeDtypeStruct((s, CONV_WIDTH), F32),
                   jax.ShapeDtypeStruct((s, d), BF16),
                   jax.ShapeDtypeStruct((s // tm, d, tm), BF16)],
        scratch_shapes=[pltpu.VMEM((d, 2 * CONV_WIDTH), BF16)],
        compiler_params=_params("norm_glu", 1),
        name="norm_glu",
    )(x, g.reshape(1, d), w)


def _swap_axis_halves(y):
    q = AXIS_DIM // 2
    return jnp.concatenate([y[q:2 * q], y[0:q], y[3 * q:4 * q], y[2 * q:3 * q]], axis=0)


def _gained_tables(g_ref, cos_ref, sin_ref, scale):
    g = jnp.broadcast_to(g_ref[...], cos_ref.shape)
    return cos_ref[...] * (g * scale), sin_ref[...] * (_swap_axis_halves(g) * scale)


def _norm_rope_head(zh, cos_g, sin_g):
    ms = jnp.mean(zh * zh, axis=0, keepdims=True)
    y = zh * lax.rsqrt(ms + EPS)
    return y * cos_g + _swap_axis_halves(y) * sin_g


def _qt_kernel(w_ref, ht_ref, g_ref, cos_ref, sin_ref, o_ref, wt_ref):
    _cast_once(pl.program_id(1), w_ref, wt_ref, transpose=True)
    cos_g, sin_g = _gained_tables(g_ref, cos_ref, sin_ref, Q_SCALE)
    for sub in range(ht_ref.shape[0]):
        cols = slice(sub * HT_TILE, (sub + 1) * HT_TILE)
        zt = jnp.dot(wt_ref[...], ht_ref[sub], preferred_element_type=F32)
        for hh in range(zt.shape[0] // HEAD_DIM):
            rows = slice(hh * HEAD_DIM, (hh + 1) * HEAD_DIM)
            o_ref[rows, cols] = _norm_rope_head(zt[rows, :], cos_g[:, cols], sin_g[:, cols]).astype(o_ref.dtype)


def _qt_proj(w, h_t, gain, tabs_t, tm=1024, tn=1024):
    n_t, d, _ = h_t.shape
    s = n_t * HT_TILE
    cos_t, sin_t = tabs_t
    tab_spec = pl.BlockSpec((HEAD_DIM, tm), lambda j, i: (0, i))
    return pl.pallas_call(
        _qt_kernel,
        grid=(Q_W // tn, s // tm),
        in_specs=[pl.BlockSpec((d, tn), lambda j, i: (0, OFF_Q // tn + j)),
                  pl.BlockSpec((tm // HT_TILE, d, HT_TILE), lambda j, i: (i, 0, 0)),
                  pl.BlockSpec((HEAD_DIM, 1), lambda j, i: (0, 0)),
                  tab_spec, tab_spec],
        out_specs=pl.BlockSpec((tn, tm), lambda j, i: (j, i)),
        out_shape=jax.ShapeDtypeStruct((Q_W, s), BF16),
        scratch_shapes=[pltpu.VMEM((tn, d), BF16)],
        compiler_params=_params("qt_proj", 2),
        name="qt_proj",
    )(w, h_t, gain.reshape(HEAD_DIM, 1), cos_t, sin_t)


def _kv_kernel(w_ref, ht_ref, g_ref, cos_ref, sin_ref, k_ref, vt_ref, wt_ref):
    _cast_once(pl.program_id(0), w_ref, wt_ref, transpose=True)
    cos_g, sin_g = _gained_tables(g_ref, cos_ref, sin_ref, 1.0)
    lane = lax.broadcasted_iota(jnp.int32, (HT_TILE, K_COLS - HEAD_DIM), 1)
    ones_col = jnp.where(lane == 0, 1.0, 0.0).astype(k_ref.dtype)
    row = lax.broadcasted_iota(jnp.int32, (V_ROWS - HEAD_DIM, HT_TILE), 0)
    ones_row = jnp.where(row == 0, 1.0, 0.0).astype(vt_ref.dtype)
    for sub in range(ht_ref.shape[0]):
        t = slice(sub * HT_TILE, (sub + 1) * HT_TILE)
        zt = jnp.dot(wt_ref[...], ht_ref[sub], preferred_element_type=F32)
        for hh in range(N_KV_HEADS):
            r = _norm_rope_head(zt[hh * HEAD_DIM:(hh + 1) * HEAD_DIM, :], cos_g[:, t], sin_g[:, t])
            k_ref[t, hh * K_COLS:hh * K_COLS + HEAD_DIM] = r.T.astype(k_ref.dtype)
            k_ref[t, hh * K_COLS + HEAD_DIM:(hh + 1) * K_COLS] = ones_col
            v_rows = slice(KV_W + hh * HEAD_DIM, KV_W + (hh + 1) * HEAD_DIM)
            vt_ref[hh * V_ROWS:hh * V_ROWS + HEAD_DIM, t] = zt[v_rows, :].astype(vt_ref.dtype)
            vt_ref[hh * V_ROWS + HEAD_DIM:(hh + 1) * V_ROWS, t] = ones_row


def _kv_proj(w, h_t, gain, tabs_t, tm=1024):
    n_t, d, _ = h_t.shape
    s = n_t * HT_TILE
    assert OFF_V == OFF_K + KV_W and OFF_K % (2 * KV_W) == 0
    cos_t, sin_t = tabs_t
    tab_spec = pl.BlockSpec((HEAD_DIM, tm), lambda i: (0, i))
    return pl.pallas_call(
        _kv_kernel,
        grid=(s // tm,),
        in_specs=[pl.BlockSpec((d, 2 * KV_W), lambda i: (0, OFF_K // (2 * KV_W))),
                  pl.BlockSpec((tm // HT_TILE, d, HT_TILE), lambda i: (i, 0, 0)),
                  pl.BlockSpec((HEAD_DIM, 1), lambda i: (0, 0)),
                  tab_spec, tab_spec],
        out_specs=[pl.BlockSpec((tm, N_KV_HEADS * K_COLS), lambda i: (i, 0)),
                   pl.BlockSpec((N_KV_HEADS * V_ROWS, tm), lambda i: (0, i))],
        out_shape=[jax.ShapeDtypeStruct((s, N_KV_HEADS * K_COLS), BF16),
                   jax.ShapeDtypeStruct((N_KV_HEADS * V_ROWS, s), BF16)],
        scratch_shapes=[pltpu.VMEM((2 * KV_W, d), BF16)],
        compiler_params=_params("kv_proj", 1),
        name="kv_proj",
    )(w, h_t, gain.reshape(HEAD_DIM, 1), cos_t, sin_t)


def _act_kernel(h_ref, w_ref, o_ref, wb_ref, *, act):
    _cast_once(pl.program_id(1), w_ref, wb_ref)
    z = jnp.dot(h_ref[...], wb_ref[...], preferred_element_type=F32)
    if act == "sigmoid":
        z = _sigmoid(z)
    elif act == "relu2":
        z = jnp.square(jnp.maximum(z, 0.0))
    o_ref[...] = z.astype(o_ref.dtype)


def _act_proj(h, w, col_off, width, act, name, tm=2048, tn=1024):
    s, d = h.shape
    return pl.pallas_call(
        functools.partial(_act_kernel, act=act),
        grid=(width // tn, s // tm),
        in_specs=[pl.BlockSpec((tm, d), lambda j, i: (i, 0)),
                  pl.BlockSpec((d, tn), lambda j, i: (0, col_off // tn + j))],
        out_specs=pl.BlockSpec((tm, tn), lambda j, i: (i, j)),
        out_shape=jax.ShapeDtypeStruct((s, width), BF16),
        scratch_shapes=[pltpu.VMEM((d, tn), BF16)],
        compiler_params=_params(name, 2),
        name=name,
    )(h, w)


def _conv_kernel(u_ref, up_ref, un_ref, wdw_ref, lng_ref, lnb_ref, wp_ref, gate_ref,
                 o_ref, buf_ref, cv_ref, wpb_ref):
    i = pl.program_id(0)
    tm = u_ref.shape[0]
    n_slabs = CONV_WIDTH // LANES
    half = CONV_ROWS // 2
    base = CONV_HALO - CONV_KERNEL // 2
    _cast_once(i, wp_ref, wpb_ref)
    prev_ok = (i > 0).astype(F32)
    next_ok = (i < pl.num_programs(0) - 1).astype(F32)
    for sl in range(n_slabs):
        lanes = slice(sl * LANES, (sl + 1) * LANES)
        buf_ref[sl, 0:CONV_HALO, :] = up_ref[:, lanes] * prev_ok
        buf_ref[sl, CONV_HALO:CONV_HALO + tm, :] = u_ref[:, lanes]
        buf_ref[sl, CONV_HALO + tm:, :] = un_ref[:, lanes] * next_ok

    for sl in range(n_slabs):
        def body(r, carry, sl=sl):
            r0 = r * CONV_ROWS
            for par in range(2):
                acc = jnp.zeros((half, LANES), F32)
                for k in range(CONV_KERNEL):
                    tap = buf_ref[sl, pl.ds(r0 + (par + base + k), half, stride=2), :]
                    acc = acc + tap * wdw_ref[k:k + 1, sl * LANES:(sl + 1) * LANES]
                cv_ref[sl, pl.ds(r0 + par, half, stride=2), :] = acc
            return carry
        lax.fori_loop(0, tm // CONV_ROWS, body, 0)

    cv = jnp.concatenate([cv_ref[sl] for sl in range(n_slabs)], axis=1)
    mu = jnp.mean(cv, axis=-1, keepdims=True)
    xc = cv - mu
    var = jnp.mean(xc * xc, axis=-1, keepdims=True)
    y = xc * lax.rsqrt(var + EPS) * lng_ref[...] + lnb_ref[...]
    y = y * _sigmoid(y)
    yc = jnp.dot(y.astype(BF16), wpb_ref[...], preferred_element_type=F32)
    o_ref[...] = gate_ref[...].astype(F32) * yc


def _conv_branch(u, w_dw, ln_g, ln_b, w_proj, gates, tm=512):
    s = u.shape[0]
    hb = tm // CONV_HALO
    n_hblk = s // CONV_HALO
    return pl.pallas_call(
        _conv_kernel,
        grid=(s // tm,),
        in_specs=[pl.BlockSpec((tm, CONV_WIDTH), lambda i: (i, 0)),
                  pl.BlockSpec((CONV_HALO, CONV_WIDTH), lambda i: (jnp.maximum(i * hb - 1, 0), 0)),
                  pl.BlockSpec((CONV_HALO, CONV_WIDTH), lambda i: (jnp.minimum((i + 1) * hb, n_hblk - 1), 0)),
                  pl.BlockSpec((CONV_KERNEL, CONV_WIDTH), lambda i: (0, 0)),
                  pl.BlockSpec((1, CONV_WIDTH), lambda i: (0, 0)),
                  pl.BlockSpec((1, CONV_WIDTH), lambda i: (0, 0)),
                  pl.BlockSpec((CONV_WIDTH, D_MODEL), lambda i: (0, 0), pipeline_mode=pl.Buffered(1)),
                  pl.BlockSpec((tm, D_MODEL), lambda i: (i, 0))],
        out_specs=pl.BlockSpec((tm, D_MODEL), lambda i: (i, 0)),
        out_shape=jax.ShapeDtypeStruct((s, D_MODEL), F32),
        scratch_shapes=[pltpu.VMEM((CONV_WIDTH // LANES, tm + 2 * CONV_HALO, LANES), F32),
                        pltpu.VMEM((CONV_WIDTH // LANES, tm, LANES), F32),
                        pltpu.VMEM((CONV_WIDTH, D_MODEL), BF16)],
        compiler_params=_params("conv_branch", 1),
        name="conv_branch",
    )(u, u, u, w_dw, ln_g.reshape(1, -1), ln_b.reshape(1, -1), w_proj, gates)


def _stage_queries(qt_ref, qs_ref, tq, shift):
    for hh in range(GROUP):
        qs_ref[0:HEAD_DIM, hh * tq:(hh + 1) * tq] = qt_ref[hh * HEAD_DIM:(hh + 1) * HEAD_DIM, :]
    row = lax.broadcasted_iota(jnp.int32, (K_COLS - HEAD_DIM, qs_ref.shape[1]), 0)
    qs_ref[HEAD_DIM:K_COLS, :] = jnp.where(row == 0, -shift, 0.0).astype(qs_ref.dtype)


def _store_attention_out(o_t, o_ref, tq):
    for hh in range(GROUP):
        o_ref[:, hh * HEAD_DIM:(hh + 1) * HEAD_DIM] = o_t[:, hh * tq:(hh + 1) * tq].T.astype(o_ref.dtype)


def _chunk_start(c, tk):
    return c * tk if isinstance(c, int) else pl.multiple_of(c * tk, tk)


def _attn_online_kernel(b_ref, qt_ref, k_ref, vt_ref, o_ref, qs_ref, acc_ref, s0_ref, s1_ref, *, tq, tk):
    m_cols = GROUP * tq
    n_chunks = k_ref.shape[0] // tk
    assert n_chunks % 2 == 0 and n_chunks >= 2
    _stage_queries(qt_ref, qs_ref, tq, b_ref[0, 0])
    acc_ref[...] = jnp.zeros(acc_ref.shape, F32)

    def scores(c, dst_ref):
        s = jnp.dot(k_ref[pl.ds(_chunk_start(c, tk), tk), :], qs_ref[...], preferred_element_type=F32)
        dst_ref[...] = s
        return jnp.max(s, axis=0, keepdims=True)

    def update(c, src_ref, col_max, m_prev):
        m_new = jnp.maximum(m_prev, col_max)
        alpha = jnp.exp2(m_prev - m_new)
        p = jnp.exp2(src_ref[...] - m_new).astype(BF16)
        vtc = vt_ref[:, pl.ds(_chunk_start(c, tk), tk)]
        acc_ref[...] = alpha * acc_ref[...] + jnp.dot(vtc, p, preferred_element_type=F32)
        return m_new

    def pair(j, carry):
        m_run, cm0 = carry
        c = 2 * j
        cm1 = scores(c + 1, s1_ref)
        m_run = update(c, s0_ref, cm0, m_run)
        cm2 = scores(c + 2, s0_ref)
        m_run = update(c + 1, s1_ref, cm1, m_run)
        return m_run, cm2

    cm0 = scores(0, s0_ref)
    carry = (jnp.full((1, m_cols), -1e30, F32), cm0)
    m_run, cm0 = lax.fori_loop(0, n_chunks // 2 - 1, pair, carry)
    cm1 = scores(n_chunks - 1, s1_ref)
    m_run = update(n_chunks - 2, s0_ref, cm0, m_run)
    update(n_chunks - 1, s1_ref, cm1, m_run)
    _store_attention_out(acc_ref[0:HEAD_DIM, :] / acc_ref[HEAD_DIM:HEAD_DIM + 1, :], o_ref, tq)


def _attn_bounded_kernel(b_ref, qt_ref, k_ref, vt_ref, w2_ref, o_ref, w2b_ref,
                         qs_ref, acc_ref, p0_ref, p1_ref, *, tq, tk):
    m_cols = GROUP * tq
    n_chunks = k_ref.shape[0] // tk
    assert n_chunks % 2 == 0 and n_chunks >= 2
    w2b_ref[...] = w2_ref[...].astype(w2b_ref.dtype)
    _stage_queries(qt_ref, qs_ref, tq, b_ref[0, 0])
    acc_ref[...] = jnp.zeros(acc_ref.shape, F32)

    def probs(c, dst_ref, l_run):
        s = jnp.dot(k_ref[pl.ds(_chunk_start(c, tk), tk), :], qs_ref[...], preferred_element_type=F32)
        p = jnp.exp2(s)
        dst_ref[...] = p.astype(BF16)
        return l_run + jnp.sum(p, axis=0, keepdims=True)

    def accumulate(c, src_ref):
        vtc = vt_ref[0:HEAD_DIM, pl.ds(_chunk_start(c, tk), tk)]
        acc_ref[...] += jnp.dot(vtc, src_ref[...], preferred_element_type=F32)

    bufs = (p0_ref, p1_ref)
    unroll = BOUNDED_UNROLL
    assert n_chunks % unroll == 0

    def body(j, l_run):
        c = unroll * j
        for e in range(unroll):
            l_run = probs(c + e + 1, bufs[(e + 1) % 2], l_run)
            accumulate(c + e, bufs[e % 2])
        return l_run

    l_run = probs(0, p0_ref, jnp.zeros((1, m_cols), F32))
    l_run = lax.fori_loop(0, n_chunks // unroll - 1, body, l_run)
    for c in range(n_chunks - unroll, n_chunks):
        if c + 1 < n_chunks:
            l_run = probs(c + 1, bufs[(c + 1) % 2], l_run)
        accumulate(c, bufs[c % 2])
    _store_attention_out(acc_ref[...] / l_run, o_ref, tq)


def _attention(qt, k, vt, q_gain, k_gain, w_ff2, tq_online=256, tq_bounded=1024, tk=512):
    s = k.shape[0]
    gw = GROUP * HEAD_DIM
    smem = pl.BlockSpec(memory_space=pltpu.SMEM)
    attn_specs = lambda tq: [smem,
                             pl.BlockSpec((gw, tq), lambda g, i: (g, i)),
                             pl.BlockSpec((s, K_COLS), lambda g, i: (0, g)),
                             pl.BlockSpec((V_ROWS, s), lambda g, i: (g, 0))]
    out_spec = lambda tq: pl.BlockSpec((tq, gw), lambda g, i: (i, g))
    o_shape = jax.ShapeDtypeStruct((s, Q_W), BF16)

    def online(b):
        tq = tq_online
        m_cols = GROUP * tq
        o = pl.pallas_call(
            functools.partial(_attn_online_kernel, tq=tq, tk=tk),
            grid=(N_KV_HEADS, s // tq),
            in_specs=attn_specs(tq),
            out_specs=out_spec(tq),
            out_shape=o_shape,
            scratch_shapes=[pltpu.VMEM((K_COLS, m_cols), BF16),
                            pltpu.VMEM((V_ROWS, m_cols), F32),
                            pltpu.VMEM((tk, m_cols), F32),
                            pltpu.VMEM((tk, m_cols), F32)],
            compiler_params=_params("gqa_online", 2),
            name="gqa_online")(b, qt, k, vt)
        return o, w_ff2.astype(BF16)

    def bounded(b):
        tq = tq_bounded
        m_cols = GROUP * tq
        n_i = s // tq
        w2_rows = w_ff2.shape[0] // (N_KV_HEADS * n_i)
        w2_spec = pl.BlockSpec((w2_rows, w_ff2.shape[1]), lambda g, i: (g * n_i + i, 0))
        return pl.pallas_call(
            functools.partial(_attn_bounded_kernel, tq=tq, tk=tk),
            grid=(N_KV_HEADS, n_i),
            in_specs=attn_specs(tq) + [w2_spec],
            out_specs=[out_spec(tq), w2_spec],
            out_shape=[o_shape, jax.ShapeDtypeStruct(w_ff2.shape, BF16)],
            scratch_shapes=[pltpu.VMEM((K_COLS, m_cols), BF16),
                            pltpu.VMEM((HEAD_DIM, m_cols), F32),
                            pltpu.VMEM((tk, m_cols), BF16),
                            pltpu.VMEM((tk, m_cols), BF16)],
            compiler_params=_params("gqa_bounded", 2),
            name="gqa_bounded")(b, qt, k, vt, w_ff2)

    bound = (HEAD_DIM * Q_SCALE * SCORE_BOUND_SLACK
             * jnp.max(jnp.abs(q_gain)) * jnp.max(jnp.abs(k_gain))).astype(F32)
    return lax.cond(bound <= MAX_FIXED_SHIFT, bounded, online, bound.reshape(1, 1))


def _merge_kernel(o_ref, w_ref, mc_ref, ga_ref, out_ref, wb_ref):
    _cast_once(pl.program_id(1), w_ref, wb_ref)
    ya = jnp.dot(o_ref[...], wb_ref[...], preferred_element_type=F32)
    out_ref[...] = (mc_ref[...] + ga_ref[...].astype(F32) * ya).astype(out_ref.dtype)


def _merge(o, w_ap, m_c, gates, tm=1024, tn=1024):
    s, d = o.shape
    ga_off = D_MODEL // tn
    return pl.pallas_call(
        _merge_kernel,
        grid=(D_MODEL // tn, s // tm),
        in_specs=[pl.BlockSpec((tm, d), lambda j, i: (i, 0)),
                  pl.BlockSpec((d, tn), lambda j, i: (0, j)),
                  pl.BlockSpec((tm, tn), lambda j, i: (i, j)),
                  pl.BlockSpec((tm, tn), lambda j, i: (i, ga_off + j))],
        out_specs=pl.BlockSpec((tm, tn), lambda j, i: (i, j)),
        out_shape=jax.ShapeDtypeStruct((s, D_MODEL), BF16),
        scratch_shapes=[pltpu.VMEM((d, tn), BF16)],
        compiler_params=_params("merge_attn_proj", 2),
        name="merge_attn_proj",
    )(o, w_ap, m_c, gates)


def _out_kernel(a_ref, w_ref, x_ref, g_ref, x1_ref, h_ref, wb_ref):
    _cast_once(pl.program_id(0), w_ref, wb_ref)
    x1 = x_ref[...] + jnp.dot(a_ref[...], wb_ref[...], preferred_element_type=F32)
    x1_ref[...] = x1
    h_ref[...] = _rms_rows(x1, g_ref[...]).astype(h_ref.dtype)


def _out_proj(a, w, x, g, tm=512):
    s, d = x.shape
    return pl.pallas_call(
        _out_kernel,
        grid=(s // tm,),
        in_specs=[pl.BlockSpec((tm, d), lambda i: (i, 0)),
                  pl.BlockSpec((d, d), lambda i: (0, 0), pipeline_mode=pl.Buffered(1)),
                  pl.BlockSpec((tm, d), lambda i: (i, 0)),
                  pl.BlockSpec((1, d), lambda i: (0, 0))],
        out_specs=[pl.BlockSpec((tm, d), lambda i: (i, 0)),
                   pl.BlockSpec((tm, d), lambda i: (i, 0))],
        out_shape=[jax.ShapeDtypeStruct((s, d), F32),
                   jax.ShapeDtypeStruct((s, d), BF16)],
        scratch_shapes=[pltpu.VMEM((d, d), BF16)],
        compiler_params=_params("out_proj_residual", 1),
        name="out_proj_residual",
    )(a, w, x, g.reshape(1, d))


def _ffn2_kernel(a_ref, w_ref, x_ref, o_ref):
    @pl.when(pl.program_id(2) == 0)
    def _():
        o_ref[...] = x_ref[...]

    o_ref[...] += jnp.dot(a_ref[...], w_ref[...], preferred_element_type=F32)


def _ffn2(a, w, x, tm=1024, tn=1024, tk=4096):
    s, kdim = a.shape
    d = x.shape[1]
    return pl.pallas_call(
        _ffn2_kernel,
        grid=(s // tm, d // tn, kdim // tk),
        in_specs=[pl.BlockSpec((tm, tk), lambda i, j, k: (i, k)),
                  pl.BlockSpec((tk, tn), lambda i, j, k: (k, j)),
                  pl.BlockSpec((tm, tn), lambda i, j, k: (i, j))],
        out_specs=pl.BlockSpec((tm, tn), lambda i, j, k: (i, j)),
        out_shape=jax.ShapeDtypeStruct((s, d), F32),
        compiler_params=_params("ffn_down_residual", 3),
        name="ffn_down_residual",
    )(a, w, x)


def _ple_kernel(x_ref, p_ref, gp_ref, wg_ref, wp_ref, gf_ref, o_ref, wgb_ref, wpb_ref):
    _cast_once(pl.program_id(0), wg_ref, wgb_ref)
    _cast_once(pl.program_id(0), wp_ref, wpb_ref)
    x = x_ref[...]
    h = _rms_rows(x, gp_ref[...]).astype(BF16)
    gate = _sigmoid(jnp.dot(h, wgb_ref[...], preferred_element_type=F32))
    pp = jnp.dot(p_ref[...].astype(BF16), wpb_ref[...], preferred_element_type=F32)
    x3 = x + gate * pp
    o_ref[...] = _rms_rows(x3, gf_ref[...])


def _ple_final(x, p, g_ple, w_gate, w_proj, g_final, tm=512):
    s, d = x.shape
    return pl.pallas_call(
        _ple_kernel,
        grid=(s // tm,),
        in_specs=[pl.BlockSpec((tm, d), lambda i: (i, 0)),
                  pl.BlockSpec((tm, PLE_DIM), lambda i: (i, 0)),
                  pl.BlockSpec((1, d), lambda i: (0, 0)),
                  pl.BlockSpec((d, d), lambda i: (0, 0), pipeline_mode=pl.Buffered(1)),
                  pl.BlockSpec((PLE_DIM, d), lambda i: (0, 0), pipeline_mode=pl.Buffered(1)),
                  pl.BlockSpec((1, d), lambda i: (0, 0))],
        out_specs=pl.BlockSpec((tm, d), lambda i: (i, 0)),
        out_shape=jax.ShapeDtypeStruct((s, d), F32),
        scratch_shapes=[pltpu.VMEM((d, d), BF16), pltpu.VMEM((PLE_DIM, d), BF16)],
        compiler_params=_params("ple_final_norm", 1),
        name="ple_final_norm",
    )(x, p, g_ple.reshape(1, d), w_gate, w_proj, g_final.reshape(1, d))


def _rope_tables():
    n_rows = SEQ // GRID_W
    inv_freq = ROPE_THETA ** (-jnp.arange(0, AXIS_DIM, 2, dtype=F32) / AXIS_DIM)
    ang_row = inv_freq[:, None] * jnp.arange(n_rows, dtype=jnp.int32).astype(F32)[None, :]
    ang_col = inv_freq[:, None] * jnp.arange(GRID_W, dtype=jnp.int32).astype(F32)[None, :]
    nf = inv_freq.shape[0]

    def over_t(row_tab, col_tab):
        r = jnp.broadcast_to(row_tab[:, :, None], (nf, n_rows, GRID_W)).reshape(nf, SEQ)
        c = jnp.broadcast_to(col_tab[:, None, :], (nf, n_rows, GRID_W)).reshape(nf, SEQ)
        return r, c

    cr, cc = over_t(jnp.cos(ang_row), jnp.cos(ang_col))
    sr, sc = over_t(jnp.sin(ang_row), jnp.sin(ang_col))
    cos_t = jnp.concatenate([cr, cr, cc, cc], axis=0)
    sin_t = jnp.concatenate([-sr, sr, -sc, sc], axis=0)
    return cos_t, sin_t


def kernel(x, p, norm_mix, w_in, w_dw, conv_ln_g, conv_ln_b, w_conv_proj, q_norm, k_norm,
           w_attn_proj, w_out, norm_ffn, w_ff1, w_ff2, norm_ple, w_ple_gate, w_ple_proj, norm_final):
    depth = w_in.shape[0]
    assert depth == 1, "the final norm is fused into the last layer's kernel"
    tabs_t = _rope_tables()
    xs = x[0]
    for li in range(depth):
        w_i = w_in[li]
        u, h, h_t = _norm_glu(xs, norm_mix[li], w_i)
        qt = _qt_proj(w_i, h_t, q_norm[li], tabs_t)
        k, vt = _kv_proj(w_i, h_t, k_norm[li], tabs_t)
        gates = _act_proj(h, w_i, OFF_G, 2 * D_MODEL, "sigmoid", "gate_proj")
        m_c = _conv_branch(u, w_dw[li], conv_ln_g[li], conv_ln_b[li], w_conv_proj[li], gates)
        o, w_ff2_b = _attention(qt, k, vt, q_norm[li], k_norm[li], w_ff2[li])
        merged = _merge(o, w_attn_proj[li], m_c, gates)
        x1, h2 = _out_proj(merged, w_out[li], xs, norm_ffn[li])
        a = _act_proj(h2, w_ff1[li], 0, D_FF, "relu2", "ffn_up")
        x2 = _ffn2(a, w_ff2_b, x1)
        xs = _ple_final(x2, p[li, 0], norm_ple[li], w_ple_gate[li], w_ple_proj[li], norm_final)
    return xs[None]
```

```python
import functools
import math

import jax
import jax.numpy as jnp
from jax import lax
from jax.experimental import pallas as pl
from jax.experimental.pallas import tpu as pltpu

D_MODEL = 2048
SEQ = 8192
N_HEADS = 16
N_KV_HEADS = 4
HEAD_DIM = 128
GROUP = N_HEADS // N_KV_HEADS
ROPE_THETA = 10000.0
AXIS_DIM = HEAD_DIM // 2
GRID_W = 64
CONV_WIDTH = D_MODEL // 2
CONV_KERNEL = 31
CONV_HALO = 16
CONV_ROWS = 256
D_FF = 4 * D_MODEL
PLE_DIM = 256
EPS = 1e-6
Q_W = N_HEADS * HEAD_DIM
KV_W = N_KV_HEADS * HEAD_DIM
LANES = 128
SUBLANES = 8
BF16_SUBLANES = 16
V_ROWS = HEAD_DIM + BF16_SUBLANES
K_COLS = 2 * HEAD_DIM

OFF_CA = 0
OFF_CB = CONV_WIDTH
OFF_Q = 2 * CONV_WIDTH
OFF_K = OFF_Q + Q_W
OFF_V = OFF_K + KV_W
OFF_G = OFF_V + KV_W

Q_SCALE = (HEAD_DIM ** -0.5) * math.log2(math.e)
SCORE_BOUND_SLACK = 1.02
MAX_FIXED_SHIFT = 60.0
BOUNDED_UNROLL = 4
HT_TILE = 512

BF16 = jnp.bfloat16
F32 = jnp.float32
MIB = 1024 * 1024


_VMEM_LIMIT_MIB = {
    "norm_glu": 56, "qt_proj": 48, "kv_proj": 48, "gate_proj": 58,
    "ffn_up": 58, "conv_branch": 48, "gqa_online": 48, "gqa_bounded": 58,
    "merge_attn_proj": 56, "out_proj_residual": 56, "ffn_down_residual": 58,
    "ple_final_norm": 56,
}


def _params(name, grid_rank):
    return pltpu.CompilerParams(dimension_semantics=("arbitrary",) * grid_rank,
                                vmem_limit_bytes=_VMEM_LIMIT_MIB[name] * MIB)


def _sigmoid(v):
    return 1.0 / (1.0 + jnp.exp(-v))


def _rms_rows(v, g):
    ms = jnp.mean(v * v, axis=-1, keepdims=True)
    return v * lax.rsqrt(ms + EPS) * g


def _cast_once(step, w_ref, wb_ref, transpose=False):
    @pl.when(step == 0)
    def _():
        w = w_ref[...]
        wb_ref[...] = (w.T if transpose else w).astype(wb_ref.dtype)


def _norm_glu_kernel(x_ref, g_ref, w_ref, u_ref, h_ref, ht_ref, wb_ref):
    _cast_once(pl.program_id(0), w_ref, wb_ref)
    y = _rms_rows(x_ref[...], g_ref[...])
    h = y.astype(BF16)
    h_ref[...] = h
    for cb in range(0, y.shape[1], LANES):
        ht_ref[0, cb:cb + LANES, :] = y[:, cb:cb + LANES].T.astype(ht_ref.dtype)
    a = jnp.dot(h, wb_ref[:, 0:CONV_WIDTH], preferred_element_type=F32)
    b = jnp.dot(h, wb_ref[:, CONV_WIDTH:2 * CONV_WIDTH], preferred_element_type=F32)
    u_ref[...] = a * _sigmoid(b)


def _norm_glu(x, g, w, tm=HT_TILE):
    s, d = x.shape
    assert OFF_CA == 0 and OFF_CB == CONV_WIDTH
    return pl.pallas_call(
        _norm_glu_kernel,
        grid=(s // tm,),
        in_specs=[pl.BlockSpec((tm, d), lambda i: (i, 0)),
                  pl.BlockSpec((1, d), lambda i: (0, 0)),
                  pl.BlockSpec((d, 2 * CONV_WIDTH), lambda i: (0, 0), pipeline_mode=pl.Buffered(1))],
        out_specs=[pl.BlockSpec((tm, CONV_WIDTH), lambda i: (i, 0)),
                   pl.BlockSpec((tm, d), lambda i: (i, 0)),
                   pl.BlockSpec((1, d, tm), lambda i: (i, 0, 0))],
        out_shape=[jax.ShapeDtypeStruct((s, CONV_WIDTH), F32),
                   jax.ShapeDtypeStruct((s, d), BF16),
                   jax.ShapeDtypeStruct((s // tm, d, tm), BF16)],
        scratch_shapes=[pltpu.VMEM((d, 2 * CONV_WIDTH), BF16)],
        compiler_params=_params("norm_glu", 1),
        name="norm_glu",
    )(x, g.reshape(1, d), w)


def _swap_axis_halves(y):
    q = AXIS_DIM // 2
    return jnp.concatenate([y[q:2 * q], y[0:q], y[3 * q:4 * q], y[2 * q:3 * q]], axis=0)


def _gained_tables(g_ref, cos_ref, sin_ref, scale):
    g = jnp.broadcast_to(g_ref[...], cos_ref.shape)
    return cos_ref[...] * (g * scale), sin_ref[...] * (_swap_axis_halves(g) * scale)


def _norm_rope_head(zh, cos_g, sin_g):
    ms = jnp.mean(zh * zh, axis=0, keepdims=True)
    y = zh * lax.rsqrt(ms + EPS)
    return y * cos_g + _swap_axis_halves(y) * sin_g


def _qt_kernel(w_ref, ht_ref, g_ref, cos_ref, sin_ref, o_ref, wt_ref):
    _cast_once(pl.program_id(1), w_ref, wt_ref, transpose=True)
    cos_g, sin_g = _gained_tables(g_ref, cos_ref, sin_ref, Q_SCALE)
    for sub in range(ht_ref.shape[0]):
        cols = slice(sub * HT_TILE, (sub + 1) * HT_TILE)
        zt = jnp.dot(wt_ref[...], ht_ref[sub], preferred_element_type=F32)
        for hh in range(zt.shape[0] // HEAD_DIM):
            rows = slice(hh * HEAD_DIM, (hh + 1) * HEAD_DIM)
            o_ref[rows, cols] = _norm_rope_head(zt[rows, :], cos_g[:, cols], sin_g[:, cols]).astype(o_ref.dtype)


def _qt_proj(w, h_t, gain, tabs_t, tm=1024, tn=1024):
    n_t, d, _ = h_t.shape
    s = n_t * HT_TILE
    cos_t, sin_t = tabs_t
    tab_spec = pl.BlockSpec((HEAD_DIM, tm), lambda j, i: (0, i))
    return pl.pallas_call(
        _qt_kernel,
        grid=(Q_W // tn, s // tm),
        in_specs=[pl.BlockSpec((d, tn), lambda j, i: (0, OFF_Q // tn + j)),
                  pl.BlockSpec((tm // HT_TILE, d, HT_TILE), lambda j, i: (i, 0, 0)),
                  pl.BlockSpec((HEAD_DIM, 1), lambda j, i: (0, 0)),
                  tab_spec, tab_spec],
        out_specs=pl.BlockSpec((tn, tm), lambda j, i: (j, i)),
        out_shape=jax.ShapeDtypeStruct((Q_W, s), BF16),
        scratch_shapes=[pltpu.VMEM((tn, d), BF16)],
        compiler_params=_params("qt_proj", 2),
        name="qt_proj",
    )(w, h_t, gain.reshape(HEAD_DIM, 1), cos_t, sin_t)


def _kv_kernel(w_ref, ht_ref, g_ref, cos_ref, sin_ref, k_ref, vt_ref, wt_ref):
    _cast_once(pl.program_id(0), w_ref, wt_ref, transpose=True)
    cos_g, sin_g = _gained_tables(g_ref, cos_ref, sin_ref, 1.0)
    lane = lax.broadcasted_iota(jnp.int32, (HT_TILE, K_COLS - HEAD_DIM), 1)
    ones_col = jnp.where(lane == 0, 1.0, 0.0).astype(k_ref.dtype)
    row = lax.broadcasted_iota(jnp.int32, (V_ROWS - HEAD_DIM, HT_TILE), 0)
    ones_row = jnp.where(row == 0, 1.0, 0.0).astype(vt_ref.dtype)
    for sub in range(ht_ref.shape[0]):
        t = slice(sub * HT_TILE, (sub + 1) * HT_TILE)
        zt = jnp.dot(wt_ref[...], ht_ref[sub], preferred_element_type=F32)
        for hh in range(N_KV_HEADS):
            r = _norm_rope_head(zt[hh * HEAD_DIM:(hh + 1) * HEAD_DIM, :], cos_g[:, t], sin_g[:, t])
            k_ref[t, hh * K_COLS:hh * K_COLS + HEAD_DIM] = r.T.astype(k_ref.dtype)
            k_ref[t, hh * K_COLS + HEAD_DIM:(hh + 1) * K_COLS] = ones_col
            v_rows = slice(KV_W + hh * HEAD_DIM, KV_W + (hh + 1) * HEAD_DIM)
            vt_ref[hh * V_ROWS:hh * V_ROWS + HEAD_DIM, t] = zt[v_rows, :].astype(vt_ref.dtype)
            vt_ref[hh * V_ROWS + HEAD_DIM:(hh + 1) * V_ROWS, t] = ones_row


def _kv_proj(w, h_t, gain, tabs_t, tm=1024):
    n_t, d, _ = h_t.shape
    s = n_t * HT_TILE
    assert OFF_V == OFF_K + KV_W and OFF_K % (2 * KV_W) == 0
    cos_t, sin_t = tabs_t
    tab_spec = pl.BlockSpec((HEAD_DIM, tm), lambda i: (0, i))
    return pl.pallas_call(
        _kv_kernel,
        grid=(s // tm,),
        in_specs=[pl.BlockSpec((d, 2 * KV_W), lambda i: (0, OFF_K // (2 * KV_W))),
                  pl.BlockSpec((tm // HT_TILE, d, HT_TILE), lambda i: (i, 0, 0)),
                  pl.BlockSpec((HEAD_DIM, 1), lambda i: (0, 0)),
                  tab_spec, tab_spec],
        out_specs=[pl.BlockSpec((tm, N_KV_HEADS * K_COLS), lambda i: (i, 0)),
                   pl.BlockSpec((N_KV_HEADS * V_ROWS, tm), lambda i: (0, i))],
        out_shape=[jax.ShapeDtypeStruct((s, N_KV_HEADS * K_COLS), BF16),
                   jax.ShapeDtypeStruct((N_KV_HEADS * V_ROWS, s), BF16)],
        scratch_shapes=[pltpu.VMEM((2 * KV_W, d), BF16)],
        compiler_params=_params("kv_proj", 1),
        name="kv_proj",
    )(w, h_t, gain.reshape(HEAD_DIM, 1), cos_t, sin_t)


def _act_kernel(h_ref, w_ref, o_ref, *scratch, act):
    if scratch:
        _cast_once(pl.program_id(1), w_ref, scratch[0])
        w_ref = scratch[0]
    z = jnp.dot(h_ref[...], w_ref[...], preferred_element_type=F32)
    if act == "sigmoid":
        z = _sigmoid(z)
    elif act == "relu2":
        z = jnp.square(jnp.maximum(z, 0.0))
    o_ref[...] = z.astype(o_ref.dtype)


def _act_proj(h, w, col_off, width, act, name, tm=2048, tn=1024):
    s, d = h.shape
    return pl.pallas_call(
        functools.partial(_act_kernel, act=act),
        grid=(width // tn, s // tm),
        in_specs=[pl.BlockSpec((tm, d), lambda j, i: (i, 0)),
                  pl.BlockSpec((d, tn), lambda j, i: (0, col_off // tn + j))],
        out_specs=pl.BlockSpec((tm, tn), lambda j, i: (i, j)),
        out_shape=jax.ShapeDtypeStruct((s, width), BF16),
        scratch_shapes=[pltpu.VMEM((d, tn), BF16)] if w.dtype == F32 else [],
        compiler_params=_params(name, 2),
        name=name,
    )(h, w)


def _gate_lagged_kernel(h_ref, w_ref, o_ref, wb_ref, z0_ref, z1_ref):
    i = pl.program_id(1)
    n_tiles = pl.num_programs(1) - 1
    _cast_once(i, w_ref, wb_ref)

    def matmul(z_ref):
        z_ref[...] = jnp.dot(h_ref[...], wb_ref[...], preferred_element_type=F32)

    def epilogue(z_ref):
        o_ref[...] = _sigmoid(z_ref[...]).astype(o_ref.dtype)

    odd = lax.rem(i, 2) == 1
    middle = (i > 0) & (i < n_tiles)

    @pl.when(i == 0)
    def _():
        matmul(z0_ref)

    @pl.when(middle & odd)
    def _():
        epilogue(z0_ref)
        matmul(z1_ref)

    @pl.when(middle & jnp.logical_not(odd))
    def _():
        epilogue(z1_ref)
        matmul(z0_ref)

    @pl.when(i == n_tiles)
    def _():
        epilogue(z1_ref)


def _gate_proj_lagged(h, w, col_off, width, tm=1024, tn=1024):
    s, d = h.shape
    n_tiles = s // tm
    assert n_tiles % 2 == 0
    return pl.pallas_call(
        _gate_lagged_kernel,
        grid=(width // tn, n_tiles + 1),
        in_specs=[pl.BlockSpec((tm, d), lambda j, i: (jnp.minimum(i, n_tiles - 1), 0)),
                  pl.BlockSpec((d, tn), lambda j, i: (0, col_off // tn + j))],
        out_specs=pl.BlockSpec((tm, tn), lambda j, i: (jnp.maximum(i - 1, 0), j)),
        out_shape=jax.ShapeDtypeStruct((s, width), BF16),
        scratch_shapes=[pltpu.VMEM((d, tn), BF16),
                        pltpu.VMEM((tm, tn), F32), pltpu.VMEM((tm, tn), F32)],
        compiler_params=_params("gate_proj", 2),
        name="gate_proj",
    )(h, w)


def _conv_kernel(u_ref, up_ref, un_ref, wdw_ref, lng_ref, lnb_ref, wp_ref, gate_ref,
                 o_ref, buf_ref, cv_ref):
    i = pl.program_id(0)
    tm = u_ref.shape[0]
    n_slabs = CONV_WIDTH // LANES
    half = CONV_ROWS // 2
    base = CONV_HALO - CONV_KERNEL // 2
    prev_ok = (i > 0).astype(F32)
    next_ok = (i < pl.num_programs(0) - 1).astype(F32)
    for sl in range(n_slabs):
        lanes = slice(sl * LANES, (sl + 1) * LANES)
        buf_ref[sl, 0:CONV_HALO, :] = up_ref[:, lanes] * prev_ok
        buf_ref[sl, CONV_HALO:CONV_HALO + tm, :] = u_ref[:, lanes]
        buf_ref[sl, CONV_HALO + tm:, :] = un_ref[:, lanes] * next_ok

    for sl in range(n_slabs):
        def body(r, carry, sl=sl):
            r0 = r * CONV_ROWS
            for par in range(2):
                acc = jnp.zeros((half, LANES), F32)
                for k in range(CONV_KERNEL):
                    tap = buf_ref[sl, pl.ds(r0 + (par + base + k), half, stride=2), :]
                    acc = acc + tap * wdw_ref[k:k + 1, sl * LANES:(sl + 1) * LANES]
                cv_ref[sl, pl.ds(r0 + par, half, stride=2), :] = acc
            return carry
        lax.fori_loop(0, tm // CONV_ROWS, body, 0)

    cv = jnp.concatenate([cv_ref[sl] for sl in range(n_slabs)], axis=1)
    mu = jnp.mean(cv, axis=-1, keepdims=True)
    xc = cv - mu
    var = jnp.mean(xc * xc, axis=-1, keepdims=True)
    y = xc * lax.rsqrt(var + EPS) * lng_ref[...] + lnb_ref[...]
    y = y * _sigmoid(y)
    yc = jnp.dot(y.astype(BF16), wp_ref[...], preferred_element_type=F32)
    o_ref[...] = gate_ref[...].astype(F32) * yc


def _conv_branch(u, w_dw, ln_g, ln_b, w_proj, gates, tm=512):
    s = u.shape[0]
    hb = tm // CONV_HALO
    n_hblk = s // CONV_HALO
    return pl.pallas_call(
        _conv_kernel,
        grid=(s // tm,),
        in_specs=[pl.BlockSpec((tm, CONV_WIDTH), lambda i: (i, 0)),
                  pl.BlockSpec((CONV_HALO, CONV_WIDTH), lambda i: (jnp.maximum(i * hb - 1, 0), 0)),
                  pl.BlockSpec((CONV_HALO, CONV_WIDTH), lambda i: (jnp.minimum((i + 1) * hb, n_hblk - 1), 0)),
                  pl.BlockSpec((CONV_KERNEL, CONV_WIDTH), lambda i: (0, 0)),
                  pl.BlockSpec((1, CONV_WIDTH), lambda i: (0, 0)),
                  pl.BlockSpec((1, CONV_WIDTH), lambda i: (0, 0)),
                  pl.BlockSpec((CONV_WIDTH, D_MODEL), lambda i: (0, 0), pipeline_mode=pl.Buffered(1)),
                  pl.BlockSpec((tm, D_MODEL), lambda i: (i, 0))],
        out_specs=pl.BlockSpec((tm, D_MODEL), lambda i: (i, 0)),
        out_shape=jax.ShapeDtypeStruct((s, D_MODEL), F32),
        scratch_shapes=[pltpu.VMEM((CONV_WIDTH // LANES, tm + 2 * CONV_HALO, LANES), F32),
                        pltpu.VMEM((CONV_WIDTH // LANES, tm, LANES), F32)],
        compiler_params=_params("conv_branch", 1),
        name="conv_branch",
    )(u, u, u, w_dw, ln_g.reshape(1, -1), ln_b.reshape(1, -1), w_proj, gates)


def _stage_queries(qt_ref, qs_ref, tq, shift):
    for hh in range(GROUP):
        qs_ref[0:HEAD_DIM, hh * tq:(hh + 1) * tq] = qt_ref[hh * HEAD_DIM:(hh + 1) * HEAD_DIM, :]
    row = lax.broadcasted_iota(jnp.int32, (K_COLS - HEAD_DIM, qs_ref.shape[1]), 0)
    qs_ref[HEAD_DIM:K_COLS, :] = jnp.where(row == 0, -shift, 0.0).astype(qs_ref.dtype)


def _store_attention_out(o_t, o_ref, tq):
    for hh in range(GROUP):
        o_ref[:, hh * HEAD_DIM:(hh + 1) * HEAD_DIM] = o_t[:, hh * tq:(hh + 1) * tq].T.astype(o_ref.dtype)


def _chunk_start(c, tk):
    return c * tk if isinstance(c, int) else pl.multiple_of(c * tk, tk)


def _attn_online_kernel(b_ref, qt_ref, k_ref, vt_ref, o_ref, qs_ref, acc_ref, s0_ref, s1_ref, *, tq, tk):
    m_cols = GROUP * tq
    n_chunks = k_ref.shape[0] // tk
    assert n_chunks % 2 == 0 and n_chunks >= 2
    _stage_queries(qt_ref, qs_ref, tq, b_ref[0, 0])
    acc_ref[...] = jnp.zeros(acc_ref.shape, F32)

    def scores(c, dst_ref):
        s = jnp.dot(k_ref[pl.ds(_chunk_start(c, tk), tk), :], qs_ref[...], preferred_element_type=F32)
        dst_ref[...] = s
        return jnp.max(s, axis=0, keepdims=True)

    def update(c, src_ref, col_max, m_prev):
        m_new = jnp.maximum(m_prev, col_max)
        alpha = jnp.exp2(m_prev - m_new)
        p = jnp.exp2(src_ref[...] - m_new).astype(BF16)
        vtc = vt_ref[:, pl.ds(_chunk_start(c, tk), tk)]
        acc_ref[...] = alpha * acc_ref[...] + jnp.dot(vtc, p, preferred_element_type=F32)
        return m_new

    def pair(j, carry):
        m_run, cm0 = carry
        c = 2 * j
        cm1 = scores(c + 1, s1_ref)
        m_run = update(c, s0_ref, cm0, m_run)
        cm2 = scores(c + 2, s0_ref)
        m_run = update(c + 1, s1_ref, cm1, m_run)
        return m_run, cm2

    cm0 = scores(0, s0_ref)
    carry = (jnp.full((1, m_cols), -1e30, F32), cm0)
    m_run, cm0 = lax.fori_loop(0, n_chunks // 2 - 1, pair, carry)
    cm1 = scores(n_chunks - 1, s1_ref)
    m_run = update(n_chunks - 2, s0_ref, cm0, m_run)
    update(n_chunks - 1, s1_ref, cm1, m_run)
    _store_attention_out(acc_ref[0:HEAD_DIM, :] / acc_ref[HEAD_DIM:HEAD_DIM + 1, :], o_ref, tq)


def _attn_bounded_kernel(b_ref, qt_ref, k_ref, vt_ref, *rest, tq, tk, n_cast):
    w_refs, o_ref, wb_refs = rest[:n_cast], rest[n_cast], rest[n_cast + 1:2 * n_cast + 1]
    qs_ref, acc_ref, p0_ref, p1_ref = rest[2 * n_cast + 1:]
    m_cols = GROUP * tq
    n_chunks = k_ref.shape[0] // tk
    assert n_chunks % 2 == 0 and n_chunks >= 2
    for w_ref, wb_ref in zip(w_refs, wb_refs):
        wb_ref[...] = w_ref[...].astype(wb_ref.dtype)
    _stage_queries(qt_ref, qs_ref, tq, b_ref[0, 0])
    acc_ref[...] = jnp.zeros(acc_ref.shape, F32)

    def probs(c, dst_ref, l_run):
        s = jnp.dot(k_ref[pl.ds(_chunk_start(c, tk), tk), :], qs_ref[...], preferred_element_type=F32)
        p = jnp.exp2(s)
        dst_ref[...] = p.astype(BF16)
        return l_run + jnp.sum(p, axis=0, keepdims=True)

    def accumulate(c, src_ref):
        vtc = vt_ref[0:HEAD_DIM, pl.ds(_chunk_start(c, tk), tk)]
        acc_ref[...] += jnp.dot(vtc, src_ref[...], preferred_element_type=F32)

    bufs = (p0_ref, p1_ref)
    unroll = BOUNDED_UNROLL
    assert n_chunks % unroll == 0

    def body(j, l_run):
        c = unroll * j
        for e in range(unroll):
            l_run = probs(c + e + 1, bufs[(e + 1) % 2], l_run)
            accumulate(c + e, bufs[e % 2])
        return l_run

    l_run = probs(0, p0_ref, jnp.zeros((1, m_cols), F32))
    l_run = lax.fori_loop(0, n_chunks // unroll - 1, body, l_run)
    for c in range(n_chunks - unroll, n_chunks):
        if c + 1 < n_chunks:
            l_run = probs(c + 1, bufs[(c + 1) % 2], l_run)
        accumulate(c, bufs[c % 2])
    _store_attention_out(acc_ref[...] / l_run, o_ref, tq)


def _attention(qt, k, vt, q_gain, k_gain, cast_weights, tq_online=256, tq_bounded=1024, tk=512):
    s = k.shape[0]
    gw = GROUP * HEAD_DIM
    smem = pl.BlockSpec(memory_space=pltpu.SMEM)
    attn_specs = lambda tq: [smem,
                             pl.BlockSpec((gw, tq), lambda g, i: (g, i)),
                             pl.BlockSpec((s, K_COLS), lambda g, i: (0, g)),
                             pl.BlockSpec((V_ROWS, s), lambda g, i: (g, 0))]
    out_spec = lambda tq: pl.BlockSpec((tq, gw), lambda g, i: (i, g))
    o_shape = jax.ShapeDtypeStruct((s, Q_W), BF16)

    def online(b):
        tq = tq_online
        m_cols = GROUP * tq
        o = pl.pallas_call(
            functools.partial(_attn_online_kernel, tq=tq, tk=tk),
            grid=(N_KV_HEADS, s // tq),
            in_specs=attn_specs(tq),
            out_specs=out_spec(tq),
            out_shape=o_shape,
            scratch_shapes=[pltpu.VMEM((K_COLS, m_cols), BF16),
                            pltpu.VMEM((V_ROWS, m_cols), F32),
                            pltpu.VMEM((tk, m_cols), F32),
                            pltpu.VMEM((tk, m_cols), F32)],
            compiler_params=_params("gqa_online", 2),
            name="gqa_online")(b, qt, k, vt)
        return o, tuple(w.astype(BF16) for w in cast_weights)

    def bounded(b):
        tq = tq_bounded
        m_cols = GROUP * tq
        n_i = s // tq
        n_steps = N_KV_HEADS * n_i
        assert all(w.shape[0] % (n_steps * BF16_SUBLANES) == 0 for w in cast_weights)
        w_specs = [pl.BlockSpec((w.shape[0] // n_steps, w.shape[1]), lambda g, i: (g * n_i + i, 0))
                   for w in cast_weights]
        outs = pl.pallas_call(
            functools.partial(_attn_bounded_kernel, tq=tq, tk=tk, n_cast=len(cast_weights)),
            grid=(N_KV_HEADS, n_i),
            in_specs=attn_specs(tq) + w_specs,
            out_specs=[out_spec(tq)] + w_specs,
            out_shape=[o_shape] + [jax.ShapeDtypeStruct(w.shape, BF16) for w in cast_weights],
            scratch_shapes=[pltpu.VMEM((K_COLS, m_cols), BF16),
                            pltpu.VMEM((HEAD_DIM, m_cols), F32),
                            pltpu.VMEM((tk, m_cols), BF16),
                            pltpu.VMEM((tk, m_cols), BF16)],
            compiler_params=_params("gqa_bounded", 2),
            name="gqa_bounded")(b, qt, k, vt, *cast_weights)
        return outs[0], tuple(outs[1:])

    bound = (HEAD_DIM * Q_SCALE * SCORE_BOUND_SLACK
             * jnp.max(jnp.abs(q_gain)) * jnp.max(jnp.abs(k_gain))).astype(F32)
    return lax.cond(bound <= MAX_FIXED_SHIFT, bounded, online, bound.reshape(1, 1))


def _merge_kernel(o_ref, w_ref, mc_ref, ga_ref, out_ref):
    ya = jnp.dot(o_ref[...], w_ref[...], preferred_element_type=F32)
    out_ref[...] = (mc_ref[...] + ga_ref[...].astype(F32) * ya).astype(out_ref.dtype)


def _merge(o, w_ap, m_c, gates, tm=1024, tn=1024):
    s, d = o.shape
    ga_off = D_MODEL // tn
    return pl.pallas_call(
        _merge_kernel,
        grid=(D_MODEL // tn, s // tm),
        in_specs=[pl.BlockSpec((tm, d), lambda j, i: (i, 0)),
                  pl.BlockSpec((d, tn), lambda j, i: (0, j)),
                  pl.BlockSpec((tm, tn), lambda j, i: (i, j)),
                  pl.BlockSpec((tm, tn), lambda j, i: (i, ga_off + j))],
        out_specs=pl.BlockSpec((tm, tn), lambda j, i: (i, j)),
        out_shape=jax.ShapeDtypeStruct((s, D_MODEL), BF16),
        compiler_params=_params("merge_attn_proj", 2),
        name="merge_attn_proj",
    )(o, w_ap, m_c, gates)


def _out_kernel(a_ref, w_ref, x_ref, g_ref, x1_ref, h_ref):
    x1 = x_ref[...] + jnp.dot(a_ref[...], w_ref[...], preferred_element_type=F32)
    x1_ref[...] = x1
    h_ref[...] = _rms_rows(x1, g_ref[...]).astype(h_ref.dtype)


def _out_proj(a, w, x, g, tm=512):
    s, d = x.shape
    return pl.pallas_call(
        _out_kernel,
        grid=(s // tm,),
        in_specs=[pl.BlockSpec((tm, d), lambda i: (i, 0)),
                  pl.BlockSpec((d, d), lambda i: (0, 0), pipeline_mode=pl.Buffered(1)),
                  pl.BlockSpec((tm, d), lambda i: (i, 0)),
                  pl.BlockSpec((1, d), lambda i: (0, 0))],
        out_specs=[pl.BlockSpec((tm, d), lambda i: (i, 0)),
                   pl.BlockSpec((tm, d), lambda i: (i, 0))],
        out_shape=[jax.ShapeDtypeStruct((s, d), F32),
                   jax.ShapeDtypeStruct((s, d), BF16)],
        compiler_params=_params("out_proj_residual", 1),
        name="out_proj_residual",
    )(a, w, x, g.reshape(1, d))


def _ffn2_kernel(a_ref, w_ref, x_ref, o_ref):
    @pl.when(pl.program_id(2) == 0)
    def _():
        o_ref[...] = x_ref[...]

    o_ref[...] += jnp.dot(a_ref[...], w_ref[...], preferred_element_type=F32)


def _ffn2(a, w, x, tm=1024, tn=1024, tk=4096):
    s, kdim = a.shape
    d = x.shape[1]
    return pl.pallas_call(
        _ffn2_kernel,
        grid=(s // tm, d // tn, kdim // tk),
        in_specs=[pl.BlockSpec((tm, tk), lambda i, j, k: (i, k)),
                  pl.BlockSpec((tk, tn), lambda i, j, k: (k, j)),
                  pl.BlockSpec((tm, tn), lambda i, j, k: (i, j))],
        out_specs=pl.BlockSpec((tm, tn), lambda i, j, k: (i, j)),
        out_shape=jax.ShapeDtypeStruct((s, d), F32),
        compiler_params=_params("ffn_down_residual", 3),
        name="ffn_down_residual",
    )(a, w, x)


def _ple_kernel(x_ref, p_ref, gp_ref, wg_ref, wp_ref, gf_ref, o_ref, wpb_ref):
    _cast_once(pl.program_id(0), wp_ref, wpb_ref)
    x = x_ref[...]
    h = _rms_rows(x, gp_ref[...]).astype(BF16)
    gate = _sigmoid(jnp.dot(h, wg_ref[...], preferred_element_type=F32))
    pp = jnp.dot(p_ref[...].astype(BF16), wpb_ref[...], preferred_element_type=F32)
    x3 = x + gate * pp
    o_ref[...] = _rms_rows(x3, gf_ref[...])


def _ple_final(x, p, g_ple, w_gate, w_proj, g_final, tm=512):
    s, d = x.shape
    return pl.pallas_call(
        _ple_kernel,
        grid=(s // tm,),
        in_specs=[pl.BlockSpec((tm, d), lambda i: (i, 0)),
                  pl.BlockSpec((tm, PLE_DIM), lambda i: (i, 0)),
                  pl.BlockSpec((1, d), lambda i: (0, 0)),
                  pl.BlockSpec((d, d), lambda i: (0, 0), pipeline_mode=pl.Buffered(1)),
                  pl.BlockSpec((PLE_DIM, d), lambda i: (0, 0), pipeline_mode=pl.Buffered(1)),
                  pl.BlockSpec((1, d), lambda i: (0, 0))],
        out_specs=pl.BlockSpec((tm, d), lambda i: (i, 0)),
        out_shape=jax.ShapeDtypeStruct((s, d), F32),
        scratch_shapes=[pltpu.VMEM((PLE_DIM, d), BF16)],
        compiler_params=_params("ple_final_norm", 1),
        name="ple_final_norm",
    )(x, p, g_ple.reshape(1, d), w_gate, w_proj, g_final.reshape(1, d))


def _rope_tables():
    n_rows = SEQ // GRID_W
    inv_freq = ROPE_THETA ** (-jnp.arange(0, AXIS_DIM, 2, dtype=F32) / AXIS_DIM)
    ang_row = inv_freq[:, None] * jnp.arange(n_rows, dtype=jnp.int32).astype(F32)[None, :]
    ang_col = inv_freq[:, None] * jnp.arange(GRID_W, dtype=jnp.int32).astype(F32)[None, :]
    nf = inv_freq.shape[0]

    def over_t(row_tab, col_tab):
        r = jnp.broadcast_to(row_tab[:, :, None], (nf, n_rows, GRID_W)).reshape(nf, SEQ)
        c = jnp.broadcast_to(col_tab[:, None, :], (nf, n_rows, GRID_W)).reshape(nf, SEQ)
        return r, c

    cr, cc = over_t(jnp.cos(ang_row), jnp.cos(ang_col))
    sr, sc = over_t(jnp.sin(ang_row), jnp.sin(ang_col))
    cos_t = jnp.concatenate([cr, cr, cc, cc], axis=0)
    sin_t = jnp.concatenate([-sr, sr, -sc, sc], axis=0)
    return cos_t, sin_t


def kernel(x, p, norm_mix, w_in, w_dw, conv_ln_g, conv_ln_b, w_conv_proj, q_norm, k_norm,
           w_attn_proj, w_out, norm_ffn, w_ff1, w_ff2, norm_ple, w_ple_gate, w_ple_proj, norm_final):
    depth = w_in.shape[0]
    assert depth == 1, "the final norm is fused into the last layer's kernel"
    tabs_t = _rope_tables()
    xs = x[0]
    for li in range(depth):
        w_i = w_in[li]
        u, h, h_t = _norm_glu(xs, norm_mix[li], w_i)
        qt = _qt_proj(w_i, h_t, q_norm[li], tabs_t)
        k, vt = _kv_proj(w_i, h_t, k_norm[li], tabs_t)
        gates = _gate_proj_lagged(h, w_i, OFF_G, 2 * D_MODEL)
        o, (w_ff2_b, w_ap_b, w_out_b, w_pg_b, w_ff1_b, w_cp_b) = _attention(
            qt, k, vt, q_norm[li], k_norm[li],
            (w_ff2[li], w_attn_proj[li], w_out[li], w_ple_gate[li], w_ff1[li], w_conv_proj[li]))
        m_c = _conv_branch(u, w_dw[li], conv_ln_g[li], conv_ln_b[li], w_cp_b, gates)
        merged = _merge(o, w_ap_b, m_c, gates)
        x1, h2 = _out_proj(merged, w_out_b, xs, norm_ffn[li])
        a = _act_proj(h2, w_ff1_b, 0, D_FF, "relu2", "ffn_up")
        x2 = _ffn2(a, w_ff2_b, x1)
        xs = _ple_final(x2, p[li, 0], norm_ple[li], w_pg_b, w_ple_proj[li], norm_final)
    return xs[None]
```
